```python
import jax
import jax.numpy as jnp
from jax import lax
import numpy as np

D_MODEL = 2048
BATCH = 4
SEQ = 2048
DEPTH = 2

CTX_LEN = 256
GRID_W = 64
ROPE_THETA = 10000.0
NORM_EPS = 1e-6
Q_BLOCK = 128
MOD_CHUNKS = 6

MLA_HEADS = 4
MLA_Q_LORA = 512
MLA_KV_LORA = 512
MLA_NOPE = 128
MLA_ROPE = 64
MLA_V = 128
MLA_SCALE = (MLA_NOPE + MLA_ROPE) ** -0.5

GQA_HEADS = 4
GQA_KV_HEADS = 2
GQA_HEAD_DIM = 128
GQA_SCALE = GQA_HEAD_DIM ** -0.5

NA_HEADS = 4
NA_HEAD_DIM = 128
NA_WIN_H_MAX = 8
NA_WIN_W = 16
NA_SCALE = NA_HEAD_DIM ** -0.5

CONV_CH = 512
CONV_WIDTH = 3

BRANCH_WIDTH = 512
N_BRANCHES = 4

A_COLS = MLA_Q_LORA + MLA_KV_LORA + MLA_ROPE
B_COLS = (GQA_HEADS + 2 * GQA_KV_HEADS) * GQA_HEAD_DIM
C_COLS = 3 * NA_HEADS * NA_HEAD_DIM
D_COLS = 3 * CONV_CH
G_COLS = N_BRANCHES * D_MODEL
IN_COLS = A_COLS + B_COLS + C_COLS + D_COLS + G_COLS
IN_CUTS = [A_COLS, A_COLS + B_COLS, A_COLS + B_COLS + C_COLS, A_COLS + B_COLS + C_COLS + D_COLS]

N_GROUPS = 8
EXPERTS_PER_GROUP = 8
N_EXPERTS = N_GROUPS * EXPERTS_PER_GROUP
TOP_K = 2
EXPERT_HIDDEN = 512
MOE_BLOCK = 128

kernel_name = 'hybrid_parallel_dit_ctx_prefix'


def rms_norm(x, g):
    xf = x.astype(jnp.float32)
    y = xf * lax.rsqrt(jnp.mean(xf * xf, axis=-1, keepdims=True) + NORM_EPS)
    return (y * g.astype(jnp.float32)).astype(x.dtype)


def modulate(h, shift, scale):
    return h * (1 + scale) + shift


def axial_rope_tables(n_tokens, rot_dim):
    quarter = rot_dim // 4
    t = jnp.arange(n_tokens)
    rows = (t // GRID_W).astype(jnp.float32)
    cols = (t % GRID_W).astype(jnp.float32)
    inv_freq = ROPE_THETA ** (-jnp.arange(quarter, dtype=jnp.float32) / quarter)
    ang_r = rows[:, None] * inv_freq[None, :]
    ang_c = cols[:, None] * inv_freq[None, :]
    return (jnp.cos(ang_r), jnp.sin(ang_r), jnp.cos(ang_c), jnp.sin(ang_c))


def _rotate_half(x, cos, sin):
    x1, x2 = jnp.split(x, 2, axis=-1)
    return jnp.concatenate([x1 * cos - x2 * sin, x1 * sin + x2 * cos], axis=-1)


def apply_axial_rope(x, tables):
    cos_r, sin_r, cos_c, sin_c = [t[:, None, :] for t in tables]
    xr, xc = jnp.split(x.astype(jnp.float32), 2, axis=-1)
    out = jnp.concatenate([_rotate_half(xr, cos_r, sin_r), _rotate_half(xc, cos_c, sin_c)], axis=-1)
    return out.astype(x.dtype)


def attention_core(q, k, v, scale):
    s = jnp.einsum('bqgrd,bkgd->bgrqk', q, k).astype(jnp.float32) * scale
    p = jax.nn.softmax(s, axis=-1).astype(v.dtype)
    return jnp.einsum('bgrqk,bkgd->bqgrd', p, v)


def blocked_attention(q, k, v, scale):
    bsz, n = q.shape[:2]
    nb = n // Q_BLOCK
    qb = jnp.moveaxis(q.reshape((bsz, nb, Q_BLOCK) + q.shape[2:]), 1, 0)
    ob = lax.map(lambda qi: attention_core(qi, k, v, scale), qb)
    return jnp.moveaxis(ob, 0, 1).reshape((bsz, n) + ob.shape[3:])


def mla_project(p, w_uq, g_qa, w_ukv, g_kva, rope):
    bsz, n = p.shape[:2]
    c_q, c_kv, k_pe = jnp.split(p, [MLA_Q_LORA, MLA_Q_LORA + MLA_KV_LORA], axis=-1)
    q = (rms_norm(c_q, g_qa) @ w_uq).reshape(bsz, n, MLA_HEADS, MLA_NOPE + MLA_ROPE)
    kv = (rms_norm(c_kv, g_kva) @ w_ukv).reshape(bsz, n, MLA_HEADS, MLA_NOPE + MLA_V)
    q_nope, q_pe = jnp.split(q, [MLA_NOPE], axis=-1)
    k_nope, v = jnp.split(kv, [MLA_NOPE], axis=-1)
    k_pe = k_pe[:, :, None, :]
    if rope is not None:
        q_pe = apply_axial_rope(q_pe, rope)
        k_pe = apply_axial_rope(k_pe, rope)
    q = jnp.concatenate([q_nope, q_pe], axis=-1)[:, :, :, None, :]
    k = jnp.concatenate([k_nope, jnp.broadcast_to(k_pe, (bsz, n, MLA_HEADS, MLA_ROPE))], axis=-1)
    return q, k, v


def gqa_project(p, g_qn, g_kn, rope):
    bsz, n = p.shape[:2]
    q, k, v = jnp.split(p, [GQA_HEADS * GQA_HEAD_DIM, (GQA_HEADS + GQA_KV_HEADS) * GQA_HEAD_DIM], axis=-1)
    q = rms_norm(q.reshape(bsz, n, GQA_HEADS, GQA_HEAD_DIM), g_qn)
    k = rms_norm(k.reshape(bsz, n, GQA_KV_HEADS, GQA_HEAD_DIM), g_kn)
    v = v.reshape(bsz, n, GQA_KV_HEADS, GQA_HEAD_DIM)
    if rope is not None:
        q = apply_axial_rope(q, rope)
        k = apply_axial_rope(k, rope)
    q = q.reshape(bsz, n, GQA_KV_HEADS, GQA_HEADS // GQA_KV_HEADS, GQA_HEAD_DIM)
    return q, k, v


def na_project(p):
    bsz, n = p.shape[:2]
    q, k, v = jnp.split(p, 3, axis=-1)
    shape = (bsz, n, NA_HEADS, NA_HEAD_DIM)
    return q.reshape(shape), k.reshape(shape), v.reshape(shape)


def neighbourhood_attention(q, k, v, k_ctx, v_ctx, rpb):
    bsz, n, heads, hd = q.shape
    rows = n // GRID_W
    win_h = min(NA_WIN_H_MAX, rows)
    band = win_h * GRID_W
    qg = jnp.moveaxis(q.reshape(bsz, rows, GRID_W, heads, hd), 1, 0)
    kg = k.reshape(bsz, rows, GRID_W, heads, hd)
    vg = v.reshape(bsz, rows, GRID_W, heads, hd)
    cols = jnp.arange(GRID_W)
    col_start = jnp.clip(cols - NA_WIN_W // 2, 0, GRID_W - NA_WIN_W)
    col_in = (cols[None, :] >= col_start[:, None]) & (cols[None, :] < col_start[:, None] + NA_WIN_W)
    dc_idx = jnp.clip(cols[None, :] - cols[:, None] + NA_WIN_W - 1, 0, 2 * NA_WIN_W - 2)
    rpb_cols = rpb.astype(jnp.float32)[:, :, dc_idx]

    def one_row(args):
        r, q_row = args
        r0 = jnp.clip(r - win_h // 2, 0, rows - win_h)
        k_band = lax.dynamic_slice_in_dim(kg, r0, win_h, axis=1)
        v_band = lax.dynamic_slice_in_dim(vg, r0, win_h, axis=1)
        dr_idx = r0 - r + jnp.arange(win_h) + NA_WIN_H_MAX - 1
        bias = jnp.transpose(jnp.take(rpb_cols, dr_idx, axis=1), (0, 2, 1, 3))
        bias = jnp.where(col_in[None, :, None, :], bias, -jnp.inf)
        s_loc = jnp.einsum('bjhd,buvhd->bhjuv', q_row, k_band).astype(jnp.float32) * NA_SCALE + bias
        s_ctx = jnp.einsum('bjhd,blhd->bhjl', q_row, k_ctx).astype(jnp.float32) * NA_SCALE
        s = jnp.concatenate([s_loc.reshape(bsz, heads, GRID_W, band), s_ctx], axis=-1)
        p = jax.nn.softmax(s, axis=-1).astype(v.dtype)
        p_loc = p[..., :band].reshape(bsz, heads, GRID_W, win_h, GRID_W)
        return (jnp.einsum('bhjuv,buvhd->bjhd', p_loc, v_band)
                + jnp.einsum('bhjl,blhd->bjhd', p[..., band:], v_ctx))

    out = lax.map(one_row, (jnp.arange(rows), qg))
    return jnp.moveaxis(out, 0, 1).reshape(bsz, n, heads * hd)


def short_conv(p, w_conv):
    b_gate, c_gate, u = jnp.split(p, 3, axis=-1)
    n = p.shape[1]
    z = jnp.pad(c_gate * u, ((0, 0), (CONV_WIDTH // 2, CONV_WIDTH // 2), (0, 0)))
    conv = sum(z[:, i:i + n] * w_conv[i] for i in range(CONV_WIDTH))
    return b_gate * conv


def merge_branches(ys, gate_logits, w_branch, w_o):
    y = jnp.stack(ys, axis=-2)
    proj = jnp.einsum('bnkc,kcd->bnkd', y, w_branch)
    gates = jax.nn.sigmoid(gate_logits.astype(jnp.float32)).astype(y.dtype)
    gates = gates.reshape(gates.shape[:-1] + (N_BRANCHES, D_MODEL))
    return jnp.sum(gates * proj, axis=-2) @ w_o


def token_mixers(h_lat, h_ctx, w_in, w_uq, g_qa, w_ukv, g_kva, g_qn, g_kn, rpb, w_conv, w_branch, w_o,
                 rope_mla, rope_gqa, with_ctx_out):
    bsz, n, _ = h_lat.shape
    n_ctx = h_ctx.shape[1]
    a_l, b_l, c_l, d_l, g_l = jnp.split(h_lat @ w_in, IN_CUTS, axis=-1)
    a_c, b_c, c_c, d_c, g_c = jnp.split(h_ctx @ w_in, IN_CUTS, axis=-1)

    qa_l, ka_l, va_l = mla_project(a_l, w_uq, g_qa, w_ukv, g_kva, rope_mla)
    qa_c, ka_c, va_c = mla_project(a_c, w_uq, g_qa, w_ukv, g_kva, None)
    ya_l = blocked_attention(qa_l, jnp.concatenate([ka_c, ka_l], 1), jnp.concatenate([va_c, va_l], 1),
                             MLA_SCALE).reshape(bsz, n, BRANCH_WIDTH)

    qb_l, kb_l, vb_l = gqa_project(b_l, g_qn, g_kn, rope_gqa)
    qb_c, kb_c, vb_c = gqa_project(b_c, g_qn, g_kn, None)
    yb_l = blocked_attention(qb_l, jnp.concatenate([kb_c, kb_l], 1), jnp.concatenate([vb_c, vb_l], 1),
                             GQA_SCALE).reshape(bsz, n, BRANCH_WIDTH)

    qc_l, kc_l, vc_l = na_project(c_l)
    qc_c, kc_c, vc_c = na_project(c_c)
    yc_l = neighbourhood_attention(qc_l, kc_l, vc_l, kc_c, vc_c, rpb)

    yd_l = short_conv(d_l, w_conv)

    out_lat = merge_branches([ya_l, yb_l, yc_l, yd_l], g_l, w_branch, w_o)
    if not with_ctx_out:
        return out_lat, None
    ya_c = attention_core(qa_c, ka_c, va_c, MLA_SCALE).reshape(bsz, n_ctx, BRANCH_WIDTH)
    yb_c = attention_core(qb_c, kb_c, vb_c, GQA_SCALE).reshape(bsz, n_ctx, BRANCH_WIDTH)
    yc_c = attention_core(qc_c[:, :, :, None, :], kc_c, vc_c, NA_SCALE).reshape(bsz, n_ctx, BRANCH_WIDTH)
    yd_c = short_conv(d_c, w_conv)
    out_ctx = merge_branches([ya_c, yb_c, yc_c, yd_c], g_c, w_branch, w_o)
    return out_lat, out_ctx


def hierarchical_moe(x, w_group, b_group, w_router, b_router, w_gate_e, w_up_e, w_down_e):
    n_tok = x.shape[0]
    g_prob = jax.nn.softmax((x @ w_group).astype(jnp.float32), axis=-1)
    g_sel = jnp.argmax(g_prob + b_group.astype(jnp.float32)[None, :], axis=-1)
    p_group = jnp.take_along_axis(g_prob, g_sel[:, None], axis=-1)
    e_logits = (x @ w_router).astype(jnp.float32).reshape(n_tok, N_GROUPS, EXPERTS_PER_GROUP)
    e_logits = jnp.take_along_axis(e_logits, g_sel[:, None, None], axis=1)[:, 0]
    e_prob = jax.nn.softmax(e_logits, axis=-1)
    e_bias = b_router.astype(jnp.float32).reshape(N_GROUPS, EXPERTS_PER_GROUP)[g_sel]
    _, e_local = lax.top_k(e_prob + e_bias, TOP_K)
    e_top = jnp.take_along_axis(e_prob, e_local, axis=-1)
    weights = p_group * e_top / jnp.sum(e_top, axis=-1, keepdims=True)
    expert_id = g_sel[:, None] * EXPERTS_PER_GROUP + e_local

    n_assign = n_tok * TOP_K
    flat_e = expert_id.reshape(-1)
    flat_t = jnp.repeat(jnp.arange(n_tok, dtype=jnp.int32), TOP_K)
    flat_w = weights.reshape(-1).astype(x.dtype)
    order = jnp.argsort(flat_e)
    se = flat_e[order]
    counts = jnp.bincount(flat_e, length=N_EXPERTS)
    padded = (counts + MOE_BLOCK - 1) // MOE_BLOCK * MOE_BLOCK
    pad_end = jnp.cumsum(padded)
    pad_start = pad_end - padded
    start = jnp.cumsum(counts) - counts
    dest = pad_start[se] + jnp.arange(n_assign) - start[se]
    n_blocks = (n_assign + N_EXPERTS * (MOE_BLOCK - 1)) // MOE_BLOCK
    n_rows = n_blocks * MOE_BLOCK
    tok_buf = jnp.full((n_rows,), n_tok, jnp.int32).at[dest].set(flat_t[order])
    w_buf = jnp.zeros((n_rows,), x.dtype).at[dest].set(flat_w[order])
    block_expert = jnp.clip(jnp.searchsorted(pad_end, jnp.arange(n_blocks) * MOE_BLOCK, side='right'),
                            0, N_EXPERTS - 1)
    x_pad = jnp.concatenate([x, jnp.zeros((1, x.shape[1]), x.dtype)], axis=0)
    xs = x_pad[tok_buf].reshape(n_blocks, MOE_BLOCK, x.shape[1])

    def expert_block(args):
        xb, e = args
        hid = jax.nn.silu(xb @ w_gate_e[e]) * (xb @ w_up_e[e])
        return hid @ w_down_e[e]

    ys = lax.map(expert_block, (xs, block_expert)).reshape(n_rows, x.shape[1])
    out = jnp.zeros((n_tok + 1, x.shape[1]), x.dtype).at[tok_buf].add(ys * w_buf[:, None])
    return out[:n_tok]


def setup_inputs(seed: int = 0) -> dict:
    key = jax.random.key(seed)
    ks = jax.random.split(key, 32)

    def nrm(k, shape, scale):
        return jax.random.normal(k, shape, jnp.float32) * scale

    def gain(k, shape):
        return 1.0 + 0.02 * jax.random.normal(k, shape, jnp.float32)

    d = D_MODEL
    return {
        'x': nrm(ks[0], (BATCH, SEQ, d), 1.0),
        'c': nrm(ks[1], (BATCH, d), 1.0),
        'ctx': nrm(ks[2], (BATCH, CTX_LEN, d), 1.0),
        'c_ctx': nrm(ks[3], (d,), 1.0),
        'w_mod': nrm(ks[4], (DEPTH, d, MOD_CHUNKS * d), 0.5 * d ** -0.5),
        'b_mod': nrm(ks[5], (DEPTH, MOD_CHUNKS * d), 0.01),
        'g_mix': gain(ks[6], (DEPTH, d)),
        'g_ffn': gain(ks[7], (DEPTH, d)),
        'w_in': nrm(ks[8], (DEPTH, d, IN_COLS), d ** -0.5),
        'w_uq': nrm(ks[9], (DEPTH, MLA_Q_LORA, MLA_HEADS * (MLA_NOPE + MLA_ROPE)), MLA_Q_LORA ** -0.5),
        'g_qa': gain(ks[10], (DEPTH, MLA_Q_LORA)),
        'w_ukv': nrm(ks[11], (DEPTH, MLA_KV_LORA, MLA_HEADS * (MLA_NOPE + MLA_V)), MLA_KV_LORA ** -0.5),
        'g_kva': gain(ks[12], (DEPTH, MLA_KV_LORA)),
        'g_qn': gain(ks[13], (DEPTH, GQA_HEAD_DIM)),
        'g_kn': gain(ks[14], (DEPTH, GQA_HEAD_DIM)),
        'rpb': nrm(ks[15], (DEPTH, NA_HEADS, 2 * NA_WIN_H_MAX - 1, 2 * NA_WIN_W - 1), 0.1),
        'w_conv': nrm(ks[16], (DEPTH, CONV_WIDTH, CONV_CH), CONV_WIDTH ** -0.5),
        'w_branch': nrm(ks[17], (DEPTH, N_BRANCHES, BRANCH_WIDTH, d), BRANCH_WIDTH ** -0.5),
        'w_o': nrm(ks[18], (DEPTH, d, d), d ** -0.5),
        'w_group': nrm(ks[19], (DEPTH, d, N_GROUPS), d ** -0.5),
        'b_group': nrm(ks[20], (DEPTH, N_GROUPS), 0.01),
        'w_router': nrm(ks[21], (DEPTH, d, N_EXPERTS), d ** -0.5),
        'b_router': nrm(ks[22], (DEPTH, N_EXPERTS), 0.01),
        'w_gate_e': nrm(ks[23], (DEPTH, N_EXPERTS, d, EXPERT_HIDDEN), d ** -0.5),
        'w_up_e': nrm(ks[24], (DEPTH, N_EXPERTS, d, EXPERT_HIDDEN), d ** -0.5),
        'w_down_e': nrm(ks[25], (DEPTH, N_EXPERTS, EXPERT_HIDDEN, d), EXPERT_HIDDEN ** -0.5),
        'g_final': gain(ks[26], (d,)),
    }


def reference(x, c, ctx, c_ctx, w_mod, b_mod, g_mix, g_ffn, w_in, w_uq, g_qa, w_ukv, g_kva, g_qn, g_kn,
              rpb, w_conv, w_branch, w_o, w_group, b_group, w_router, b_router, w_gate_e, w_up_e, w_down_e,
              g_final):
    bsz, seq, _ = x.shape
    rope_mla = axial_rope_tables(seq, MLA_ROPE)
    rope_gqa = axial_rope_tables(seq, GQA_HEAD_DIM)
    s_c = jax.nn.silu(c)
    s_cc = jax.nn.silu(c_ctx)
    x_lat, x_ctx = x, ctx
    n_lat = bsz * seq
    for l in range(DEPTH):
        last = l == DEPTH - 1
        sh1, sc1, ga1, sh2, sc2, ga2 = jnp.split((s_c @ w_mod[l] + b_mod[l])[:, None, :], MOD_CHUNKS, axis=-1)
        csh1, csc1, cga1, csh2, csc2, cga2 = jnp.split((s_cc @ w_mod[l] + b_mod[l])[None, None, :],
                                                      MOD_CHUNKS, axis=-1)
        h_lat = modulate(rms_norm(x_lat, g_mix[l]), sh1, sc1)
        h_ctx = modulate(rms_norm(x_ctx, g_mix[l]), csh1, csc1)
        o_lat, o_ctx = token_mixers(h_lat, h_ctx, w_in[l], w_uq[l], g_qa[l], w_ukv[l], g_kva[l], g_qn[l],
                                    g_kn[l], rpb[l], w_conv[l], w_branch[l], w_o[l], rope_mla, rope_gqa,
                                    not last)
        x_lat = x_lat + ga1 * o_lat
        h2_lat = modulate(rms_norm(x_lat, g_ffn[l]), sh2, sc2)
        moe_w = (w_group[l], b_group[l], w_router[l], b_router[l], w_gate_e[l], w_up_e[l], w_down_e[l])
        if last:
            y = hierarchical_moe(h2_lat.reshape(n_lat, D_MODEL), *moe_w)
            x_lat = x_lat + ga2 * y.reshape(x_lat.shape)
        else:
            x_ctx = x_ctx + cga1 * o_ctx
            h2_ctx = modulate(rms_norm(x_ctx, g_ffn[l]), csh2, csc2)
            tokens = jnp.concatenate([h2_lat.reshape(n_lat, D_MODEL), h2_ctx.reshape(-1, D_MODEL)], axis=0)
            y = hierarchical_moe(tokens, *moe_w)
            x_lat = x_lat + ga2 * y[:n_lat].reshape(x_lat.shape)
            x_ctx = x_ctx + cga2 * y[n_lat:].reshape(x_ctx.shape)
    return rms_norm(x_lat, g_final)
```

```python
import functools

import jax
import jax.numpy as jnp
from jax import lax
from jax.experimental import pallas as pl
from jax.experimental.pallas import tpu as pltpu

F32 = jnp.float32
BF16 = jnp.bfloat16

D_MODEL = 2048
SEQ = 2048
DEPTH = 2
CTX_LEN = 256
GRID_W = 64
ROPE_THETA = 10000.0
NORM_EPS = 1e-6
MOD_CHUNKS = 6

MLA_HEADS = 4
MLA_Q_LORA = 512
MLA_KV_LORA = 512
MLA_NOPE = 128
MLA_ROPE = 64
MLA_V = 128
MLA_SCALE = (MLA_NOPE + MLA_ROPE) ** -0.5
MLA_DK_PAD = 256

GQA_HEADS = 4
GQA_KV_HEADS = 2
GQA_HEAD_DIM = 128
GQA_SCALE = GQA_HEAD_DIM ** -0.5

NA_HEADS = 4
NA_HEAD_DIM = 128
NA_WIN_H = 8
NA_WIN_W = 16
NA_SCALE = NA_HEAD_DIM ** -0.5
NA_Q_ROWS = 8
NA_K_ROWS = 16

CONV_CH = 512
CONV_WIDTH = 3
BRANCH_WIDTH = 512
N_BRANCHES = 4

A_COLS = MLA_Q_LORA + MLA_KV_LORA + MLA_ROPE
B_COLS = (GQA_HEADS + 2 * GQA_KV_HEADS) * GQA_HEAD_DIM
C_COLS = 3 * NA_HEADS * NA_HEAD_DIM
D_COLS = 3 * CONV_CH
G_COLS = N_BRANCHES * D_MODEL

P_CQ = 0
P_CKV = 512
P_B = 1024
P_C = 2048
P_D = 3584
P_G = 5120
P_KPE = 13312
P_COLS = 13440

N_GROUPS = 8
EXPERTS_PER_GROUP = 8
N_EXPERTS = 64
TOP_K = 2
EXPERT_HIDDEN = 512
MOE_BLOCK = 128

LANES = 128
VMEM_LIMIT = 56 * 1024 * 1024


def _params(*sem):
    return pltpu.CompilerParams(dimension_semantics=sem, vmem_limit_bytes=VMEM_LIMIT)


def _pick(n, cands):
    for c in cands:
        if n % c == 0:
            return c
    raise ValueError(f"no tile for {n}")


def _rms(x, g):
    return x * lax.rsqrt(jnp.mean(x * x, axis=-1, keepdims=True) + NORM_EPS) * g


def _mod_row(i, tm, bsz):
    return jnp.minimum(i * tm // SEQ, bsz)


def _mod_kernel(c_ref, w_ref, b_ref, o_ref):
    c = c_ref[...]
    s = c * jax.nn.sigmoid(c)
    o_ref[0] = jnp.dot(s, w_ref[0], precision=lax.Precision.HIGHEST,
                       preferred_element_type=F32) + b_ref[0]


def _modulation(c_all, w_mod, b_mod):
    rows = c_all.shape[0]
    n = MOD_CHUNKS * D_MODEL
    tn = 1024
    return pl.pallas_call(
        _mod_kernel,
        grid=(DEPTH, n // tn),
        in_specs=[
            pl.BlockSpec((rows, D_MODEL), lambda l, j: (0, 0)),
            pl.BlockSpec((1, D_MODEL, tn), lambda l, j: (l, 0, j)),
            pl.BlockSpec((1, 1, tn), lambda l, j: (l, 0, j)),
        ],
        out_specs=pl.BlockSpec((1, rows, tn), lambda l, j: (l, 0, j)),
        out_shape=jax.ShapeDtypeStruct((DEPTH, rows, n), F32),
        compiler_params=_params("arbitrary", "arbitrary"),
        name="modulation",
    )(c_all, w_mod, b_mod.reshape(DEPTH, 1, n))


def _hmod_kernel(x_ref, g_ref, sh_ref, sc_ref, o_ref, *, tm, bsz):
    r = _mod_row(pl.program_id(0), tm, bsz)
    y = _rms(x_ref[...], g_ref[...])
    o_ref[...] = (y * (1.0 + sc_ref[pl.ds(r, 1), :]) + sh_ref[pl.ds(r, 1), :]).astype(o_ref.dtype)


def _hmod(x_all, gain, mod, bsz):
    n = x_all.shape[0]
    tm = _pick(n, (512, 256))
    rows = mod.shape[0]
    return pl.pallas_call(
        functools.partial(_hmod_kernel, tm=tm, bsz=bsz),
        grid=(n // tm,),
        in_specs=[
            pl.BlockSpec((tm, D_MODEL), lambda i: (i, 0)),
            pl.BlockSpec((1, D_MODEL), lambda i: (0, 0)),
            pl.BlockSpec((rows, D_MODEL), lambda i: (0, 0)),
            pl.BlockSpec((rows, D_MODEL), lambda i: (0, 1)),
        ],
        out_specs=pl.BlockSpec((tm, D_MODEL), lambda i: (i, 0)),
        out_shape=jax.ShapeDtypeStruct((n, D_MODEL), BF16),
        compiler_params=_params("arbitrary"),
        name="hmod",
    )(x_all, gain.reshape(1, D_MODEL), mod, mod)


def _mm_kernel(a_ref, w_ref, o_ref):
    o_ref[...] = jnp.dot(a_ref[...], w_ref[...], preferred_element_type=F32).astype(o_ref.dtype)


def _matmul(a, w, tn, out_dtype, name):
    m, k = a.shape
    n = w.shape[1]
    tm = _pick(m, (1024, 512, 256))
    return pl.pallas_call(
        _mm_kernel,
        grid=(n // tn, m // tm),
        in_specs=[
            pl.BlockSpec((tm, k), lambda j, i: (i, 0)),
            pl.BlockSpec((k, tn), lambda j, i: (0, j)),
        ],
        out_specs=pl.BlockSpec((tm, tn), lambda j, i: (i, j)),
        out_shape=jax.ShapeDtypeStruct((m, n), out_dtype),
        compiler_params=_params("arbitrary", "arbitrary"),
        name=name,
    )(a, w)


def _mm_res_kernel(a_ref, w_ref, x_ref, ga_ref, o_ref, *, tm, bsz):
    r = _mod_row(pl.program_id(1), tm, bsz)
    acc = jnp.dot(a_ref[...], w_ref[...], preferred_element_type=F32)
    o_ref[...] = x_ref[...] + ga_ref[pl.ds(r, 1), :] * acc


def _out_proj_residual(mixed, w_o, x_all, mod, bsz):
    m = mixed.shape[0]
    tm = _pick(m, (1024, 512, 256))
    tn = 512
    rows = mod.shape[0]
    ga_blk = 2 * D_MODEL // tn
    return pl.pallas_call(
        functools.partial(_mm_res_kernel, tm=tm, bsz=bsz),
        grid=(D_MODEL // tn, m // tm),
        in_specs=[
            pl.BlockSpec((tm, D_MODEL), lambda j, i: (i, 0)),
            pl.BlockSpec((D_MODEL, tn), lambda j, i: (0, j)),
            pl.BlockSpec((tm, tn), lambda j, i: (i, j)),
            pl.BlockSpec((rows, tn), lambda j, i: (0, ga_blk + j)),
        ],
        out_specs=pl.BlockSpec((tm, tn), lambda j, i: (i, j)),
        out_shape=jax.ShapeDtypeStruct((m, D_MODEL), F32),
        compiler_params=_params("arbitrary", "arbitrary"),
        name="out_proj_residual",
    )(mixed, w_o, x_all, mod)


def _prep_kernel(cq_ref, ckv_ref, kpe_ref, bq_ref, bk_ref,
                 cosa_ref, sina_ref, csa_ref, cosb_ref, sinb_ref,
                 wuq_ref, wukv_ref, gqa_ref, gkva_ref, gqn_ref, gkn_ref,
                 qa_ref, ka_ref, va_ref, qb_ref, kb_ref):
    tm = cq_ref.shape[0]
    lane = lax.broadcasted_iota(jnp.int32, (tm, LANES), 1)

    nq = _rms(cq_ref[...].astype(F32), gqa_ref[...]).astype(BF16)
    qf = jnp.dot(nq, wuq_ref[...], preferred_element_type=F32)
    cosa = cosa_ref[...]
    sina = sina_ref[...]
    for h in range(MLA_HEADS):
        base = h * 3 * LANES
        nope = qf[:, base:base + LANES]
        roped = qf[:, base + LANES:base + 2 * LANES] * cosa + qf[:, base + 2 * LANES:base + 3 * LANES] * sina
        qa_ref[:, h * MLA_DK_PAD:h * MLA_DK_PAD + LANES] = (nope * MLA_SCALE).astype(BF16)
        qa_ref[:, h * MLA_DK_PAD + LANES:(h + 1) * MLA_DK_PAD] = (roped * MLA_SCALE).astype(BF16)

    nkv = _rms(ckv_ref[...].astype(F32), gkva_ref[...]).astype(BF16)
    kvf = jnp.dot(nkv, wukv_ref[...], preferred_element_type=F32)
    t = kpe_ref[...].astype(F32) * csa_ref[...]
    kpe = jnp.where(lane < MLA_ROPE, t + pltpu.roll(t, MLA_ROPE, axis=1), 0.0).astype(BF16)
    for h in range(MLA_HEADS):
        ka_ref[:, h * MLA_DK_PAD:h * MLA_DK_PAD + LANES] = kvf[:, h * LANES:(h + 1) * LANES].astype(BF16)
        ka_ref[:, h * MLA_DK_PAD + LANES:(h + 1) * MLA_DK_PAD] = kpe
    va_ref[...] = kvf[:, MLA_HEADS * MLA_NOPE:].astype(BF16)

    cosb = cosb_ref[...]
    sinb = sinb_ref[...]
    first = (lane % (GQA_HEAD_DIM // 2)) < (GQA_HEAD_DIM // 4)

    def rope_b(x):
        rot = jnp.where(first, pltpu.roll(x, LANES - GQA_HEAD_DIM // 4, axis=1),
                        pltpu.roll(x, GQA_HEAD_DIM // 4, axis=1))
        return x * cosb + rot * sinb

    gqn = gqn_ref[...]
    gkn = gkn_ref[...]
    for h in range(GQA_HEADS):
        x = bq_ref[:, h * LANES:(h + 1) * LANES].astype(F32)
        qb_ref[:, h * LANES:(h + 1) * LANES] = (rope_b(_rms(x, gqn)) * GQA_SCALE).astype(BF16)
    for h in range(GQA_KV_HEADS):
        x = bk_ref[:, h * LANES:(h + 1) * LANES].astype(F32)
        kb_ref[:, h * LANES:(h + 1) * LANES] = rope_b(_rms(x, gkn)).astype(BF16)


def _prep(p, tables, w_uq_x, w_ukv_r, g_qa, g_kva, g_qn, g_kn):
    n = p.shape[0]
    tm = _pick(n, (512, 256))
    n_pos = SEQ // tm
    n_lat_tiles = (n // (SEQ + CTX_LEN)) * n_pos

    def tab(i):
        return (jnp.where(i < n_lat_tiles, i % n_pos, n_pos), 0)

    def col(width, off):
        return pl.BlockSpec((tm, width), lambda i: (i, off // width))

    def whole(a):
        return pl.BlockSpec(a.shape, lambda i: (0,) * a.ndim)

    tab_spec = pl.BlockSpec((tm, LANES), tab)
    cosa, sina, csa, cosb, sinb = tables
    outs = [(MLA_HEADS * MLA_DK_PAD, "qa"), (MLA_HEADS * MLA_DK_PAD, "ka"), (MLA_HEADS * MLA_V, "va"),
            (GQA_HEADS * GQA_HEAD_DIM, "qb"), (GQA_KV_HEADS * GQA_HEAD_DIM, "kb")]
    return pl.pallas_call(
        _prep_kernel,
        grid=(n // tm,),
        in_specs=[col(512, P_CQ), col(512, P_CKV), col(LANES, P_KPE), col(512, P_B), col(256, P_B + 512),
                  tab_spec, tab_spec, tab_spec, tab_spec, tab_spec,
                  whole(w_uq_x), whole(w_ukv_r), whole(g_qa), whole(g_kva), whole(g_qn), whole(g_kn)],
        out_specs=[pl.BlockSpec((tm, w), lambda i: (i, 0)) for w, _ in outs],
        out_shape=[jax.ShapeDtypeStruct((n, w), BF16) for w, _ in outs],
        compiler_params=_params("arbitrary"),
        name="attn_prep",
    )(p, p, p, p, p, cosa, sina, csa, cosb, sinb, w_uq_x, w_ukv_r, g_qa, g_kva, g_qn, g_kn)


def _attn_kernel(*refs, n_seg, n_groups, rep, dk, dv, scale):
    q_ref = refs[0]
    segs = [(refs[1 + 2 * s], refs[2 + 2 * s]) for s in range(n_seg)]
    o_ref = refs[1 + 2 * n_seg]
    nt = (((1,), (1,)), ((), ()))
    for g in range(n_groups):
        ks = [k_ref[:, g * dk:(g + 1) * dk] for k_ref, _ in segs]
        vs = [v_ref[:, g * dv:(g + 1) * dv] for _, v_ref in segs]
        for r in range(rep):
            hq = g * rep + r
            q = q_ref[:, hq * dk:(hq + 1) * dk]
            ss = [lax.dot_general(q, k, nt, preferred_element_type=F32) for k in ks]
            if scale != 1.0:
                ss = [s * scale for s in ss]
            m = functools.reduce(jnp.maximum, [jnp.max(s, axis=-1, keepdims=True) for s in ss])
            acc = None
            den = None
            for s, v in zip(ss, vs):
                e = jnp.exp(s - m)
                d = jnp.sum(e, axis=-1, keepdims=True)
                a = jnp.dot(e.astype(BF16), v, preferred_element_type=F32)
                acc = a if acc is None else acc + a
                den = d if den is None else den + d
            o_ref[:, hq * dv:(hq + 1) * dv] = (acc / den).astype(o_ref.dtype)


def _attention(q, q_off, segs, bsz, n_q_per_batch, q_row0, *, n_groups, rep, dk, dv, scale, name):
    tq = min(256, n_q_per_batch)
    nq = n_q_per_batch // tq
    qw = n_groups * rep * dk
    ow = n_groups * rep * dv

    def q_map(b, i):
        return ((q_row0 + b * n_q_per_batch) // tq + i, q_off // qw)

    in_specs = [pl.BlockSpec((tq, qw), q_map)]
    args = [q]
    for k_arr, k_off, v_arr, v_off, length, row0 in segs:
        kw, vw = n_groups * dk, n_groups * dv
        in_specs.append(pl.BlockSpec((length, kw), functools.partial(
            lambda b, i, length, row0, blk: (row0 // length + b, blk), length=length, row0=row0, blk=k_off // kw)))
        in_specs.append(pl.BlockSpec((length, vw), functools.partial(
            lambda b, i, length, row0, blk: (row0 // length + b, blk), length=length, row0=row0, blk=v_off // vw)))
        args += [k_arr, v_arr]
    return pl.pallas_call(
        functools.partial(_attn_kernel, n_seg=len(segs), n_groups=n_groups, rep=rep, dk=dk, dv=dv, scale=scale),
        grid=(bsz, nq),
        in_specs=in_specs,
        out_specs=pl.BlockSpec((tq, ow), lambda b, i: (b * nq + i, 0)),
        out_shape=jax.ShapeDtypeStruct((bsz * n_q_per_batch, ow), BF16),
        compiler_params=_params("arbitrary", "arbitrary"),
        name=name,
    )(*args)


def _na_kernel(q_ref, k_ref, v_ref, kc_ref, vc_ref, bias_ref, o_ref):
    j = pl.program_id(0)
    k_rows = NA_K_ROWS * GRID_W
    start = jnp.clip(NA_Q_ROWS * j - NA_WIN_H // 2, 0, SEQ // GRID_W - NA_K_ROWS) * GRID_W
    start = pl.multiple_of(start, 256)
    nt = (((1,), (1,)), ((), ()))
    q = q_ref[...]
    s1 = lax.dot_general(q, k_ref[pl.ds(start, k_rows), :], nt, preferred_element_type=F32) * NA_SCALE
    s1 = s1 + bias_ref[0, 0]
    s2 = lax.dot_general(q, kc_ref[...], nt, preferred_element_type=F32) * NA_SCALE
    m = jnp.maximum(jnp.max(s1, axis=-1, keepdims=True), jnp.max(s2, axis=-1, keepdims=True))
    e1 = jnp.exp(s1 - m)
    e2 = jnp.exp(s2 - m)
    den = jnp.sum(e1, axis=-1, keepdims=True) + jnp.sum(e2, axis=-1, keepdims=True)
    acc = jnp.dot(e1.astype(BF16), v_ref[pl.ds(start, k_rows), :], preferred_element_type=F32)
    acc = acc + jnp.dot(e2.astype(BF16), vc_ref[...], preferred_element_type=F32)
    o_ref[...] = (acc / den).astype(o_ref.dtype)


def _na_bias(rpb):
    rows = SEQ // GRID_W
    n_tiles = rows // NA_Q_ROWS
    j = jnp.arange(n_tiles)
    k0 = jnp.clip(NA_Q_ROWS * j - NA_WIN_H // 2, 0, rows - NA_K_ROWS)
    rq = NA_Q_ROWS * j[:, None] + jnp.arange(NA_Q_ROWS)[None, :]
    rk = k0[:, None] + jnp.arange(NA_K_ROWS)[None, :]
    r0 = jnp.clip(rq - NA_WIN_H // 2, 0, rows - NA_WIN_H)
    row_ok = (rk[:, None, :] >= r0[:, :, None]) & (rk[:, None, :] < r0[:, :, None] + NA_WIN_H)
    dr = jnp.clip(rk[:, None, :] - rq[:, :, None] + NA_WIN_H - 1, 0, 2 * NA_WIN_H - 2)
    cols = jnp.arange(GRID_W)
    c0 = jnp.clip(cols - NA_WIN_W // 2, 0, GRID_W - NA_WIN_W)
    col_ok = (cols[None, :] >= c0[:, None]) & (cols[None, :] < c0[:, None] + NA_WIN_W)
    dc = jnp.clip(cols[None, :] - cols[:, None] + NA_WIN_W - 1, 0, 2 * NA_WIN_W - 2)
    b = rpb.astype(F32)[:, dr[:, :, None, :, None], dc[None, None, :, None, :]]
    ok = row_ok[:, :, None, :, None] & col_ok[None, None, :, None, :]
    b = jnp.where(ok[None], b, -jnp.inf)
    b = jnp.transpose(b, (1, 0, 2, 3, 4, 5))
    return b.reshape(n_tiles, NA_HEADS, NA_Q_ROWS * GRID_W, NA_K_ROWS * GRID_W)


def _na_attention(p, bias, bsz, n_lat):
    tq = NA_Q_ROWS * GRID_W
    n_tiles = SEQ // tq
    hd = NA_HEAD_DIM
    qb, kb, vb = P_C // hd, (P_C + 512) // hd, (P_C + 1024) // hd
    return pl.pallas_call(
        _na_kernel,
        grid=(n_tiles, NA_HEADS, bsz),
        in_specs=[
            pl.BlockSpec((tq, hd), lambda j, h, b: (b * n_tiles + j, qb + h)),
            pl.BlockSpec((SEQ, hd), lambda j, h, b: (b, kb + h)),
            pl.BlockSpec((SEQ, hd), lambda j, h, b: (b, vb + h)),
            pl.BlockSpec((CTX_LEN, hd), lambda j, h, b: (n_lat // CTX_LEN + b, kb + h)),
            pl.BlockSpec((CTX_LEN, hd), lambda j, h, b: (n_lat // CTX_LEN + b, vb + h)),
            pl.BlockSpec((1, 1, tq, NA_K_ROWS * GRID_W), lambda j, h, b: (j, h, 0, 0)),
        ],
        out_specs=pl.BlockSpec((tq, hd), lambda j, h, b: (b * n_tiles + j, h)),
        out_shape=jax.ShapeDtypeStruct((n_lat, NA_HEADS * hd), BF16),
        compiler_params=_params("arbitrary", "arbitrary", "arbitrary"),
        name="na_attention",
    )(p, p, p, p, p, bias)


def _conv_kernel(b_ref, c_ref, u_ref, w_ref, o_ref):
    length = c_ref.shape[0]
    z = c_ref[...].astype(F32) * u_ref[...].astype(F32)
    row = lax.broadcasted_iota(jnp.int32, z.shape, 0)
    z_prev = jnp.where(row == 0, 0.0, pltpu.roll(z, 1, axis=0))
    z_next = jnp.where(row == length - 1, 0.0, pltpu.roll(z, length - 1, axis=0))
    conv = z_prev * w_ref[0:1, :] + z * w_ref[1:2, :] + z_next * w_ref[2:3, :]
    o_ref[...] = (b_ref[...].astype(F32) * conv).astype(o_ref.dtype)


def _short_conv(p, w_conv8, bsz, length, row0):
    blk = P_D // CONV_CH

    def rows(b):
        return row0 // length + b

    return pl.pallas_call(
        _conv_kernel,
        grid=(bsz,),
        in_specs=[
            pl.BlockSpec((length, CONV_CH), lambda b: (rows(b), blk)),
            pl.BlockSpec((length, CONV_CH), lambda b: (rows(b), blk + 1)),
            pl.BlockSpec((length, CONV_CH), lambda b: (rows(b), blk + 2)),
            pl.BlockSpec((8, CONV_CH), lambda b: (0, 0)),
        ],
        out_specs=pl.BlockSpec((length, CONV_CH), lambda b: (b, 0)),
        out_shape=jax.ShapeDtypeStruct((bsz * length, CONV_CH), BF16),
        compiler_params=_params("arbitrary"),
        name="short_conv",
    )(p, p, p, w_conv8)


def _merge_kernel(ya_ref, yb_ref, yc_ref, yd_ref, g0_ref, g1_ref, g2_ref, g3_ref, wb_ref, o_ref):
    acc = None
    for k, (y_ref, g_ref) in enumerate(((ya_ref, g0_ref), (yb_ref, g1_ref), (yc_ref, g2_ref), (yd_ref, g3_ref))):
        proj = jnp.dot(y_ref[...], wb_ref[k], preferred_element_type=F32)
        term = jax.nn.sigmoid(g_ref[...].astype(F32)) * proj
        acc = term if acc is None else acc + term
    o_ref[...] = acc.astype(o_ref.dtype)


def _merge(ys, p, w_branch, n_act):
    tm = _pick(n_act, (1024, 512, 256))
    tn = 512
    g_specs = [pl.BlockSpec((tm, tn), functools.partial(
        lambda j, i, k: (i, (P_G + k * D_MODEL) // tn + j), k=k)) for k in range(N_BRANCHES)]
    return pl.pallas_call(
        _merge_kernel,
        grid=(D_MODEL // tn, n_act // tm),
        in_specs=[pl.BlockSpec((tm, BRANCH_WIDTH), lambda j, i: (i, 0))] * N_BRANCHES + g_specs + [
            pl.BlockSpec((N_BRANCHES, BRANCH_WIDTH, tn), lambda j, i: (0, 0, j))],
        out_specs=pl.BlockSpec((tm, tn), lambda j, i: (i, j)),
        out_shape=jax.ShapeDtypeStruct((n_act, D_MODEL), BF16),
        compiler_params=_params("arbitrary", "arbitrary"),
        name="branch_merge",
    )(*ys, p, p, p, p, w_branch)


def _router_kernel(x_ref, g_ref, sh_ref, sc_ref, wr_ref, br_ref, h_ref, route_ref, *, tm, bsz):
    r = _mod_row(pl.program_id(0), tm, bsz)
    h = _rms(x_ref[...], g_ref[...]) * (1.0 + sc_ref[pl.ds(r, 1), :]) + sh_ref[pl.ds(r, 1), :]
    h_ref[...] = h
    logits = jnp.dot(h, wr_ref[...], precision=lax.Precision.HIGHEST, preferred_element_type=F32)
    bias = br_ref[...]
    lane = lax.broadcasted_iota(jnp.int32, logits.shape, 1)
    lane_f = lane.astype(F32)
    neg = -jnp.inf

    def rmax(x, mask):
        return jnp.max(jnp.where(mask, x, neg), axis=-1, keepdims=True)

    def first_at(x, val, mask):
        return jnp.min(jnp.where(mask & (x == val), lane_f, 2.0 * LANES), axis=-1, keepdims=True)

    def pick(x, idx):
        return jnp.sum(jnp.where(lane_f == idx, x, 0.0), axis=-1, keepdims=True)

    gmask = lane < N_GROUPS
    ge = jnp.where(gmask, jnp.exp(logits - rmax(logits, gmask)), 0.0)
    g_prob = ge / jnp.sum(ge, axis=-1, keepdims=True)
    g_score = g_prob + bias
    g_sel = first_at(g_score, rmax(g_score, gmask), gmask)
    p_group = pick(g_prob, g_sel)

    group_of_lane = lax.shift_right_arithmetic(lane - N_EXPERTS, 3).astype(F32)
    emask = (lane >= N_EXPERTS) & (group_of_lane == g_sel)
    ee = jnp.where(emask, jnp.exp(logits - rmax(logits, emask)), 0.0)
    e_prob = ee / jnp.sum(ee, axis=-1, keepdims=True)
    e_score = e_prob + bias
    i1 = first_at(e_score, rmax(e_score, emask), emask)
    mask2 = emask & (lane_f != i1)
    i2 = first_at(e_score, rmax(e_score, mask2), mask2)
    p1 = pick(e_prob, i1)
    p2 = pick(e_prob, i2)
    w1 = p_group * p1 / (p1 + p2)
    w2 = p_group * p2 / (p1 + p2)
    e1 = i1 - N_EXPERTS
    e2 = i2 - N_EXPERTS
    route_ref[...] = jnp.where(lane == 0, e1, jnp.where(lane == 1, e2, jnp.where(
        lane == 2, w1, jnp.where(lane == 3, w2, 0.0))))


def _ffn_input_and_route(x1, gain, mod, w_route, b_route, bsz):
    n = x1.shape[0]
    tm = _pick(n, (512, 256))
    rows = mod.shape[0]
    return pl.pallas_call(
        functools.partial(_router_kernel, tm=tm, bsz=bsz),
        grid=(n // tm,),
        in_specs=[
            pl.BlockSpec((tm, D_MODEL), lambda i: (i, 0)),
            pl.BlockSpec((1, D_MODEL), lambda i: (0, 0)),
            pl.BlockSpec((rows, D_MODEL), lambda i: (0, 3)),
            pl.BlockSpec((rows, D_MODEL), lambda i: (0, 4)),
            pl.BlockSpec((D_MODEL, LANES), lambda i: (0, 0)),
            pl.BlockSpec((1, LANES), lambda i: (0, 0)),
        ],
        out_specs=[pl.BlockSpec((tm, D_MODEL), lambda i: (i, 0)),
                   pl.BlockSpec((tm, LANES), lambda i: (i, 0))],
        out_shape=[jax.ShapeDtypeStruct((n, D_MODEL), F32), jax.ShapeDtypeStruct((n, LANES), F32)],
        compiler_params=_params("arbitrary"),
        name="ffn_input_route",
    )(x1, gain.reshape(1, D_MODEL), mod, mod, w_route, b_route)


def _row_copy(src_hbm, row, dst, slot, dst_row, sem):
    return pltpu.make_async_copy(src_hbm.at[pl.ds(row, 1), :], dst.at[slot, pl.ds(dst_row, 1), :], sem.at[slot])


def _moe_kernel(be_ref, tok_ref, nused_ref, h_hbm, wg_ref, wu_ref, wd_ref, o_ref,
                xbuf, wgb, wub, wdb, sem):
    i = pl.program_id(0)
    slot = i % 2
    n_used = nused_ref[0]

    def gather(blk, s):
        def body(r, carry):
            _row_copy(h_hbm, tok_ref[blk * MOE_BLOCK + r], xbuf, s, r, sem).start()
            return carry
        lax.fori_loop(0, MOE_BLOCK, body, 0)

    @pl.when((i == 0) & (n_used > 0))
    def _():
        gather(0, 0)

    @pl.when(i + 1 < n_used)
    def _():
        gather(i + 1, 1 - slot)

    @pl.when((i == 0) | (be_ref[i] != be_ref[jnp.maximum(i - 1, 0)]))
    def _():
        wgb[...] = wg_ref[0].astype(BF16)
        wub[...] = wu_ref[0].astype(BF16)
        wdb[...] = wd_ref[0].astype(BF16)

    @pl.when(i < n_used)
    def _():
        pltpu.make_async_copy(h_hbm.at[pl.ds(0, MOE_BLOCK), :], xbuf.at[slot], sem.at[slot]).wait()
        x = xbuf[slot].astype(BF16)
        gate = jnp.dot(x, wgb[...], preferred_element_type=F32)
        up = jnp.dot(x, wub[...], preferred_element_type=F32)
        hid = (gate * jax.nn.sigmoid(gate) * up).astype(BF16)
        o_ref[...] = jnp.dot(hid, wdb[...], preferred_element_type=F32)

    @pl.when(i >= n_used)
    def _():
        o_ref[...] = jnp.zeros_like(o_ref)


def _moe(h2, blk_e, tok_buf, n_used, w_gate, w_up, w_down):
    n_blocks = blk_e.shape[0]
    grid_spec = pltpu.PrefetchScalarGridSpec(
        num_scalar_prefetch=3,
        grid=(n_blocks,),
        in_specs=[
            pl.BlockSpec(memory_space=pl.ANY),
            pl.BlockSpec((1, D_MODEL, EXPERT_HIDDEN), lambda i, be, tok, nu: (be[i], 0, 0)),
            pl.BlockSpec((1, D_MODEL, EXPERT_HIDDEN), lambda i, be, tok, nu: (be[i], 0, 0)),
            pl.BlockSpec((1, EXPERT_HIDDEN, D_MODEL), lambda i, be, tok, nu: (be[i], 0, 0)),
        ],
        out_specs=pl.BlockSpec((MOE_BLOCK, D_MODEL), lambda i, be, tok, nu: (i, 0)),
        scratch_shapes=[
            pltpu.VMEM((2, MOE_BLOCK, D_MODEL), F32),
            pltpu.VMEM((D_MODEL, EXPERT_HIDDEN), BF16),
            pltpu.VMEM((D_MODEL, EXPERT_HIDDEN), BF16),
            pltpu.VMEM((EXPERT_HIDDEN, D_MODEL), BF16),
            pltpu.SemaphoreType.DMA((2,)),
        ],
    )
    return pl.pallas_call(
        _moe_kernel,
        grid_spec=grid_spec,
        out_shape=jax.ShapeDtypeStruct((n_blocks * MOE_BLOCK, D_MODEL), F32),
        compiler_params=_params("arbitrary"),
        name="moe_experts",
    )(blk_e, tok_buf, n_used, h2, w_gate, w_up, w_down)


def _combine_kernel(pos_ref, ys_hbm, x_ref, route_ref, ga_ref, gf_ref, o_ref, rbuf, sem, *, tm, bsz, final):
    i = pl.program_id(0)
    n = pl.num_programs(0)
    slot = i % 2

    def gather(blk, s):
        def body(r, carry):
            for k in range(TOP_K):
                _row_copy(ys_hbm, pos_ref[(blk * tm + r) * TOP_K + k], rbuf, s, k * tm + r, sem).start()
            return carry
        lax.fori_loop(0, tm, body, 0)

    @pl.when(i == 0)
    def _():
        gather(0, 0)

    @pl.when(i + 1 < n)
    def _():
        gather(i + 1, 1 - slot)

    pltpu.make_async_copy(ys_hbm.at[pl.ds(0, TOP_K * tm), :], rbuf.at[slot], sem.at[slot]).wait()
    w0 = route_ref[:, 2:3]
    w1 = route_ref[:, 3:4]
    y = rbuf[slot, 0:tm, :] * w0 + rbuf[slot, tm:2 * tm, :] * w1
    r = _mod_row(i, tm, bsz)
    x = x_ref[...] + ga_ref[pl.ds(r, 1), :] * y
    if final:
        x = _rms(x, gf_ref[...])
    o_ref[...] = x


def _combine(ys, pos, x1, route, mod, g_final, bsz, final):
    n = x1.shape[0]
    tm = 128
    rows = mod.shape[0]
    grid_spec = pltpu.PrefetchScalarGridSpec(
        num_scalar_prefetch=1,
        grid=(n // tm,),
        in_specs=[
            pl.BlockSpec(memory_space=pl.ANY),
            pl.BlockSpec((tm, D_MODEL), lambda i, pos: (i, 0)),
            pl.BlockSpec((tm, LANES), lambda i, pos: (i, 0)),
            pl.BlockSpec((rows, D_MODEL), lambda i, pos: (0, 5)),
            pl.BlockSpec((1, D_MODEL), lambda i, pos: (0, 0)),
        ],
        out_specs=pl.BlockSpec((tm, D_MODEL), lambda i, pos: (i, 0)),
        scratch_shapes=[pltpu.VMEM((2, TOP_K * tm, D_MODEL), F32), pltpu.SemaphoreType.DMA((2,))],
    )
    return pl.pallas_call(
        functools.partial(_combine_kernel, tm=tm, bsz=bsz, final=final),
        grid_spec=grid_spec,
        out_shape=jax.ShapeDtypeStruct((n, D_MODEL), F32),
        compiler_params=_params("arbitrary"),
        name="moe_combine",
    )(pos, ys, x1, route, mod, g_final.reshape(1, D_MODEL))


def _dispatch_indices(route, n_tok):
    flat_e = route[:, :TOP_K].astype(jnp.int32).reshape(-1)
    n_assign = n_tok * TOP_K
    order = jnp.argsort(flat_e).astype(jnp.int32)
    se = flat_e[order]
    counts = jnp.bincount(flat_e, length=N_EXPERTS).astype(jnp.int32)
    padded = (counts + MOE_BLOCK - 1) // MOE_BLOCK * MOE_BLOCK
    pad_end = jnp.cumsum(padded)
    pad_start = pad_end - padded
    start = jnp.cumsum(counts) - counts
    dest = (pad_start[se] + jnp.arange(n_assign, dtype=jnp.int32) - start[se]).astype(jnp.int32)
    n_blocks = (n_assign + N_EXPERTS * (MOE_BLOCK - 1)) // MOE_BLOCK
    tok_buf = jnp.zeros((n_blocks * MOE_BLOCK,), jnp.int32).at[dest].set(order // TOP_K)
    pos = jnp.zeros((n_assign,), jnp.int32).at[order].set(dest)
    blk_e = jnp.clip(jnp.searchsorted(pad_end, jnp.arange(n_blocks, dtype=jnp.int32) * MOE_BLOCK, side='right'),
                     0, N_EXPERTS - 1).astype(jnp.int32)
    n_used = (pad_end[-1:] // MOE_BLOCK).astype(jnp.int32)
    return blk_e, tok_buf, n_used, pos


def _rotate_half_cols(w, quarter):
    r1, r2, c1, c2 = (w[..., k * quarter:(k + 1) * quarter] for k in range(4))
    return jnp.concatenate([-r2, r1, -c2, c1], axis=-1)


def _relayout_w_in(w_in):
    kpe = w_in[:, MLA_Q_LORA + MLA_KV_LORA:A_COLS]
    return jnp.concatenate([w_in[:, :MLA_Q_LORA + MLA_KV_LORA], w_in[:, A_COLS:], kpe,
                            _rotate_half_cols(kpe, MLA_ROPE // 4)], axis=1).astype(BF16)


def _relayout_w_uq(w_uq):
    w = w_uq.reshape(MLA_Q_LORA, MLA_HEADS, MLA_NOPE + MLA_ROPE)
    nope, pe = w[..., :MLA_NOPE], w[..., MLA_NOPE:]
    z = jnp.zeros_like(pe)
    out = jnp.concatenate([nope, pe, z, _rotate_half_cols(pe, MLA_ROPE // 4), z], axis=-1)
    return out.reshape(MLA_Q_LORA, MLA_HEADS * 3 * LANES).astype(BF16)


def _relayout_w_ukv(w_ukv):
    w = w_ukv.reshape(MLA_KV_LORA, MLA_HEADS, MLA_NOPE + MLA_V)
    return jnp.concatenate([w[..., :MLA_NOPE].reshape(MLA_KV_LORA, -1),
                            w[..., MLA_NOPE:].reshape(MLA_KV_LORA, -1)], axis=-1).astype(BF16)


def _rope_tables(tm):
    t = jnp.arange(SEQ)
    rows = (t // GRID_W).astype(F32)
    cols = (t % GRID_W).astype(F32)

    def cs(rot_dim):
        quarter = rot_dim // 4
        inv_freq = ROPE_THETA ** (-jnp.arange(quarter, dtype=F32) / quarter)
        ang_r = rows[:, None] * inv_freq[None, :]
        ang_c = cols[:, None] * inv_freq[None, :]
        cos = jnp.concatenate([jnp.cos(ang_r)] * 2 + [jnp.cos(ang_c)] * 2, axis=-1)
        sin = jnp.concatenate([jnp.sin(ang_r)] * 2 + [jnp.sin(ang_c)] * 2, axis=-1)
        return cos, sin

    def with_identity(tab, ident):
        return jnp.concatenate([tab, jnp.broadcast_to(ident, (tm, LANES))], axis=0)

    cos_a, sin_a = cs(MLA_ROPE)
    zeros = jnp.zeros((SEQ, LANES - MLA_ROPE), F32)
    lane = jnp.arange(LANES)
    one_lo = (lane < MLA_ROPE).astype(F32)[None, :]
    cosa = with_identity(jnp.concatenate([cos_a, zeros], axis=1), one_lo)
    sina = with_identity(jnp.concatenate([sin_a, zeros], axis=1), jnp.zeros((1, LANES), F32))
    csa = with_identity(jnp.concatenate([cos_a, sin_a], axis=1), one_lo)
    cos_b, sin_b = cs(GQA_HEAD_DIM)
    sign = jnp.where((lane % (GQA_HEAD_DIM // 2)) < GQA_HEAD_DIM // 4, -1.0, 1.0).astype(F32)[None, :]
    cosb = with_identity(cos_b, jnp.ones((1, LANES), F32))
    sinb = with_identity(sin_b * sign, jnp.zeros((1, LANES), F32))
    return cosa, sina, csa, cosb, sinb


def kernel(x, c, ctx, c_ctx, w_mod, b_mod, g_mix, g_ffn, w_in, w_uq, g_qa, w_ukv, g_kva, g_qn, g_kn, rpb, w_conv,
           w_branch, w_o, w_group, b_group, w_router, b_router, w_gate_e, w_up_e, w_down_e, g_final):
    bsz = x.shape[0]
    n_lat = bsz * SEQ
    n_ctx = bsz * CTX_LEN
    n_all = n_lat + n_ctx
    mod_rows = -(-(bsz + 1) // 8) * 8

    x_all = jnp.concatenate([x.reshape(n_lat, D_MODEL), ctx.reshape(n_ctx, D_MODEL)], axis=0)
    c_all = jnp.concatenate([c, c_ctx[None, :], jnp.zeros((mod_rows - bsz - 1, D_MODEL), F32)], axis=0)
    mods = _modulation(c_all, w_mod, b_mod)
    tables = _rope_tables(_pick(n_all, (512, 256)))

    out = None
    for l in range(DEPTH):
        last = l == DEPTH - 1
        n_act = n_lat if last else n_all
        mod = mods[l]
        w_in_r = _relayout_w_in(w_in[l])
        w_uq_x = _relayout_w_uq(w_uq[l])
        w_ukv_r = _relayout_w_ukv(w_ukv[l])
        w_route = jnp.concatenate([w_group[l], jnp.zeros((D_MODEL, N_EXPERTS - N_GROUPS), F32), w_router[l]], axis=1)
        b_route = jnp.concatenate([b_group[l], jnp.zeros((N_EXPERTS - N_GROUPS,), F32), b_router[l]])[None, :]
        w_conv8 = jnp.concatenate([w_conv[l], jnp.zeros((8 - CONV_WIDTH, CONV_CH), F32)], axis=0)

        h = _hmod(x_all, g_mix[l], mod, bsz)
        p = _matmul(h, w_in_r, 896, BF16, "in_proj")
        qa, ka, va, qb, kb = _prep(p, tables, w_uq_x, w_ukv_r, g_qa[l][None, :], g_kva[l][None, :],
                                   g_qn[l][None, :], g_kn[l][None, :])

        lat_ctx = lambda k_arr, k_off, v_arr, v_off: [(k_arr, k_off, v_arr, v_off, SEQ, 0),
                                                      (k_arr, k_off, v_arr, v_off, CTX_LEN, n_lat)]
        ctx_only = lambda k_arr, k_off, v_arr, v_off: [(k_arr, k_off, v_arr, v_off, CTX_LEN, n_lat)]
        mla = dict(n_groups=MLA_HEADS, rep=1, dk=MLA_DK_PAD, dv=MLA_V, scale=1.0)
        gqa = dict(n_groups=GQA_KV_HEADS, rep=GQA_HEADS // GQA_KV_HEADS, dk=GQA_HEAD_DIM, dv=GQA_HEAD_DIM, scale=1.0)
        nac = dict(n_groups=NA_HEADS, rep=1, dk=NA_HEAD_DIM, dv=NA_HEAD_DIM, scale=NA_SCALE)

        ys_lat = [
            _attention(qa, 0, lat_ctx(ka, 0, va, 0), bsz, SEQ, 0, name="mla_attention", **mla),
            _attention(qb, 0, lat_ctx(kb, 0, p, P_B + 768), bsz, SEQ, 0, name="gqa_attention", **gqa),
            _na_attention(p, _na_bias(rpb[l]), bsz, n_lat),
            _short_conv(p, w_conv8, bsz, SEQ, 0),
        ]
        if last:
            ys_all = ys_lat
        else:
            ys_ctx = [
                _attention(qa, 0, ctx_only(ka, 0, va, 0), bsz, CTX_LEN, n_lat, name="mla_attention_ctx", **mla),
                _attention(qb, 0, ctx_only(kb, 0, p, P_B + 768), bsz, CTX_LEN, n_lat,
                           name="gqa_attention_ctx", **gqa),
                _attention(p, P_C, ctx_only(p, P_C + 512, p, P_C + 1024), bsz, CTX_LEN, n_lat,
                           name="na_attention_ctx", **nac),
                _short_conv(p, w_conv8, bsz, CTX_LEN, n_lat),
            ]
            ys_all = [jnp.concatenate([a, b], axis=0) for a, b in zip(ys_lat, ys_ctx)]

        mixed = _merge(ys_all, p, w_branch[l].astype(BF16), n_act)
        x1 = _out_proj_residual(mixed, w_o[l].astype(BF16), x_all, mod, bsz)
        h2, route = _ffn_input_and_route(x1, g_ffn[l], mod, w_route, b_route, bsz)
        blk_e, tok_buf, n_used, pos = _dispatch_indices(route, n_act)
        ys = _moe(h2, blk_e, tok_buf, n_used, w_gate_e[l], w_up_e[l], w_down_e[l])
        x_all = _combine(ys, pos, x1, route, mod, g_final, bsz, last)
        out = x_all
    return out.reshape(bsz, SEQ, D_MODEL)
```

```python
import functools

import jax
import jax.numpy as jnp
from jax import lax
from jax.experimental import pallas as pl
from jax.experimental.pallas import tpu as pltpu

F32 = jnp.float32
BF16 = jnp.bfloat16

D_MODEL = 2048
SEQ = 2048
DEPTH = 2
CTX_LEN = 256
GRID_W = 64
ROPE_THETA = 10000.0
NORM_EPS = 1e-6
MOD_CHUNKS = 6

MLA_HEADS = 4
MLA_Q_LORA = 512
MLA_KV_LORA = 512
MLA_NOPE = 128
MLA_ROPE = 64
MLA_V = 128
MLA_SCALE = (MLA_NOPE + MLA_ROPE) ** -0.5
MLA_DK_PAD = 256

GQA_HEADS = 4
GQA_KV_HEADS = 2
GQA_HEAD_DIM = 128
GQA_SCALE = GQA_HEAD_DIM ** -0.5

NA_HEADS = 4
NA_HEAD_DIM = 128
NA_WIN_H = 8
NA_WIN_W = 16
NA_SCALE = NA_HEAD_DIM ** -0.5
NA_Q_ROWS = 8
NA_K_ROWS = 16

CONV_CH = 512
CONV_WIDTH = 3
BRANCH_WIDTH = 512
N_BRANCHES = 4

A_COLS = MLA_Q_LORA + MLA_KV_LORA + MLA_ROPE
B_COLS = (GQA_HEADS + 2 * GQA_KV_HEADS) * GQA_HEAD_DIM
C_COLS = 3 * NA_HEADS * NA_HEAD_DIM
D_COLS = 3 * CONV_CH
G_COLS = N_BRANCHES * D_MODEL

P_CQ = 0
P_CKV = 512
P_B = 1024
P_C = 2048
P_D = 3584
P_G = 5120
P_KPE = 13312
P_COLS = 13440

N_GROUPS = 8
EXPERTS_PER_GROUP = 8
N_EXPERTS = 64
TOP_K = 2
EXPERT_HIDDEN = 512
MOE_BLOCK = 128

LANES = 128
VMEM_LIMIT = 56 * 1024 * 1024


def _params(*sem):
    return pltpu.CompilerParams(dimension_semantics=sem, vmem_limit_bytes=VMEM_LIMIT)


def _pick(n, cands):
    for c in cands:
        if n % c == 0:
            return c
    raise ValueError(f"no tile for {n}")


def _rms(x, g):
    return x * lax.rsqrt(jnp.mean(x * x, axis=-1, keepdims=True) + NORM_EPS) * g


def _mod_row(i, tm, bsz):
    return jnp.minimum(i * tm // SEQ, bsz)


def _mod_kernel(c_ref, w_ref, b_ref, o_ref):
    c = c_ref[...]
    s = c * jax.nn.sigmoid(c)
    o_ref[0] = jnp.dot(s, w_ref[0], precision=lax.Precision.HIGHEST,
                       preferred_element_type=F32) + b_ref[0]


def _modulation(c_all, w_mod, b_mod):
    rows = c_all.shape[0]
    n = MOD_CHUNKS * D_MODEL
    tn = 1024
    return pl.pallas_call(
        _mod_kernel,
        grid=(DEPTH, n // tn),
        in_specs=[
            pl.BlockSpec((rows, D_MODEL), lambda l, j: (0, 0)),
            pl.BlockSpec((1, D_MODEL, tn), lambda l, j: (l, 0, j)),
            pl.BlockSpec((1, 1, tn), lambda l, j: (l, 0, j)),
        ],
        out_specs=pl.BlockSpec((1, rows, tn), lambda l, j: (l, 0, j)),
        out_shape=jax.ShapeDtypeStruct((DEPTH, rows, n), F32),
        compiler_params=_params("arbitrary", "arbitrary"),
        name="modulation",
    )(c_all, w_mod, b_mod.reshape(DEPTH, 1, n))


def _hmod_kernel(x_ref, g_ref, sh_ref, sc_ref, o_ref, *, tm, bsz):
    r = _mod_row(pl.program_id(0), tm, bsz)
    y = _rms(x_ref[...], g_ref[...])
    o_ref[...] = (y * (1.0 + sc_ref[pl.ds(r, 1), :]) + sh_ref[pl.ds(r, 1), :]).astype(o_ref.dtype)


def _hmod(x_all, gain, mod, bsz):
    n = x_all.shape[0]
    tm = _pick(n, (512, 256))
    rows = mod.shape[0]
    return pl.pallas_call(
        functools.partial(_hmod_kernel, tm=tm, bsz=bsz),
        grid=(n // tm,),
        in_specs=[
            pl.BlockSpec((tm, D_MODEL), lambda i: (i, 0)),
            pl.BlockSpec((1, D_MODEL), lambda i: (0, 0)),
            pl.BlockSpec((rows, D_MODEL), lambda i: (0, 0)),
            pl.BlockSpec((rows, D_MODEL), lambda i: (0, 1)),
        ],
        out_specs=pl.BlockSpec((tm, D_MODEL), lambda i: (i, 0)),
        out_shape=jax.ShapeDtypeStruct((n, D_MODEL), BF16),
        compiler_params=_params("arbitrary"),
        name="hmod",
    )(x_all, gain.reshape(1, D_MODEL), mod, mod)


def _mm_kernel(a_ref, w_ref, o_ref):
    o_ref[...] = jnp.dot(a_ref[...], w_ref[...], preferred_element_type=F32).astype(o_ref.dtype)


def _matmul(a, w, tn, out_dtype, name):
    m, k = a.shape
    n = w.shape[1]
    tm = _pick(m, (1024, 512, 256))
    return pl.pallas_call(
        _mm_kernel,
        grid=(n // tn, m // tm),
        in_specs=[
            pl.BlockSpec((tm, k), lambda j, i: (i, 0)),
            pl.BlockSpec((k, tn), lambda j, i: (0, j)),
        ],
        out_specs=pl.BlockSpec((tm, tn), lambda j, i: (i, j)),
        out_shape=jax.ShapeDtypeStruct((m, n), out_dtype),
        compiler_params=_params("arbitrary", "arbitrary"),
        name=name,
    )(a, w)


def _mm_res_kernel(a_ref, w_ref, x_ref, ga_ref, o_ref, *, tm, bsz):
    r = _mod_row(pl.program_id(1), tm, bsz)
    acc = jnp.dot(a_ref[...], w_ref[...], preferred_element_type=F32)
    o_ref[...] = x_ref[...] + ga_ref[pl.ds(r, 1), :] * acc


def _out_proj_residual(mixed, w_o, x_all, mod, bsz):
    m = mixed.shape[0]
    tm = _pick(m, (1024, 512, 256))
    tn = 512
    rows = mod.shape[0]
    ga_blk = 2 * D_MODEL // tn
    return pl.pallas_call(
        functools.partial(_mm_res_kernel, tm=tm, bsz=bsz),
        grid=(D_MODEL // tn, m // tm),
        in_specs=[
            pl.BlockSpec((tm, D_MODEL), lambda j, i: (i, 0)),
            pl.BlockSpec((D_MODEL, tn), lambda j, i: (0, j)),
            pl.BlockSpec((tm, tn), lambda j, i: (i, j)),
            pl.BlockSpec((rows, tn), lambda j, i: (0, ga_blk + j)),
        ],
        out_specs=pl.BlockSpec((tm, tn), lambda j, i: (i, j)),
        out_shape=jax.ShapeDtypeStruct((m, D_MODEL), F32),
        compiler_params=_params("arbitrary", "arbitrary"),
        name="out_proj_residual",
    )(mixed, w_o, x_all, mod)


def _prep_kernel(cq_ref, ckv_ref, kpe_ref, bq_ref, bk_ref,
                 cosa_ref, sina_ref, csa_ref, cosb_ref, sinb_ref,
                 wuq_ref, wukv_ref, gqa_ref, gkva_ref, gqn_ref, gkn_ref,
                 qa_ref, ka_ref, va_ref, qb_ref, kb_ref):
    tm = cq_ref.shape[0]
    lane = lax.broadcasted_iota(jnp.int32, (tm, LANES), 1)

    nq = _rms(cq_ref[...].astype(F32), gqa_ref[...]).astype(BF16)
    qf = jnp.dot(nq, wuq_ref[...], preferred_element_type=F32)
    cosa = cosa_ref[...]
    sina = sina_ref[...]
    for h in range(MLA_HEADS):
        base = h * 3 * LANES
        nope = qf[:, base:base + LANES]
        roped = qf[:, base + LANES:base + 2 * LANES] * cosa + qf[:, base + 2 * LANES:base + 3 * LANES] * sina
        qa_ref[:, h * MLA_DK_PAD:h * MLA_DK_PAD + LANES] = (nope * MLA_SCALE).astype(BF16)
        qa_ref[:, h * MLA_DK_PAD + LANES:(h + 1) * MLA_DK_PAD] = (roped * MLA_SCALE).astype(BF16)

    nkv = _rms(ckv_ref[...].astype(F32), gkva_ref[...]).astype(BF16)
    kvf = jnp.dot(nkv, wukv_ref[...], preferred_element_type=F32)
    t = kpe_ref[...].astype(F32) * csa_ref[...]
    kpe = jnp.where(lane < MLA_ROPE, t + pltpu.roll(t, MLA_ROPE, axis=1), 0.0).astype(BF16)
    for h in range(MLA_HEADS):
        ka_ref[:, h * MLA_DK_PAD:h * MLA_DK_PAD + LANES] = kvf[:, h * LANES:(h + 1) * LANES].astype(BF16)
        ka_ref[:, h * MLA_DK_PAD + LANES:(h + 1) * MLA_DK_PAD] = kpe
    va_ref[...] = kvf[:, MLA_HEADS * MLA_NOPE:].astype(BF16)

    cosb = cosb_ref[...]
    sinb = sinb_ref[...]
    first = (lane % (GQA_HEAD_DIM // 2)) < (GQA_HEAD_DIM // 4)

    def rope_b(x):
        rot = jnp.where(first, pltpu.roll(x, LANES - GQA_HEAD_DIM // 4, axis=1),
                        pltpu.roll(x, GQA_HEAD_DIM // 4, axis=1))
        return x * cosb + rot * sinb

    gqn = gqn_ref[...]
    gkn = gkn_ref[...]
    for h in range(GQA_HEADS):
        x = bq_ref[:, h * LANES:(h + 1) * LANES].astype(F32)
        qb_ref[:, h * LANES:(h + 1) * LANES] = (rope_b(_rms(x, gqn)) * GQA_SCALE).astype(BF16)
    for h in range(GQA_KV_HEADS):
        x = bk_ref[:, h * LANES:(h + 1) * LANES].astype(F32)
        kb_ref[:, h * LANES:(h + 1) * LANES] = rope_b(_rms(x, gkn)).astype(BF16)


def _prep(p, tables, w_uq_x, w_ukv_r, g_qa, g_kva, g_qn, g_kn):
    n = p.shape[0]
    tm = _pick(n, (512, 256))
    n_pos = SEQ // tm
    n_lat_tiles = (n // (SEQ + CTX_LEN)) * n_pos

    def tab(i):
        return (jnp.where(i < n_lat_tiles, i % n_pos, n_pos), 0)

    def col(width, off):
        return pl.BlockSpec((tm, width), lambda i: (i, off // width))

    def whole(a):
        return pl.BlockSpec(a.shape, lambda i: (0,) * a.ndim)

    tab_spec = pl.BlockSpec((tm, LANES), tab)
    cosa, sina, csa, cosb, sinb = tables
    outs = [(MLA_HEADS * MLA_DK_PAD, "qa"), (MLA_HEADS * MLA_DK_PAD, "ka"), (MLA_HEADS * MLA_V, "va"),
            (GQA_HEADS * GQA_HEAD_DIM, "qb"), (GQA_KV_HEADS * GQA_HEAD_DIM, "kb")]
    return pl.pallas_call(
        _prep_kernel,
        grid=(n // tm,),
        in_specs=[col(512, P_CQ), col(512, P_CKV), col(LANES, P_KPE), col(512, P_B), col(256, P_B + 512),
                  tab_spec, tab_spec, tab_spec, tab_spec, tab_spec,
                  whole(w_uq_x), whole(w_ukv_r), whole(g_qa), whole(g_kva), whole(g_qn), whole(g_kn)],
        out_specs=[pl.BlockSpec((tm, w), lambda i: (i, 0)) for w, _ in outs],
        out_shape=[jax.ShapeDtypeStruct((n, w), BF16) for w, _ in outs],
        compiler_params=_params("arbitrary"),
        name="attn_prep",
    )(p, p, p, p, p, cosa, sina, csa, cosb, sinb, w_uq_x, w_ukv_r, g_qa, g_kva, g_qn, g_kn)


def _attn_kernel(*refs, n_seg, n_groups, rep, dk, dv, scale):
    q_ref = refs[0]
    segs = [(refs[1 + 2 * s], refs[2 + 2 * s]) for s in range(n_seg)]
    o_ref = refs[1 + 2 * n_seg]
    nt = (((1,), (1,)), ((), ()))
    for g in range(n_groups):
        ks = [k_ref[:, g * dk:(g + 1) * dk] for k_ref, _ in segs]
        vs = [v_ref[:, g * dv:(g + 1) * dv] for _, v_ref in segs]
        for r in range(rep):
            hq = g * rep + r
            q = q_ref[:, hq * dk:(hq + 1) * dk]
            ss = [lax.dot_general(q, k, nt, preferred_element_type=F32) for k in ks]
            if scale != 1.0:
                ss = [s * scale for s in ss]
            m = functools.reduce(jnp.maximum, [jnp.max(s, axis=-1, keepdims=True) for s in ss])
            acc = None
            den = None
            for s, v in zip(ss, vs):
                e = jnp.exp(s - m)
                d = jnp.sum(e, axis=-1, keepdims=True)
                a = jnp.dot(e.astype(BF16), v, preferred_element_type=F32)
                acc = a if acc is None else acc + a
                den = d if den is None else den + d
            o_ref[:, hq * dv:(hq + 1) * dv] = (acc / den).astype(o_ref.dtype)


def _attention(q, q_off, segs, bsz, n_q_per_batch, q_row0, *, n_groups, rep, dk, dv, scale, name):
    tq = min(256, n_q_per_batch)
    nq = n_q_per_batch // tq
    qw = n_groups * rep * dk
    ow = n_groups * rep * dv

    def q_map(b, i):
        return ((q_row0 + b * n_q_per_batch) // tq + i, q_off // qw)

    in_specs = [pl.BlockSpec((tq, qw), q_map)]
    args = [q]
    for k_arr, k_off, v_arr, v_off, length, row0 in segs:
        kw, vw = n_groups * dk, n_groups * dv
        in_specs.append(pl.BlockSpec((length, kw), functools.partial(
            lambda b, i, length, row0, blk: (row0 // length + b, blk), length=length, row0=row0, blk=k_off // kw)))
        in_specs.append(pl.BlockSpec((length, vw), functools.partial(
            lambda b, i, length, row0, blk: (row0 // length + b, blk), length=length, row0=row0, blk=v_off // vw)))
        args += [k_arr, v_arr]
    return pl.pallas_call(
        functools.partial(_attn_kernel, n_seg=len(segs), n_groups=n_groups, rep=rep, dk=dk, dv=dv, scale=scale),
        grid=(bsz, nq),
        in_specs=in_specs,
        out_specs=pl.BlockSpec((tq, ow), lambda b, i: (b * nq + i, 0)),
        out_shape=jax.ShapeDtypeStruct((bsz * n_q_per_batch, ow), BF16),
        compiler_params=_params("arbitrary", "arbitrary"),
        name=name,
    )(*args)


def _na_kernel(q_ref, k_ref, v_ref, kc_ref, vc_ref, bias_ref, o_ref):
    j = pl.program_id(0)
    k_rows = NA_K_ROWS * GRID_W
    start = jnp.clip(NA_Q_ROWS * j - NA_WIN_H // 2, 0, SEQ // GRID_W - NA_K_ROWS) * GRID_W
    start = pl.multiple_of(start, 256)
    nt = (((1,), (1,)), ((), ()))
    q = q_ref[...]
    s1 = lax.dot_general(q, k_ref[pl.ds(start, k_rows), :], nt, preferred_element_type=F32) * NA_SCALE
    s1 = s1 + bias_ref[0, 0]
    s2 = lax.dot_general(q, kc_ref[...], nt, preferred_element_type=F32) * NA_SCALE
    m = jnp.maximum(jnp.max(s1, axis=-1, keepdims=True), jnp.max(s2, axis=-1, keepdims=True))
    e1 = jnp.exp(s1 - m)
    e2 = jnp.exp(s2 - m)
    den = jnp.sum(e1, axis=-1, keepdims=True) + jnp.sum(e2, axis=-1, keepdims=True)
    acc = jnp.dot(e1.astype(BF16), v_ref[pl.ds(start, k_rows), :], preferred_element_type=F32)
    acc = acc + jnp.dot(e2.astype(BF16), vc_ref[...], preferred_element_type=F32)
    o_ref[...] = (acc / den).astype(o_ref.dtype)


def _na_bias(rpb):
    rows = SEQ // GRID_W
    n_tiles = rows // NA_Q_ROWS
    j = jnp.arange(n_tiles)
    k0 = jnp.clip(NA_Q_ROWS * j - NA_WIN_H // 2, 0, rows - NA_K_ROWS)
    rq = NA_Q_ROWS * j[:, None] + jnp.arange(NA_Q_ROWS)[None, :]
    rk = k0[:, None] + jnp.arange(NA_K_ROWS)[None, :]
    r0 = jnp.clip(rq - NA_WIN_H // 2, 0, rows - NA_WIN_H)
    row_ok = (rk[:, None, :] >= r0[:, :, None]) & (rk[:, None, :] < r0[:, :, None] + NA_WIN_H)
    dr = jnp.clip(rk[:, None, :] - rq[:, :, None] + NA_WIN_H - 1, 0, 2 * NA_WIN_H - 2)
    cols = jnp.arange(GRID_W)
    c0 = jnp.clip(cols - NA_WIN_W // 2, 0, GRID_W - NA_WIN_W)
    col_ok = (cols[None, :] >= c0[:, None]) & (cols[None, :] < c0[:, None] + NA_WIN_W)
    dc = jnp.clip(cols[None, :] - cols[:, None] + NA_WIN_W - 1, 0, 2 * NA_WIN_W - 2)
    hi = lax.Precision.HIGHEST
    oh_c = ((dc[:, :, None] == jnp.arange(2 * NA_WIN_W - 1)) & col_ok[:, :, None]).astype(F32)
    oh_r = ((dr[..., None] == jnp.arange(2 * NA_WIN_H - 1)) & row_ok[..., None]).astype(F32)
    by_col = jnp.einsum('hdc,qkc->hdqk', rpb.astype(F32), oh_c, precision=hi)
    b = jnp.einsum('tabd,hdqk->thaqbk', oh_r, by_col, precision=hi)
    ok = row_ok[:, None, :, None, :, None] & col_ok[None, None, None, :, None, :]
    b = jnp.where(ok, b, -jnp.inf)
    return b.reshape(n_tiles, NA_HEADS, NA_Q_ROWS * GRID_W, NA_K_ROWS * GRID_W)


def _na_attention(p, bias, bsz, n_lat):
    tq = NA_Q_ROWS * GRID_W
    n_tiles = SEQ // tq
    hd = NA_HEAD_DIM
    qb, kb, vb = P_C // hd, (P_C + 512) // hd, (P_C + 1024) // hd
    return pl.pallas_call(
        _na_kernel,
        grid=(n_tiles, NA_HEADS, bsz),
        in_specs=[
            pl.BlockSpec((tq, hd), lambda j, h, b: (b * n_tiles + j, qb + h)),
            pl.BlockSpec((SEQ, hd), lambda j, h, b: (b, kb + h)),
            pl.BlockSpec((SEQ, hd), lambda j, h, b: (b, vb + h)),
            pl.BlockSpec((CTX_LEN, hd), lambda j, h, b: (n_lat // CTX_LEN + b, kb + h)),
            pl.BlockSpec((CTX_LEN, hd), lambda j, h, b: (n_lat // CTX_LEN + b, vb + h)),
            pl.BlockSpec((1, 1, tq, NA_K_ROWS * GRID_W), lambda j, h, b: (j, h, 0, 0)),
        ],
        out_specs=pl.BlockSpec((tq, hd), lambda j, h, b: (b * n_tiles + j, h)),
        out_shape=jax.ShapeDtypeStruct((n_lat, NA_HEADS * hd), BF16),
        compiler_params=_params("arbitrary", "arbitrary", "arbitrary"),
        name="na_attention",
    )(p, p, p, p, p, bias)


def _conv_kernel(b_ref, c_ref, u_ref, w_ref, o_ref):
    length = c_ref.shape[0]
    z = c_ref[...].astype(F32) * u_ref[...].astype(F32)
    row = lax.broadcasted_iota(jnp.int32, z.shape, 0)
    z_prev = jnp.where(row == 0, 0.0, pltpu.roll(z, 1, axis=0))
    z_next = jnp.where(row == length - 1, 0.0, pltpu.roll(z, length - 1, axis=0))
    conv = z_prev * w_ref[0:1, :] + z * w_ref[1:2, :] + z_next * w_ref[2:3, :]
    o_ref[...] = (b_ref[...].astype(F32) * conv).astype(o_ref.dtype)


def _short_conv(p, w_conv8, bsz, length, row0):
    blk = P_D // CONV_CH

    def rows(b):
        return row0 // length + b

    return pl.pallas_call(
        _conv_kernel,
        grid=(bsz,),
        in_specs=[
            pl.BlockSpec((length, CONV_CH), lambda b: (rows(b), blk)),
            pl.BlockSpec((length, CONV_CH), lambda b: (rows(b), blk + 1)),
            pl.BlockSpec((length, CONV_CH), lambda b: (rows(b), blk + 2)),
            pl.BlockSpec((8, CONV_CH), lambda b: (0, 0)),
        ],
        out_specs=pl.BlockSpec((length, CONV_CH), lambda b: (b, 0)),
        out_shape=jax.ShapeDtypeStruct((bsz * length, CONV_CH), BF16),
        compiler_params=_params("arbitrary"),
        name="short_conv",
    )(p, p, p, w_conv8)


def _merge_kernel(ya_ref, yb_ref, yc_ref, yd_ref, g0_ref, g1_ref, g2_ref, g3_ref, wb_ref, o_ref):
    acc = None
    for k, (y_ref, g_ref) in enumerate(((ya_ref, g0_ref), (yb_ref, g1_ref), (yc_ref, g2_ref), (yd_ref, g3_ref))):
        proj = jnp.dot(y_ref[...], wb_ref[k], preferred_element_type=F32)
        term = jax.nn.sigmoid(g_ref[...].astype(F32)) * proj
        acc = term if acc is None else acc + term
    o_ref[...] = acc.astype(o_ref.dtype)


def _merge(ys, p, w_branch, n_act):
    tm = _pick(n_act, (1024, 512, 256))
    tn = 512
    g_specs = [pl.BlockSpec((tm, tn), functools.partial(
        lambda j, i, k: (i, (P_G + k * D_MODEL) // tn + j), k=k)) for k in range(N_BRANCHES)]
    return pl.pallas_call(
        _merge_kernel,
        grid=(D_MODEL // tn, n_act // tm),
        in_specs=[pl.BlockSpec((tm, BRANCH_WIDTH), lambda j, i: (i, 0))] * N_BRANCHES + g_specs + [
            pl.BlockSpec((N_BRANCHES, BRANCH_WIDTH, tn), lambda j, i: (0, 0, j))],
        out_specs=pl.BlockSpec((tm, tn), lambda j, i: (i, j)),
        out_shape=jax.ShapeDtypeStruct((n_act, D_MODEL), BF16),
        compiler_params=_params("arbitrary", "arbitrary"),
        name="branch_merge",
    )(*ys, p, p, p, p, w_branch)


def _router_kernel(x_ref, g_ref, sh_ref, sc_ref, wr_ref, br_ref, h_ref, route_ref, *, tm, bsz):
    r = _mod_row(pl.program_id(0), tm, bsz)
    h = _rms(x_ref[...], g_ref[...]) * (1.0 + sc_ref[pl.ds(r, 1), :]) + sh_ref[pl.ds(r, 1), :]
    h_ref[...] = h
    logits = jnp.dot(h, wr_ref[...], precision=lax.Precision.HIGHEST, preferred_element_type=F32)
    bias = br_ref[...]
    lane = lax.broadcasted_iota(jnp.int32, logits.shape, 1)
    lane_f = lane.astype(F32)
    neg = -jnp.inf

    def rmax(x, mask):
        return jnp.max(jnp.where(mask, x, neg), axis=-1, keepdims=True)

    def first_at(x, val, mask):
        return jnp.min(jnp.where(mask & (x == val), lane_f, 2.0 * LANES), axis=-1, keepdims=True)

    def pick(x, idx):
        return jnp.sum(jnp.where(lane_f == idx, x, 0.0), axis=-1, keepdims=True)

    gmask = lane < N_GROUPS
    ge = jnp.where(gmask, jnp.exp(logits - rmax(logits, gmask)), 0.0)
    g_prob = ge / jnp.sum(ge, axis=-1, keepdims=True)
    g_score = g_prob + bias
    g_sel = first_at(g_score, rmax(g_score, gmask), gmask)
    p_group = pick(g_prob, g_sel)

    group_of_lane = lax.shift_right_arithmetic(lane - N_EXPERTS, 3).astype(F32)
    emask = (lane >= N_EXPERTS) & (group_of_lane == g_sel)
    ee = jnp.where(emask, jnp.exp(logits - rmax(logits, emask)), 0.0)
    e_prob = ee / jnp.sum(ee, axis=-1, keepdims=True)
    e_score = e_prob + bias
    i1 = first_at(e_score, rmax(e_score, emask), emask)
    mask2 = emask & (lane_f != i1)
    i2 = first_at(e_score, rmax(e_score, mask2), mask2)
    p1 = pick(e_prob, i1)
    p2 = pick(e_prob, i2)
    w1 = p_group * p1 / (p1 + p2)
    w2 = p_group * p2 / (p1 + p2)
    e1 = i1 - N_EXPERTS
    e2 = i2 - N_EXPERTS
    route_ref[...] = jnp.where(lane == 0, e1, jnp.where(lane == 1, e2, jnp.where(
        lane == 2, w1, jnp.where(lane == 3, w2, 0.0))))


def _ffn_input_and_route(x1, gain, mod, w_route, b_route, bsz):
    n = x1.shape[0]
    tm = _pick(n, (512, 256))
    rows = mod.shape[0]
    return pl.pallas_call(
        functools.partial(_router_kernel, tm=tm, bsz=bsz),
        grid=(n // tm,),
        in_specs=[
            pl.BlockSpec((tm, D_MODEL), lambda i: (i, 0)),
            pl.BlockSpec((1, D_MODEL), lambda i: (0, 0)),
            pl.BlockSpec((rows, D_MODEL), lambda i: (0, 3)),
            pl.BlockSpec((rows, D_MODEL), lambda i: (0, 4)),
            pl.BlockSpec((D_MODEL, LANES), lambda i: (0, 0)),
            pl.BlockSpec((1, LANES), lambda i: (0, 0)),
        ],
        out_specs=[pl.BlockSpec((tm, D_MODEL), lambda i: (i, 0)),
                   pl.BlockSpec((tm, LANES), lambda i: (i, 0))],
        out_shape=[jax.ShapeDtypeStruct((n, D_MODEL), F32), jax.ShapeDtypeStruct((n, LANES), F32)],
        compiler_params=_params("arbitrary"),
        name="ffn_input_route",
    )(x1, gain.reshape(1, D_MODEL), mod, mod, w_route, b_route)


def _row_copy(src_hbm, row, dst, slot, dst_row, sem):
    return pltpu.make_async_copy(src_hbm.at[pl.ds(row, 1), :], dst.at[slot, pl.ds(dst_row, 1), :], sem.at[slot])


def _moe_kernel(be_ref, tok_ref, nused_ref, h_hbm, wg_ref, wu_ref, wd_ref, o_ref,
                xbuf, wgb, wub, wdb, sem):
    i = pl.program_id(0)
    slot = i % 2
    n_used = nused_ref[0]

    def gather(blk, s):
        def body(r, carry):
            _row_copy(h_hbm, tok_ref[blk * MOE_BLOCK + r], xbuf, s, r, sem).start()
            return carry
        lax.fori_loop(0, MOE_BLOCK, body, 0)

    @pl.when((i == 0) & (n_used > 0))
    def _():
        gather(0, 0)

    @pl.when(i + 1 < n_used)
    def _():
        gather(i + 1, 1 - slot)

    @pl.when((i == 0) | (be_ref[i] != be_ref[jnp.maximum(i - 1, 0)]))
    def _():
        wgb[...] = wg_ref[0, 0].astype(BF16)
        wub[...] = wu_ref[0, 0].astype(BF16)
        wdb[...] = wd_ref[0, 0].astype(BF16)

    @pl.when(i < n_used)
    def _():
        pltpu.make_async_copy(h_hbm.at[pl.ds(0, MOE_BLOCK), :], xbuf.at[slot], sem.at[slot]).wait()
        x = xbuf[slot].astype(BF16)
        gate = jnp.dot(x, wgb[...], preferred_element_type=F32)
        up = jnp.dot(x, wub[...], preferred_element_type=F32)
        hid = (gate * jax.nn.sigmoid(gate) * up).astype(BF16)
        o_ref[...] = jnp.dot(hid, wdb[...], preferred_element_type=F32)

    @pl.when(i >= n_used)
    def _():
        o_ref[...] = jnp.zeros_like(o_ref)


def _moe(h2, blk_e, tok_buf, n_used, w_gate, w_up, w_down, layer):
    n_blocks = blk_e.shape[0]
    grid_spec = pltpu.PrefetchScalarGridSpec(
        num_scalar_prefetch=3,
        grid=(n_blocks,),
        in_specs=[
            pl.BlockSpec(memory_space=pl.ANY),
            pl.BlockSpec((1, 1, D_MODEL, EXPERT_HIDDEN), lambda i, be, tok, nu: (layer, be[i], 0, 0)),
            pl.BlockSpec((1, 1, D_MODEL, EXPERT_HIDDEN), lambda i, be, tok, nu: (layer, be[i], 0, 0)),
            pl.BlockSpec((1, 1, EXPERT_HIDDEN, D_MODEL), lambda i, be, tok, nu: (layer, be[i], 0, 0)),
        ],
        out_specs=pl.BlockSpec((MOE_BLOCK, D_MODEL), lambda i, be, tok, nu: (i, 0)),
        scratch_shapes=[
            pltpu.VMEM((2, MOE_BLOCK, D_MODEL), F32),
            pltpu.VMEM((D_MODEL, EXPERT_HIDDEN), BF16),
            pltpu.VMEM((D_MODEL, EXPERT_HIDDEN), BF16),
            pltpu.VMEM((EXPERT_HIDDEN, D_MODEL), BF16),
            pltpu.SemaphoreType.DMA((2,)),
        ],
    )
    return pl.pallas_call(
        _moe_kernel,
        grid_spec=grid_spec,
        out_shape=jax.ShapeDtypeStruct((n_blocks * MOE_BLOCK, D_MODEL), F32),
        compiler_params=_params("arbitrary"),
        name="moe_experts",
    )(blk_e, tok_buf, n_used, h2, w_gate, w_up, w_down)


def _combine_kernel(pos_ref, ys_hbm, x_ref, route_ref, ga_ref, gf_ref, o_ref, rbuf, sem, *, tm, bsz, final):
    i = pl.program_id(0)
    n = pl.num_programs(0)
    slot = i % 2

    def gather(blk, s):
        def body(r, carry):
            for k in range(TOP_K):
                _row_copy(ys_hbm, pos_ref[(blk * tm + r) * TOP_K + k], rbuf, s, k * tm + r, sem).start()
            return carry
        lax.fori_loop(0, tm, body, 0)

    @pl.when(i == 0)
    def _():
        gather(0, 0)

    @pl.when(i + 1 < n)
    def _():
        gather(i + 1, 1 - slot)

    pltpu.make_async_copy(ys_hbm.at[pl.ds(0, TOP_K * tm), :], rbuf.at[slot], sem.at[slot]).wait()
    w0 = route_ref[:, 2:3]
    w1 = route_ref[:, 3:4]
    y = rbuf[slot, 0:tm, :] * w0 + rbuf[slot, tm:2 * tm, :] * w1
    r = _mod_row(i, tm, bsz)
    x = x_ref[...] + ga_ref[pl.ds(r, 1), :] * y
    if final:
        x = _rms(x, gf_ref[...])
    o_ref[...] = x


def _combine(ys, pos, x1, route, mod, g_final, bsz, final):
    n = x1.shape[0]
    tm = 128
    rows = mod.shape[0]
    grid_spec = pltpu.PrefetchScalarGridSpec(
        num_scalar_prefetch=1,
        grid=(n // tm,),
        in_specs=[
            pl.BlockSpec(memory_space=pl.ANY),
            pl.BlockSpec((tm, D_MODEL), lambda i, pos: (i, 0)),
            pl.BlockSpec((tm, LANES), lambda i, pos: (i, 0)),
            pl.BlockSpec((rows, D_MODEL), lambda i, pos: (0, 5)),
            pl.BlockSpec((1, D_MODEL), lambda i, pos: (0, 0)),
        ],
        out_specs=pl.BlockSpec((tm, D_MODEL), lambda i, pos: (i, 0)),
        scratch_shapes=[pltpu.VMEM((2, TOP_K * tm, D_MODEL), F32), pltpu.SemaphoreType.DMA((2,))],
    )
    return pl.pallas_call(
        functools.partial(_combine_kernel, tm=tm, bsz=bsz, final=final),
        grid_spec=grid_spec,
        out_shape=jax.ShapeDtypeStruct((n, D_MODEL), F32),
        compiler_params=_params("arbitrary"),
        name="moe_combine",
    )(pos, ys, x1, route, mod, g_final.reshape(1, D_MODEL))


def _dispatch_indices(route, n_tok):
    flat_e = route[:, :TOP_K].astype(jnp.int32).reshape(-1)
    n_assign = n_tok * TOP_K
    order = jnp.argsort(flat_e).astype(jnp.int32)
    se = flat_e[order]
    counts = jnp.bincount(flat_e, length=N_EXPERTS).astype(jnp.int32)
    padded = (counts + MOE_BLOCK - 1) // MOE_BLOCK * MOE_BLOCK
    pad_end = jnp.cumsum(padded)
    pad_start = pad_end - padded
    start = jnp.cumsum(counts) - counts
    dest = (pad_start[se] + jnp.arange(n_assign, dtype=jnp.int32) - start[se]).astype(jnp.int32)
    n_blocks = (n_assign + N_EXPERTS * (MOE_BLOCK - 1)) // MOE_BLOCK
    tok_buf = jnp.zeros((n_blocks * MOE_BLOCK,), jnp.int32).at[dest].set(order // TOP_K)
    pos = jnp.zeros((n_assign,), jnp.int32).at[order].set(dest)
    blk_e = jnp.clip(jnp.searchsorted(pad_end, jnp.arange(n_blocks, dtype=jnp.int32) * MOE_BLOCK, side='right'),
                     0, N_EXPERTS - 1).astype(jnp.int32)
    n_used = (pad_end[-1:] // MOE_BLOCK).astype(jnp.int32)
    return blk_e, tok_buf, n_used, pos


def _rotate_half_cols(w, quarter):
    r1, r2, c1, c2 = (w[..., k * quarter:(k + 1) * quarter] for k in range(4))
    return jnp.concatenate([-r2, r1, -c2, c1], axis=-1)


def _relayout_w_in(w_in):
    kpe = w_in[:, MLA_Q_LORA + MLA_KV_LORA:A_COLS]
    return jnp.concatenate([w_in[:, :MLA_Q_LORA + MLA_KV_LORA], w_in[:, A_COLS:], kpe,
                            _rotate_half_cols(kpe, MLA_ROPE // 4)], axis=1).astype(BF16)


def _relayout_w_uq(w_uq):
    w = w_uq.reshape(MLA_Q_LORA, MLA_HEADS, MLA_NOPE + MLA_ROPE)
    nope, pe = w[..., :MLA_NOPE], w[..., MLA_NOPE:]
    z = jnp.zeros_like(pe)
    out = jnp.concatenate([nope, pe, z, _rotate_half_cols(pe, MLA_ROPE // 4), z], axis=-1)
    return out.reshape(MLA_Q_LORA, MLA_HEADS * 3 * LANES).astype(BF16)


def _relayout_w_ukv(w_ukv):
    w = w_ukv.reshape(MLA_KV_LORA, MLA_HEADS, MLA_NOPE + MLA_V)
    return jnp.concatenate([w[..., :MLA_NOPE].reshape(MLA_KV_LORA, -1),
                            w[..., MLA_NOPE:].reshape(MLA_KV_LORA, -1)], axis=-1).astype(BF16)


def _rope_tables(tm):
    t = jnp.arange(SEQ)
    rows = (t // GRID_W).astype(F32)
    cols = (t % GRID_W).astype(F32)

    def cs(rot_dim):
        quarter = rot_dim // 4
        inv_freq = ROPE_THETA ** (-jnp.arange(quarter, dtype=F32) / quarter)
        ang_r = rows[:, None] * inv_freq[None, :]
        ang_c = cols[:, None] * inv_freq[None, :]
        cos = jnp.concatenate([jnp.cos(ang_r)] * 2 + [jnp.cos(ang_c)] * 2, axis=-1)
        sin = jnp.concatenate([jnp.sin(ang_r)] * 2 + [jnp.sin(ang_c)] * 2, axis=-1)
        return cos, sin

    def with_identity(tab, ident):
        return jnp.concatenate([tab, jnp.broadcast_to(ident, (tm, LANES))], axis=0)

    cos_a, sin_a = cs(MLA_ROPE)
    zeros = jnp.zeros((SEQ, LANES - MLA_ROPE), F32)
    lane = jnp.arange(LANES)
    one_lo = (lane < MLA_ROPE).astype(F32)[None, :]
    cosa = with_identity(jnp.concatenate([cos_a, zeros], axis=1), one_lo)
    sina = with_identity(jnp.concatenate([sin_a, zeros], axis=1), jnp.zeros((1, LANES), F32))
    csa = with_identity(jnp.concatenate([cos_a, sin_a], axis=1), one_lo)
    cos_b, sin_b = cs(GQA_HEAD_DIM)
    sign = jnp.where((lane % (GQA_HEAD_DIM // 2)) < GQA_HEAD_DIM // 4, -1.0, 1.0).astype(F32)[None, :]
    cosb = with_identity(cos_b, jnp.ones((1, LANES), F32))
    sinb = with_identity(sin_b * sign, jnp.zeros((1, LANES), F32))
    return cosa, sina, csa, cosb, sinb


def kernel(x, c, ctx, c_ctx, w_mod, b_mod, g_mix, g_ffn, w_in, w_uq, g_qa, w_ukv, g_kva, g_qn, g_kn, rpb, w_conv,
           w_branch, w_o, w_group, b_group, w_router, b_router, w_gate_e, w_up_e, w_down_e, g_final):
    bsz = x.shape[0]
    n_lat = bsz * SEQ
    n_ctx = bsz * CTX_LEN
    n_all = n_lat + n_ctx
    mod_rows = -(-(bsz + 1) // 8) * 8

    x_all = jnp.concatenate([x.reshape(n_lat, D_MODEL), ctx.reshape(n_ctx, D_MODEL)], axis=0)
    c_all = jnp.concatenate([c, c_ctx[None, :], jnp.zeros((mod_rows - bsz - 1, D_MODEL), F32)], axis=0)
    mods = _modulation(c_all, w_mod, b_mod)
    tables = _rope_tables(_pick(n_all, (512, 256)))

    out = None
    for l in range(DEPTH):
        last = l == DEPTH - 1
        n_act = n_lat if last else n_all
        mod = mods[l]
        w_in_r = _relayout_w_in(w_in[l])
        w_uq_x = _relayout_w_uq(w_uq[l])
        w_ukv_r = _relayout_w_ukv(w_ukv[l])
        w_route = jnp.concatenate([w_group[l], jnp.zeros((D_MODEL, N_EXPERTS - N_GROUPS), F32), w_router[l]], axis=1)
        b_route = jnp.concatenate([b_group[l], jnp.zeros((N_EXPERTS - N_GROUPS,), F32), b_router[l]])[None, :]
        w_conv8 = jnp.concatenate([w_conv[l], jnp.zeros((8 - CONV_WIDTH, CONV_CH), F32)], axis=0)

        h = _hmod(x_all, g_mix[l], mod, bsz)
        p = _matmul(h, w_in_r, 896, BF16, "in_proj")
        qa, ka, va, qb, kb = _prep(p, tables, w_uq_x, w_ukv_r, g_qa[l][None, :], g_kva[l][None, :],
                                   g_qn[l][None, :], g_kn[l][None, :])

        lat_ctx = lambda k_arr, k_off, v_arr, v_off: [(k_arr, k_off, v_arr, v_off, SEQ, 0),
                                                      (k_arr, k_off, v_arr, v_off, CTX_LEN, n_lat)]
        ctx_only = lambda k_arr, k_off, v_arr, v_off: [(k_arr, k_off, v_arr, v_off, CTX_LEN, n_lat)]
        mla = dict(n_groups=MLA_HEADS, rep=1, dk=MLA_DK_PAD, dv=MLA_V, scale=1.0)
        gqa = dict(n_groups=GQA_KV_HEADS, rep=GQA_HEADS // GQA_KV_HEADS, dk=GQA_HEAD_DIM, dv=GQA_HEAD_DIM, scale=1.0)
        nac = dict(n_groups=NA_HEADS, rep=1, dk=NA_HEAD_DIM, dv=NA_HEAD_DIM, scale=NA_SCALE)

        ys_lat = [
            _attention(qa, 0, lat_ctx(ka, 0, va, 0), bsz, SEQ, 0, name="mla_attention", **mla),
            _attention(qb, 0, lat_ctx(kb, 0, p, P_B + 768), bsz, SEQ, 0, name="gqa_attention", **gqa),
            _na_attention(p, _na_bias(rpb[l]), bsz, n_lat),
            _short_conv(p, w_conv8, bsz, SEQ, 0),
        ]
        if last:
            ys_all = ys_lat
        else:
            ys_ctx = [
                _attention(qa, 0, ctx_only(ka, 0, va, 0), bsz, CTX_LEN, n_lat, name="mla_attention_ctx", **mla),
                _attention(qb, 0, ctx_only(kb, 0, p, P_B + 768), bsz, CTX_LEN, n_lat,
                           name="gqa_attention_ctx", **gqa),
                _attention(p, P_C, ctx_only(p, P_C + 512, p, P_C + 1024), bsz, CTX_LEN, n_lat,
                           name="na_attention_ctx", **nac),
                _short_conv(p, w_conv8, bsz, CTX_LEN, n_lat),
            ]
            ys_all = [jnp.concatenate([a, b], axis=0) for a, b in zip(ys_lat, ys_ctx)]

        mixed = _merge(ys_all, p, w_branch[l].astype(BF16), n_act)
        x1 = _out_proj_residual(mixed, w_o[l].astype(BF16), x_all, mod, bsz)
        h2, route = _ffn_input_and_route(x1, g_ffn[l], mod, w_route, b_route, bsz)
        blk_e, tok_buf, n_used, pos = _dispatch_indices(route, n_act)
        ys = _moe(h2, blk_e, tok_buf, n_used, w_gate_e, w_up_e, w_down_e, l)
        x_all = _combine(ys, pos, x1, route, mod, g_final, bsz, last)
        out = x_all
    return out.reshape(bsz, SEQ, D_MODEL)
```

```python
import functools

import jax
import jax.numpy as jnp
from jax import lax
from jax.experimental import pallas as pl
from jax.experimental.pallas import tpu as pltpu

F32 = jnp.float32
BF16 = jnp.bfloat16

D_MODEL = 2048
SEQ = 2048
DEPTH = 2
CTX_LEN = 256
GRID_W = 64
ROPE_THETA = 10000.0
NORM_EPS = 1e-6
MOD_CHUNKS = 6

MLA_HEADS = 4
MLA_Q_LORA = 512
MLA_KV_LORA = 512
MLA_NOPE = 128
MLA_ROPE = 64
MLA_V = 128
MLA_SCALE = (MLA_NOPE + MLA_ROPE) ** -0.5
MLA_DK_PAD = 256

GQA_HEADS = 4
GQA_KV_HEADS = 2
GQA_HEAD_DIM = 128
GQA_SCALE = GQA_HEAD_DIM ** -0.5

NA_HEADS = 4
NA_HEAD_DIM = 128
NA_WIN_H = 8
NA_WIN_W = 16
NA_SCALE = NA_HEAD_DIM ** -0.5
NA_Q_ROWS = 8
NA_K_ROWS = 16

CONV_CH = 512
CONV_WIDTH = 3
BRANCH_WIDTH = 512
N_BRANCHES = 4

A_COLS = MLA_Q_LORA + MLA_KV_LORA + MLA_ROPE
B_COLS = (GQA_HEADS + 2 * GQA_KV_HEADS) * GQA_HEAD_DIM
C_COLS = 3 * NA_HEADS * NA_HEAD_DIM
D_COLS = 3 * CONV_CH
G_COLS = N_BRANCHES * D_MODEL

P_B = 0
P_C = P_B + B_COLS
P_D = P_C + C_COLS
P_G = P_D + D_COLS
P_CQ = P_G + G_COLS
P_CKV = P_CQ + MLA_Q_LORA
P_COLS = P_CKV + MLA_KV_LORA
IN_TILE = 1024

N_GROUPS = 8
EXPERTS_PER_GROUP = 8
N_EXPERTS = 64
TOP_K = 2
EXPERT_HIDDEN = 512
MOE_BLOCK = 128

LANES = 128
VMEM_LIMIT = 56 * 1024 * 1024


def _params(*sem):
    return pltpu.CompilerParams(dimension_semantics=sem, vmem_limit_bytes=VMEM_LIMIT)


def _pick(n, cands):
    for c in cands:
        if n % c == 0:
            return c
    raise ValueError(f"no tile for {n}")


def _rms(x, g):
    return x * lax.rsqrt(jnp.mean(x * x, axis=-1, keepdims=True) + NORM_EPS) * g


def _mod_row(i, tm, bsz):
    return jnp.minimum(i * tm // SEQ, bsz)


def _mod_kernel(c_ref, w_ref, b_ref, o_ref):
    c = c_ref[...]
    s = c * jax.nn.sigmoid(c)
    o_ref[0] = jnp.dot(s, w_ref[0], precision=lax.Precision.HIGHEST,
                       preferred_element_type=F32) + b_ref[0]


def _modulation(c_all, w_mod, b_mod):
    rows = c_all.shape[0]
    n = MOD_CHUNKS * D_MODEL
    tn = 1024
    return pl.pallas_call(
        _mod_kernel,
        grid=(DEPTH, n // tn),
        in_specs=[
            pl.BlockSpec((rows, D_MODEL), lambda l, j: (0, 0)),
            pl.BlockSpec((1, D_MODEL, tn), lambda l, j: (l, 0, j)),
            pl.BlockSpec((1, 1, tn), lambda l, j: (l, 0, j)),
        ],
        out_specs=pl.BlockSpec((1, rows, tn), lambda l, j: (l, 0, j)),
        out_shape=jax.ShapeDtypeStruct((DEPTH, rows, n), F32),
        compiler_params=_params("arbitrary", "arbitrary"),
        name="modulation",
    )(c_all, w_mod, b_mod.reshape(DEPTH, 1, n))


def _hmod_kernel(x_ref, g_ref, sh_ref, sc_ref, o_ref, *, tm, bsz):
    r = _mod_row(pl.program_id(0), tm, bsz)
    y = _rms(x_ref[...], g_ref[...])
    o_ref[...] = (y * (1.0 + sc_ref[pl.ds(r, 1), :]) + sh_ref[pl.ds(r, 1), :]).astype(o_ref.dtype)


def _hmod(x_all, gain, mod, bsz):
    n = x_all.shape[0]
    tm = _pick(n, (512, 256))
    rows = mod.shape[0]
    return pl.pallas_call(
        functools.partial(_hmod_kernel, tm=tm, bsz=bsz),
        grid=(n // tm,),
        in_specs=[
            pl.BlockSpec((tm, D_MODEL), lambda i: (i, 0)),
            pl.BlockSpec((1, D_MODEL), lambda i: (0, 0)),
            pl.BlockSpec((rows, D_MODEL), lambda i: (0, 0)),
            pl.BlockSpec((rows, D_MODEL), lambda i: (0, 1)),
        ],
        out_specs=pl.BlockSpec((tm, D_MODEL), lambda i: (i, 0)),
        out_shape=jax.ShapeDtypeStruct((n, D_MODEL), BF16),
        compiler_params=_params("arbitrary"),
        name="hmod",
    )(x_all, gain.reshape(1, D_MODEL), mod, mod)


NT_DIMS = (((1,), (1,)), ((), ()))


def _mm_nt_kernel(a_ref, w_ref, o_ref):
    o_ref[...] = lax.dot_general(a_ref[...], w_ref[0], NT_DIMS, preferred_element_type=F32).astype(o_ref.dtype)


def _matmul_nt(a, wt, layer, out_dtype, name):
    m, k = a.shape
    n = wt.shape[1]
    tm = _pick(m, (1024, 512, 256))
    return pl.pallas_call(
        _mm_nt_kernel,
        grid=(m // tm,),
        in_specs=[pl.BlockSpec((tm, k), lambda i: (i, 0)), pl.BlockSpec((1, n, k), lambda i: (layer, 0, 0))],
        out_specs=pl.BlockSpec((tm, n), lambda i: (i, 0)),
        out_shape=jax.ShapeDtypeStruct((m, n), out_dtype),
        compiler_params=_params("arbitrary"),
        name=name,
    )(a, wt)


def _in_proj_kernel(a_ref, wt_hbm, o_ref, wbuf, wbf, sem, *, layer):
    j = pl.program_id(0)
    n_tiles = pl.num_programs(0)
    n_shift = P_CQ // IN_TILE

    def w_copy(jj, slot):
        row = pl.multiple_of(jnp.where(jj < n_shift, A_COLS + jj * IN_TILE, 0), MLA_ROPE)
        return pltpu.make_async_copy(wt_hbm.at[layer, pl.ds(row, IN_TILE), :], wbuf.at[slot], sem.at[slot])

    @pl.when(pl.program_id(1) == 0)
    def _():
        @pl.when(j == 0)
        def _():
            w_copy(0, 0).start()

        @pl.when(j + 1 < n_tiles)
        def _():
            w_copy(j + 1, (j + 1) % 2).start()

        w_copy(j, j % 2).wait()
        wbf[...] = wbuf[j % 2].astype(BF16)

    o_ref[...] = lax.dot_general(a_ref[...], wbf[...], NT_DIMS, preferred_element_type=F32).astype(o_ref.dtype)


def _in_proj(h, wt, layer):
    m, k = h.shape
    tm = _pick(m, (1024, 512, 256))
    return pl.pallas_call(
        functools.partial(_in_proj_kernel, layer=layer),
        grid=(P_COLS // IN_TILE, m // tm),
        in_specs=[pl.BlockSpec((tm, k), lambda j, i: (i, 0)), pl.BlockSpec(memory_space=pl.ANY)],
        out_specs=pl.BlockSpec((tm, IN_TILE), lambda j, i: (i, j)),
        out_shape=jax.ShapeDtypeStruct((m, P_COLS), BF16),
        scratch_shapes=[pltpu.VMEM((2, IN_TILE, k), F32), pltpu.VMEM((IN_TILE, k), BF16),
                        pltpu.SemaphoreType.DMA((2,))],
        compiler_params=_params("arbitrary", "arbitrary"),
        name="in_proj",
    )(h, wt)


def _mm_res_kernel(a_ref, w_ref, x_ref, ga_ref, o_ref, *, tm, bsz):
    r = _mod_row(pl.program_id(1), tm, bsz)
    acc = jnp.dot(a_ref[...], w_ref[0], preferred_element_type=F32)
    o_ref[...] = x_ref[...] + ga_ref[pl.ds(r, 1), :] * acc


def _out_proj_residual(mixed, w_o, layer, x_all, mod, bsz):
    m = mixed.shape[0]
    tm = _pick(m, (1024, 512, 256))
    tn = 512
    rows = mod.shape[0]
    ga_blk = 2 * D_MODEL // tn
    return pl.pallas_call(
        functools.partial(_mm_res_kernel, tm=tm, bsz=bsz),
        grid=(D_MODEL // tn, m // tm),
        in_specs=[
            pl.BlockSpec((tm, D_MODEL), lambda j, i: (i, 0)),
            pl.BlockSpec((1, D_MODEL, tn), lambda j, i: (layer, 0, j)),
            pl.BlockSpec((tm, tn), lambda j, i: (i, j)),
            pl.BlockSpec((rows, tn), lambda j, i: (0, ga_blk + j)),
        ],
        out_specs=pl.BlockSpec((tm, tn), lambda j, i: (i, j)),
        out_shape=jax.ShapeDtypeStruct((m, D_MODEL), F32),
        compiler_params=_params("arbitrary", "arbitrary"),
        name="out_proj_residual",
    )(mixed, w_o, x_all, mod)


def _prep_kernel(cq_ref, ckv_ref, kpe_ref, bq_ref, bk_ref,
                 cosa_ref, sina_ref, csa_ref, cosb_ref, sinb_ref,
                 wuq_ref, wukv_ref, gqa_ref, gkva_ref, gqn_ref, gkn_ref,
                 qa_ref, ka_ref, va_ref, qb_ref, kb_ref):
    tm = cq_ref.shape[0]
    lane = lax.broadcasted_iota(jnp.int32, (tm, LANES), 1)

    nq = _rms(cq_ref[...].astype(F32), gqa_ref[...]).astype(BF16)
    qf = jnp.dot(nq, wuq_ref[...], preferred_element_type=F32)
    cosa = cosa_ref[...]
    sina = sina_ref[...]
    for h in range(MLA_HEADS):
        base = h * 3 * LANES
        nope = qf[:, base:base + LANES]
        roped = qf[:, base + LANES:base + 2 * LANES] * cosa + qf[:, base + 2 * LANES:base + 3 * LANES] * sina
        qa_ref[:, h * MLA_DK_PAD:h * MLA_DK_PAD + LANES] = (nope * MLA_SCALE).astype(BF16)
        qa_ref[:, h * MLA_DK_PAD + LANES:(h + 1) * MLA_DK_PAD] = (roped * MLA_SCALE).astype(BF16)

    nkv = _rms(ckv_ref[...].astype(F32), gkva_ref[...]).astype(BF16)
    kvf = jnp.dot(nkv, wukv_ref[...], preferred_element_type=F32)
    t = kpe_ref[...].astype(F32) * csa_ref[...]
    kpe = jnp.where(lane < MLA_ROPE, t + pltpu.roll(t, MLA_ROPE, axis=1), 0.0).astype(BF16)
    for h in range(MLA_HEADS):
        ka_ref[:, h * MLA_DK_PAD:h * MLA_DK_PAD + LANES] = kvf[:, h * LANES:(h + 1) * LANES].astype(BF16)
        ka_ref[:, h * MLA_DK_PAD + LANES:(h + 1) * MLA_DK_PAD] = kpe
    va_ref[...] = kvf[:, MLA_HEADS * MLA_NOPE:].astype(BF16)

    cosb = cosb_ref[...]
    sinb = sinb_ref[...]
    first = (lane % (GQA_HEAD_DIM // 2)) < (GQA_HEAD_DIM // 4)

    def rope_b(x):
        rot = jnp.where(first, pltpu.roll(x, LANES - GQA_HEAD_DIM // 4, axis=1),
                        pltpu.roll(x, GQA_HEAD_DIM // 4, axis=1))
        return x * cosb + rot * sinb

    gqn = gqn_ref[...]
    gkn = gkn_ref[...]
    for h in range(GQA_HEADS):
        x = bq_ref[:, h * LANES:(h + 1) * LANES].astype(F32)
        qb_ref[:, h * LANES:(h + 1) * LANES] = (rope_b(_rms(x, gqn)) * GQA_SCALE).astype(BF16)
    for h in range(GQA_KV_HEADS):
        x = bk_ref[:, h * LANES:(h + 1) * LANES].astype(F32)
        kb_ref[:, h * LANES:(h + 1) * LANES] = rope_b(_rms(x, gkn)).astype(BF16)


def _prep(p, pk, tables, w_uq_x, w_ukv_r, g_qa, g_kva, g_qn, g_kn):
    n = p.shape[0]
    tm = _pick(n, (512, 256))
    n_pos = SEQ // tm
    n_lat_tiles = (n // (SEQ + CTX_LEN)) * n_pos

    def tab(i):
        return (jnp.where(i < n_lat_tiles, i % n_pos, n_pos), 0)

    def col(width, off):
        return pl.BlockSpec((tm, width), lambda i: (i, off // width))

    def whole(a):
        return pl.BlockSpec(a.shape, lambda i: (0,) * a.ndim)

    tab_spec = pl.BlockSpec((tm, LANES), tab)
    cosa, sina, csa, cosb, sinb = tables
    outs = [(MLA_HEADS * MLA_DK_PAD, "qa"), (MLA_HEADS * MLA_DK_PAD, "ka"), (MLA_HEADS * MLA_V, "va"),
            (GQA_HEADS * GQA_HEAD_DIM, "qb"), (GQA_KV_HEADS * GQA_HEAD_DIM, "kb")]
    return pl.pallas_call(
        _prep_kernel,
        grid=(n // tm,),
        in_specs=[col(512, P_CQ), col(512, P_CKV), col(LANES, 0), col(512, P_B), col(256, P_B + 512),
                  tab_spec, tab_spec, tab_spec, tab_spec, tab_spec,
                  whole(w_uq_x), whole(w_ukv_r), whole(g_qa), whole(g_kva), whole(g_qn), whole(g_kn)],
        out_specs=[pl.BlockSpec((tm, w), lambda i: (i, 0)) for w, _ in outs],
        out_shape=[jax.ShapeDtypeStruct((n, w), BF16) for w, _ in outs],
        compiler_params=_params("arbitrary"),
        name="attn_prep",
    )(p, p, pk, p, p, cosa, sina, csa, cosb, sinb, w_uq_x, w_ukv_r, g_qa, g_kva, g_qn, g_kn)


def _attn_kernel(*refs, n_seg, n_groups, rep, dk, dv, scale):
    q_ref = refs[0]
    segs = [(refs[1 + 2 * s], refs[2 + 2 * s]) for s in range(n_seg)]
    o_ref = refs[1 + 2 * n_seg]
    nt = (((1,), (1,)), ((), ()))
    for g in range(n_groups):
        ks = [k_ref[:, g * dk:(g + 1) * dk] for k_ref, _ in segs]
        vs = [v_ref[:, g * dv:(g + 1) * dv] for _, v_ref in segs]
        for r in range(rep):
            hq = g * rep + r
            q = q_ref[:, hq * dk:(hq + 1) * dk]
            ss = [lax.dot_general(q, k, nt, preferred_element_type=F32) for k in ks]
            if scale != 1.0:
                ss = [s * scale for s in ss]
            m = functools.reduce(jnp.maximum, [jnp.max(s, axis=-1, keepdims=True) for s in ss])
            acc = None
            den = None
            for s, v in zip(ss, vs):
                e = jnp.exp(s - m)
                d = jnp.sum(e, axis=-1, keepdims=True)
                a = jnp.dot(e.astype(BF16), v, preferred_element_type=F32)
                acc = a if acc is None else acc + a
                den = d if den is None else den + d
            o_ref[:, hq * dv:(hq + 1) * dv] = (acc / den).astype(o_ref.dtype)


def _attention(q, q_off, segs, bsz, n_q_per_batch, q_row0, *, n_groups, rep, dk, dv, scale, name):
    tq = min(256, n_q_per_batch)
    nq = n_q_per_batch // tq
    qw = n_groups * rep * dk
    ow = n_groups * rep * dv

    def q_map(b, i):
        return ((q_row0 + b * n_q_per_batch) // tq + i, q_off // qw)

    in_specs = [pl.BlockSpec((tq, qw), q_map)]
    args = [q]
    for k_arr, k_off, v_arr, v_off, length, row0 in segs:
        kw, vw = n_groups * dk, n_groups * dv
        in_specs.append(pl.BlockSpec((length, kw), functools.partial(
            lambda b, i, length, row0, blk: (row0 // length + b, blk), length=length, row0=row0, blk=k_off // kw)))
        in_specs.append(pl.BlockSpec((length, vw), functools.partial(
            lambda b, i, length, row0, blk: (row0 // length + b, blk), length=length, row0=row0, blk=v_off // vw)))
        args += [k_arr, v_arr]
    return pl.pallas_call(
        functools.partial(_attn_kernel, n_seg=len(segs), n_groups=n_groups, rep=rep, dk=dk, dv=dv, scale=scale),
        grid=(bsz, nq),
        in_specs=in_specs,
        out_specs=pl.BlockSpec((tq, ow), lambda b, i: (b * nq + i, 0)),
        out_shape=jax.ShapeDtypeStruct((bsz * n_q_per_batch, ow), BF16),
        compiler_params=_params("arbitrary", "arbitrary"),
        name=name,
    )(*args)


def _na_kernel(q_ref, k_ref, v_ref, kc_ref, vc_ref, bias_ref, o_ref):
    j = pl.program_id(0)
    k_rows = NA_K_ROWS * GRID_W
    start = jnp.clip(NA_Q_ROWS * j - NA_WIN_H // 2, 0, SEQ // GRID_W - NA_K_ROWS) * GRID_W
    start = pl.multiple_of(start, 256)
    nt = (((1,), (1,)), ((), ()))
    q = q_ref[...]
    s1 = lax.dot_general(q, k_ref[pl.ds(start, k_rows), :], nt, preferred_element_type=F32) * NA_SCALE
    s1 = s1 + bias_ref[0, 0]
    s2 = lax.dot_general(q, kc_ref[...], nt, preferred_element_type=F32) * NA_SCALE
    m = jnp.maximum(jnp.max(s1, axis=-1, keepdims=True), jnp.max(s2, axis=-1, keepdims=True))
    e1 = jnp.exp(s1 - m)
    e2 = jnp.exp(s2 - m)
    den = jnp.sum(e1, axis=-1, keepdims=True) + jnp.sum(e2, axis=-1, keepdims=True)
    acc = jnp.dot(e1.astype(BF16), v_ref[pl.ds(start, k_rows), :], preferred_element_type=F32)
    acc = acc + jnp.dot(e2.astype(BF16), vc_ref[...], preferred_element_type=F32)
    o_ref[...] = (acc / den).astype(o_ref.dtype)


def _na_bias_kernel(tp_ref, o_ref):
    t = pl.program_id(0)
    rows = SEQ // GRID_W
    k0 = jnp.clip(NA_Q_ROWS * t - NA_WIN_H // 2, 0, rows - NA_K_ROWS)
    left = lax.broadcasted_iota(jnp.int32, (GRID_W, 2 * GRID_W), 1) < GRID_W
    neg = -jnp.inf
    for a in range(NA_Q_ROWS):
        rq = NA_Q_ROWS * t + a
        r0 = jnp.clip(rq - NA_WIN_H // 2, 0, rows - NA_WIN_H)
        for b2 in range(NA_K_ROWS // 2):
            rk = k0 + 2 * b2
            ok_l = (rk >= r0) & (rk < r0 + NA_WIN_H)
            ok_r = (rk + 1 >= r0) & (rk + 1 < r0 + NA_WIN_H)
            pair = tp_ref[0, jnp.clip(rk - rq + NA_WIN_H, 0, 2 * NA_WIN_H - 1)]
            keep = jnp.where(left, ok_l.astype(jnp.int32), ok_r.astype(jnp.int32)) > 0
            o_ref[0, 0, a * GRID_W:(a + 1) * GRID_W, b2 * 2 * GRID_W:(b2 + 1) * 2 * GRID_W] = jnp.where(keep, pair, neg)


def _na_bias(rpb):
    n_dr = 2 * NA_WIN_H - 1
    cols = jnp.arange(GRID_W)
    c0 = jnp.clip(cols - NA_WIN_W // 2, 0, GRID_W - NA_WIN_W)
    col_ok = (cols[None, :] >= c0[:, None]) & (cols[None, :] < c0[:, None] + NA_WIN_W)
    dc = jnp.clip(cols[None, :] - cols[:, None] + NA_WIN_W - 1, 0, 2 * NA_WIN_W - 2)
    oh_c = (dc[:, :, None] == jnp.arange(2 * NA_WIN_W - 1)).astype(F32)
    by_col = jnp.einsum('hdc,qkc->hdqk', rpb.astype(F32), oh_c, precision=lax.Precision.HIGHEST)
    by_col = jnp.where(col_ok[None, None], by_col, -jnp.inf)
    zero = jnp.zeros((NA_HEADS, 1, GRID_W, GRID_W), F32)
    padded = jnp.concatenate([zero, by_col, zero], axis=1)
    pairs = jnp.concatenate([padded[:, :n_dr + 1], padded[:, 1:]], axis=-1)
    n_tiles = SEQ // GRID_W // NA_Q_ROWS
    tq, tk = NA_Q_ROWS * GRID_W, NA_K_ROWS * GRID_W
    return pl.pallas_call(
        _na_bias_kernel,
        grid=(n_tiles, NA_HEADS),
        in_specs=[pl.BlockSpec((1, n_dr + 1, GRID_W, 2 * GRID_W), lambda t, h: (h, 0, 0, 0))],
        out_specs=pl.BlockSpec((1, 1, tq, tk), lambda t, h: (t, h, 0, 0)),
        out_shape=jax.ShapeDtypeStruct((n_tiles, NA_HEADS, tq, tk), F32),
        compiler_params=_params("arbitrary", "arbitrary"),
        name="na_bias",
    )(pairs)


def _na_attention(p, bias, bsz, n_lat):
    tq = NA_Q_ROWS * GRID_W
    n_tiles = SEQ // tq
    hd = NA_HEAD_DIM
    qb, kb, vb = P_C // hd, (P_C + 512) // hd, (P_C + 1024) // hd
    return pl.pallas_call(
        _na_kernel,
        grid=(n_tiles, NA_HEADS, bsz),
        in_specs=[
            pl.BlockSpec((tq, hd), lambda j, h, b: (b * n_tiles + j, qb + h)),
            pl.BlockSpec((SEQ, hd), lambda j, h, b: (b, kb + h)),
            pl.BlockSpec((SEQ, hd), lambda j, h, b: (b, vb + h)),
            pl.BlockSpec((CTX_LEN, hd), lambda j, h, b: (n_lat // CTX_LEN + b, kb + h)),
            pl.BlockSpec((CTX_LEN, hd), lambda j, h, b: (n_lat // CTX_LEN + b, vb + h)),
            pl.BlockSpec((1, 1, tq, NA_K_ROWS * GRID_W), lambda j, h, b: (j, h, 0, 0)),
        ],
        out_specs=pl.BlockSpec((tq, hd), lambda j, h, b: (b * n_tiles + j, h)),
        out_shape=jax.ShapeDtypeStruct((n_lat, NA_HEADS * hd), BF16),
        compiler_params=_params("arbitrary", "arbitrary", "arbitrary"),
        name="na_attention",
    )(p, p, p, p, p, bias)


def _conv_kernel(b_ref, c_ref, u_ref, w_ref, o_ref):
    length = c_ref.shape[0]
    z = c_ref[...].astype(F32) * u_ref[...].astype(F32)
    row = lax.broadcasted_iota(jnp.int32, z.shape, 0)
    z_prev = jnp.where(row == 0, 0.0, pltpu.roll(z, 1, axis=0))
    z_next = jnp.where(row == length - 1, 0.0, pltpu.roll(z, length - 1, axis=0))
    conv = z_prev * w_ref[0:1, :] + z * w_ref[1:2, :] + z_next * w_ref[2:3, :]
    o_ref[...] = (b_ref[...].astype(F32) * conv).astype(o_ref.dtype)


def _short_conv(p, w_conv8, bsz, length, row0):
    blk = P_D // CONV_CH

    def rows(b):
        return row0 // length + b

    return pl.pallas_call(
        _conv_kernel,
        grid=(bsz,),
        in_specs=[
            pl.BlockSpec((length, CONV_CH), lambda b: (rows(b), blk)),
            pl.BlockSpec((length, CONV_CH), lambda b: (rows(b), blk + 1)),
            pl.BlockSpec((length, CONV_CH), lambda b: (rows(b), blk + 2)),
            pl.BlockSpec((8, CONV_CH), lambda b: (0, 0)),
        ],
        out_specs=pl.BlockSpec((length, CONV_CH), lambda b: (b, 0)),
        out_shape=jax.ShapeDtypeStruct((bsz * length, CONV_CH), BF16),
        compiler_params=_params("arbitrary"),
        name="short_conv",
    )(p, p, p, w_conv8)


def _merge_kernel(ya_ref, yb_ref, yc_ref, yd_ref, g0_ref, g1_ref, g2_ref, g3_ref, wb_ref, o_ref):
    acc = None
    for k, (y_ref, g_ref) in enumerate(((ya_ref, g0_ref), (yb_ref, g1_ref), (yc_ref, g2_ref), (yd_ref, g3_ref))):
        proj = jnp.dot(y_ref[...], wb_ref[0, k], preferred_element_type=F32)
        term = jax.nn.sigmoid(g_ref[...].astype(F32)) * proj
        acc = term if acc is None else acc + term
    o_ref[...] = acc.astype(o_ref.dtype)


def _merge(ys, p, w_branch, layer, n_act):
    tm = _pick(n_act, (1024, 512, 256))
    tn = 512
    g_specs = [pl.BlockSpec((tm, tn), functools.partial(
        lambda j, i, k: (i, (P_G + k * D_MODEL) // tn + j), k=k)) for k in range(N_BRANCHES)]
    return pl.pallas_call(
        _merge_kernel,
        grid=(D_MODEL // tn, n_act // tm),
        in_specs=[pl.BlockSpec((tm, BRANCH_WIDTH), lambda j, i: (i, 0))] * N_BRANCHES + g_specs + [
            pl.BlockSpec((1, N_BRANCHES, BRANCH_WIDTH, tn), lambda j, i: (layer, 0, 0, j))],
        out_specs=pl.BlockSpec((tm, tn), lambda j, i: (i, j)),
        out_shape=jax.ShapeDtypeStruct((n_act, D_MODEL), BF16),
        compiler_params=_params("arbitrary", "arbitrary"),
        name="branch_merge",
    )(*ys, p, p, p, p, w_branch)


R_E, R_W, R_RANK = 0, 2, 4


def _router_kernel(x_ref, g_ref, sh_ref, sc_ref, wr_ref, br_ref, h_ref, route_ref, counts_ref, run_ref,
                   *, tm, bsz):
    step = pl.program_id(0)
    r = _mod_row(step, tm, bsz)
    h = _rms(x_ref[...], g_ref[...]) * (1.0 + sc_ref[pl.ds(r, 1), :]) + sh_ref[pl.ds(r, 1), :]
    h_ref[...] = h
    logits = jnp.dot(h, wr_ref[...], precision=lax.Precision.HIGHEST, preferred_element_type=F32)
    bias = br_ref[...]
    lane = lax.broadcasted_iota(jnp.int32, logits.shape, 1)
    lane_f = lane.astype(F32)
    neg = -jnp.inf

    def rmax(x, mask):
        return jnp.max(jnp.where(mask, x, neg), axis=-1, keepdims=True)

    def first_at(x, val, mask):
        return jnp.min(jnp.where(mask & (x == val), lane_f, 2.0 * LANES), axis=-1, keepdims=True)

    def pick(x, idx):
        return jnp.sum(jnp.where(lane_f == idx, x, 0.0), axis=-1, keepdims=True)

    gmask = lane < N_GROUPS
    ge = jnp.where(gmask, jnp.exp(logits - rmax(logits, gmask)), 0.0)
    g_prob = ge / jnp.sum(ge, axis=-1, keepdims=True)
    g_score = g_prob + bias
    g_sel = first_at(g_score, rmax(g_score, gmask), gmask)
    p_group = pick(g_prob, g_sel)

    group_of_lane = lax.shift_right_arithmetic(lane - N_EXPERTS, 3).astype(F32)
    emask = (lane >= N_EXPERTS) & (group_of_lane == g_sel)
    ee = jnp.where(emask, jnp.exp(logits - rmax(logits, emask)), 0.0)
    e_prob = ee / jnp.sum(ee, axis=-1, keepdims=True)
    e_score = e_prob + bias
    i1 = first_at(e_score, rmax(e_score, emask), emask)
    mask2 = emask & (lane_f != i1)
    i2 = first_at(e_score, rmax(e_score, mask2), mask2)
    p1 = pick(e_prob, i1)
    p2 = pick(e_prob, i2)
    w1 = p_group * p1 / (p1 + p2)
    w2 = p_group * p2 / (p1 + p2)
    e1 = i1 - N_EXPERTS
    e2 = i2 - N_EXPERTS

    @pl.when(step == 0)
    def _():
        run_ref[...] = jnp.zeros_like(run_ref)

    oh1 = lane_f == e1
    oh2 = lane_f == e2
    both = oh1.astype(F32) + oh2.astype(F32)
    tri = (lax.broadcasted_iota(jnp.int32, (tm, tm), 1) < lax.broadcasted_iota(jnp.int32, (tm, tm), 0)).astype(BF16)
    before = jnp.dot(tri, both.astype(BF16), preferred_element_type=F32) + run_ref[0:1, :]
    rank1 = jnp.sum(jnp.where(oh1, before, 0.0), axis=-1, keepdims=True)
    rank2 = jnp.sum(jnp.where(oh2, before + oh1.astype(F32), 0.0), axis=-1, keepdims=True)
    run_ref[...] = run_ref[...] + jnp.sum(both, axis=0, keepdims=True)
    counts_ref[...] = run_ref[...]

    vals = (e1, e2, w1, w2, rank1, rank2)
    out = jnp.zeros_like(logits)
    for k, v in enumerate(vals):
        out = jnp.where(lane == k, v, out)
    route_ref[...] = out


def _ffn_input_and_route(x1, gain, mod, w_route, b_route, bsz):
    n = x1.shape[0]
    tm = _pick(n, (512, 256))
    rows = mod.shape[0]
    return pl.pallas_call(
        functools.partial(_router_kernel, tm=tm, bsz=bsz),
        grid=(n // tm,),
        in_specs=[
            pl.BlockSpec((tm, D_MODEL), lambda i: (i, 0)),
            pl.BlockSpec((1, D_MODEL), lambda i: (0, 0)),
            pl.BlockSpec((rows, D_MODEL), lambda i: (0, 3)),
            pl.BlockSpec((rows, D_MODEL), lambda i: (0, 4)),
            pl.BlockSpec((D_MODEL, LANES), lambda i: (0, 0)),
            pl.BlockSpec((1, LANES), lambda i: (0, 0)),
        ],
        out_specs=[pl.BlockSpec((tm, D_MODEL), lambda i: (i, 0)),
                   pl.BlockSpec((tm, LANES), lambda i: (i, 0)),
                   pl.BlockSpec((8, LANES), lambda i: (0, 0))],
        out_shape=[jax.ShapeDtypeStruct((n, D_MODEL), F32),
                   jax.ShapeDtypeStruct((n, LANES), F32),
                   jax.ShapeDtypeStruct((8, LANES), F32)],
        scratch_shapes=[pltpu.VMEM((8, LANES), F32)],
        compiler_params=_params("arbitrary"),
        name="ffn_input_route",
    )(x1, gain.reshape(1, D_MODEL), mod, mod, w_route, b_route)


def _dispatch_indices(route, counts_row, n_tok):
    counts = counts_row[0, :N_EXPERTS].astype(jnp.int32)
    padded = (counts + MOE_BLOCK - 1) // MOE_BLOCK * MOE_BLOCK
    pad_end = jnp.cumsum(padded)
    pad_start = pad_end - padded
    eid = route[:, R_E:R_E + TOP_K].astype(jnp.int32)
    rank = route[:, R_RANK:R_RANK + TOP_K].astype(jnp.int32)
    onehot = eid[:, :, None] == jnp.arange(N_EXPERTS, dtype=jnp.int32)[None, None, :]
    pos = (jnp.sum(jnp.where(onehot, pad_start[None, None, :], 0), axis=-1) + rank).reshape(-1)
    n_blocks = (n_tok * TOP_K + N_EXPERTS * (MOE_BLOCK - 1)) // MOE_BLOCK
    blk_start = jnp.arange(n_blocks, dtype=jnp.int32) * MOE_BLOCK
    blk_e = jnp.minimum(jnp.sum(pad_end[None, :] <= blk_start[:, None], axis=1), N_EXPERTS - 1).astype(jnp.int32)
    n_used = (pad_end[-1:] // MOE_BLOCK).astype(jnp.int32)
    return blk_e, n_used, pos.astype(jnp.int32), n_blocks


def _row_copy(src, src_row, dst, dst_row, sem):
    return pltpu.make_async_copy(src.at[pl.ds(src_row, 1), :], dst.at[pl.ds(dst_row, 1), :], sem)


DISPATCH_CHUNK = 512
DMA_UNROLL = 8


def _dispatch_kernel(pos_ref, h_ref, xs_in_hbm, xs_hbm, sem):
    del xs_in_hbm
    base = pl.program_id(0) * DISPATCH_CHUNK

    def body(r, carry):
        for k in range(TOP_K):
            _row_copy(h_ref, r, xs_hbm, pos_ref[(base + r) * TOP_K + k], sem.at[0]).start()
        return carry

    lax.fori_loop(0, DISPATCH_CHUNK, body, 0, unroll=DMA_UNROLL)
    for k in range(TOP_K):
        pltpu.make_async_copy(h_ref, xs_hbm.at[pl.ds(0, DISPATCH_CHUNK), :], sem.at[0]).wait()


def _dispatch(h2, pos, n_rows):
    n, width = h2.shape
    assert n % DISPATCH_CHUNK == 0
    grid_spec = pltpu.PrefetchScalarGridSpec(
        num_scalar_prefetch=1,
        grid=(n // DISPATCH_CHUNK,),
        in_specs=[pl.BlockSpec((DISPATCH_CHUNK, width), lambda i, pos: (i, 0)), pl.BlockSpec(memory_space=pl.ANY)],
        out_specs=pl.BlockSpec(memory_space=pl.ANY),
        scratch_shapes=[pltpu.SemaphoreType.DMA((1,))],
    )
    return pl.pallas_call(
        _dispatch_kernel,
        grid_spec=grid_spec,
        out_shape=jax.ShapeDtypeStruct((n_rows, width), h2.dtype),
        input_output_aliases={2: 0},
        compiler_params=_params("arbitrary"),
        name="moe_dispatch",
    )(pos, h2, jnp.zeros((n_rows, width), h2.dtype))


def _moe_kernel(be_ref, nused_ref, x_ref, wg_ref, wu_ref, wd_ref, o_ref, wgb, wub, wdb):
    i = pl.program_id(0)

    @pl.when((i == 0) | (be_ref[i] != be_ref[jnp.maximum(i - 1, 0)]))
    def _():
        wgb[...] = wg_ref[0, 0].astype(BF16)
        wub[...] = wu_ref[0, 0].astype(BF16)
        wdb[...] = wd_ref[0, 0].astype(BF16)

    @pl.when(i < nused_ref[0])
    def _():
        x = x_ref[...].astype(BF16)
        gate = jnp.dot(x, wgb[...], preferred_element_type=F32)
        up = jnp.dot(x, wub[...], preferred_element_type=F32)
        hid = (gate * jax.nn.sigmoid(gate) * up).astype(BF16)
        o_ref[...] = jnp.dot(hid, wdb[...], preferred_element_type=F32)

    @pl.when(i >= nused_ref[0])
    def _():
        o_ref[...] = jnp.zeros_like(o_ref)


def _moe(xs, blk_e, n_used, w_gate, w_up, w_down, layer):
    n_blocks = blk_e.shape[0]
    grid_spec = pltpu.PrefetchScalarGridSpec(
        num_scalar_prefetch=2,
        grid=(n_blocks,),
        in_specs=[
            pl.BlockSpec((MOE_BLOCK, D_MODEL), lambda i, be, nu: (i, 0)),
            pl.BlockSpec((1, 1, D_MODEL, EXPERT_HIDDEN), lambda i, be, nu: (layer, be[i], 0, 0)),
            pl.BlockSpec((1, 1, D_MODEL, EXPERT_HIDDEN), lambda i, be, nu: (layer, be[i], 0, 0)),
            pl.BlockSpec((1, 1, EXPERT_HIDDEN, D_MODEL), lambda i, be, nu: (layer, be[i], 0, 0)),
        ],
        out_specs=pl.BlockSpec((MOE_BLOCK, D_MODEL), lambda i, be, nu: (i, 0)),
        scratch_shapes=[
            pltpu.VMEM((D_MODEL, EXPERT_HIDDEN), BF16),
            pltpu.VMEM((D_MODEL, EXPERT_HIDDEN), BF16),
            pltpu.VMEM((EXPERT_HIDDEN, D_MODEL), BF16),
        ],
    )
    return pl.pallas_call(
        _moe_kernel,
        grid_spec=grid_spec,
        out_shape=jax.ShapeDtypeStruct((n_blocks * MOE_BLOCK, D_MODEL), F32),
        compiler_params=_params("arbitrary"),
        name="moe_experts",
    )(blk_e, n_used, xs, w_gate, w_up, w_down)


def _combine_kernel(pos_ref, ys_hbm, x_ref, route_ref, ga_ref, gf_ref, o_ref, rbuf, sem, *, tm, bsz, final):
    i = pl.program_id(0)
    n = pl.num_programs(0)
    slot = i % 2

    def gather(blk, s):
        def body(r, carry):
            for k in range(TOP_K):
                _row_copy(ys_hbm, pos_ref[(blk * tm + r) * TOP_K + k], rbuf.at[s], k * tm + r, sem.at[s]).start()
            return carry
        lax.fori_loop(0, tm, body, 0, unroll=DMA_UNROLL)

    @pl.when(i == 0)
    def _():
        gather(0, 0)

    @pl.when(i + 1 < n)
    def _():
        gather(i + 1, 1 - slot)

    pltpu.make_async_copy(ys_hbm.at[pl.ds(0, TOP_K * tm), :], rbuf.at[slot], sem.at[slot]).wait()
    y = rbuf[slot, 0:tm, :] * route_ref[:, R_W:R_W + 1] + rbuf[slot, tm:2 * tm, :] * route_ref[:, R_W + 1:R_W + 2]
    r = _mod_row(i, tm, bsz)
    x = x_ref[...] + ga_ref[pl.ds(r, 1), :] * y
    if final:
        x = _rms(x, gf_ref[...])
    o_ref[...] = x


def _combine(ys, pos, x1, route, mod, g_final, bsz, final):
    n = x1.shape[0]
    tm = 128
    rows = mod.shape[0]
    grid_spec = pltpu.PrefetchScalarGridSpec(
        num_scalar_prefetch=1,
        grid=(n // tm,),
        in_specs=[
            pl.BlockSpec(memory_space=pl.ANY),
            pl.BlockSpec((tm, D_MODEL), lambda i, pos: (i, 0)),
            pl.BlockSpec((tm, LANES), lambda i, pos: (i, 0)),
            pl.BlockSpec((rows, D_MODEL), lambda i, pos: (0, 5)),
            pl.BlockSpec((1, D_MODEL), lambda i, pos: (0, 0)),
        ],
        out_specs=pl.BlockSpec((tm, D_MODEL), lambda i, pos: (i, 0)),
        scratch_shapes=[pltpu.VMEM((2, TOP_K * tm, D_MODEL), F32), pltpu.SemaphoreType.DMA((2,))],
    )
    return pl.pallas_call(
        functools.partial(_combine_kernel, tm=tm, bsz=bsz, final=final),
        grid_spec=grid_spec,
        out_shape=jax.ShapeDtypeStruct((n, D_MODEL), F32),
        compiler_params=_params("arbitrary"),
        name="moe_combine",
    )(pos, ys, x1, route, mod, g_final.reshape(1, D_MODEL))


def _rotate_half_cols(w, quarter):
    r1, r2, c1, c2 = (w[..., k * quarter:(k + 1) * quarter] for k in range(4))
    return jnp.concatenate([-r2, r1, -c2, c1], axis=-1)


def _kpe_rows(wt):
    kpe = jnp.swapaxes(wt[:, MLA_Q_LORA + MLA_KV_LORA:A_COLS, :], 1, 2)
    both = jnp.concatenate([kpe, _rotate_half_cols(kpe, MLA_ROPE // 4)], axis=-1)
    return jnp.swapaxes(both, 1, 2).astype(BF16)


def _relayout_w_uq(w_uq):
    w = w_uq.reshape(MLA_Q_LORA, MLA_HEADS, MLA_NOPE + MLA_ROPE)
    nope, pe = w[..., :MLA_NOPE], w[..., MLA_NOPE:]
    z = jnp.zeros_like(pe)
    out = jnp.concatenate([nope, pe, z, _rotate_half_cols(pe, MLA_ROPE // 4), z], axis=-1)
    return out.reshape(MLA_Q_LORA, MLA_HEADS * 3 * LANES).astype(BF16)


def _relayout_w_ukv(w_ukv):
    w = w_ukv.reshape(MLA_KV_LORA, MLA_HEADS, MLA_NOPE + MLA_V)
    return jnp.concatenate([w[..., :MLA_NOPE].reshape(MLA_KV_LORA, -1),
                            w[..., MLA_NOPE:].reshape(MLA_KV_LORA, -1)], axis=-1).astype(BF16)


def _rope_tables(tm):
    t = jnp.arange(SEQ)
    rows = (t // GRID_W).astype(F32)
    cols = (t % GRID_W).astype(F32)

    def cs(rot_dim):
        quarter = rot_dim // 4
        inv_freq = ROPE_THETA ** (-jnp.arange(quarter, dtype=F32) / quarter)
        ang_r = rows[:, None] * inv_freq[None, :]
        ang_c = cols[:, None] * inv_freq[None, :]
        cos = jnp.concatenate([jnp.cos(ang_r)] * 2 + [jnp.cos(ang_c)] * 2, axis=-1)
        sin = jnp.concatenate([jnp.sin(ang_r)] * 2 + [jnp.sin(ang_c)] * 2, axis=-1)
        return cos, sin

    def with_identity(tab, ident):
        return jnp.concatenate([tab, jnp.broadcast_to(ident, (tm, LANES))], axis=0)

    cos_a, sin_a = cs(MLA_ROPE)
    zeros = jnp.zeros((SEQ, LANES - MLA_ROPE), F32)
    lane = jnp.arange(LANES)
    one_lo = (lane < MLA_ROPE).astype(F32)[None, :]
    cosa = with_identity(jnp.concatenate([cos_a, zeros], axis=1), one_lo)
    sina = with_identity(jnp.concatenate([sin_a, zeros], axis=1), jnp.zeros((1, LANES), F32))
    csa = with_identity(jnp.concatenate([cos_a, sin_a], axis=1), one_lo)
    cos_b, sin_b = cs(GQA_HEAD_DIM)
    sign = jnp.where((lane % (GQA_HEAD_DIM // 2)) < GQA_HEAD_DIM // 4, -1.0, 1.0).astype(F32)[None, :]
    cosb = with_identity(cos_b, jnp.ones((1, LANES), F32))
    sinb = with_identity(sin_b * sign, jnp.zeros((1, LANES), F32))
    return cosa, sina, csa, cosb, sinb


def kernel(x, c, ctx, c_ctx, w_mod, b_mod, g_mix, g_ffn, w_in, w_uq, g_qa, w_ukv, g_kva, g_qn, g_kn, rpb, w_conv,
           w_branch, w_o, w_group, b_group, w_router, b_router, w_gate_e, w_up_e, w_down_e, g_final):
    bsz = x.shape[0]
    n_lat = bsz * SEQ
    n_ctx = bsz * CTX_LEN
    n_all = n_lat + n_ctx
    mod_rows = -(-(bsz + 1) // 8) * 8

    x_all = jnp.concatenate([x.reshape(n_lat, D_MODEL), ctx.reshape(n_ctx, D_MODEL)], axis=0)
    c_all = jnp.concatenate([c, c_ctx[None, :], jnp.zeros((mod_rows - bsz - 1, D_MODEL), F32)], axis=0)
    mods = _modulation(c_all, w_mod, b_mod)
    tables = _rope_tables(_pick(n_all, (512, 256)))

    w_in_t = jnp.swapaxes(w_in, 1, 2)
    w_kpe_t = _kpe_rows(w_in_t)
    w_branch_b = w_branch.astype(BF16)
    w_o_b = w_o.astype(BF16)

    out = None
    for l in range(DEPTH):
        last = l == DEPTH - 1
        n_act = n_lat if last else n_all
        mod = mods[l]
        w_uq_x = _relayout_w_uq(w_uq[l])
        w_ukv_r = _relayout_w_ukv(w_ukv[l])
        w_route = jnp.concatenate([w_group[l], jnp.zeros((D_MODEL, N_EXPERTS - N_GROUPS), F32), w_router[l]], axis=1)
        b_route = jnp.concatenate([b_group[l], jnp.zeros((N_EXPERTS - N_GROUPS,), F32), b_router[l]])[None, :]
        w_conv8 = jnp.concatenate([w_conv[l], jnp.zeros((8 - CONV_WIDTH, CONV_CH), F32)], axis=0)

        h = _hmod(x_all, g_mix[l], mod, bsz)
        p = _in_proj(h, w_in_t, l)
        pk = _matmul_nt(h, w_kpe_t, l, BF16, "kpe_proj")
        qa, ka, va, qb, kb = _prep(p, pk, tables, w_uq_x, w_ukv_r, g_qa[l][None, :], g_kva[l][None, :],
                                   g_qn[l][None, :], g_kn[l][None, :])

        lat_ctx = lambda k_arr, k_off, v_arr, v_off: [(k_arr, k_off, v_arr, v_off, SEQ, 0),
                                                      (k_arr, k_off, v_arr, v_off, CTX_LEN, n_lat)]
        ctx_only = lambda k_arr, k_off, v_arr, v_off: [(k_arr, k_off, v_arr, v_off, CTX_LEN, n_lat)]
        mla = dict(n_groups=MLA_HEADS, rep=1, dk=MLA_DK_PAD, dv=MLA_V, scale=1.0)
        gqa = dict(n_groups=GQA_KV_HEADS, rep=GQA_HEADS // GQA_KV_HEADS, dk=GQA_HEAD_DIM, dv=GQA_HEAD_DIM, scale=1.0)
        nac = dict(n_groups=NA_HEADS, rep=1, dk=NA_HEAD_DIM, dv=NA_HEAD_DIM, scale=NA_SCALE)

        ys_lat = [
            _attention(qa, 0, lat_ctx(ka, 0, va, 0), bsz, SEQ, 0, name="mla_attention", **mla),
            _attention(qb, 0, lat_ctx(kb, 0, p, P_B + 768), bsz, SEQ, 0, name="gqa_attention", **gqa),
            _na_attention(p, _na_bias(rpb[l]), bsz, n_lat),
            _short_conv(p, w_conv8, bsz, SEQ, 0),
        ]
        if last:
            ys_all = ys_lat
        else:
            ys_ctx = [
                _attention(qa, 0, ctx_only(ka, 0, va, 0), bsz, CTX_LEN, n_lat, name="mla_attention_ctx", **mla),
                _attention(qb, 0, ctx_only(kb, 0, p, P_B + 768), bsz, CTX_LEN, n_lat,
                           name="gqa_attention_ctx", **gqa),
                _attention(p, P_C, ctx_only(p, P_C + 512, p, P_C + 1024), bsz, CTX_LEN, n_lat,
                           name="na_attention_ctx", **nac),
                _short_conv(p, w_conv8, bsz, CTX_LEN, n_lat),
            ]
            ys_all = [jnp.concatenate([a, b], axis=0) for a, b in zip(ys_lat, ys_ctx)]

        mixed = _merge(ys_all, p, w_branch_b, l, n_act)
        x1 = _out_proj_residual(mixed, w_o_b, l, x_all, mod, bsz)
        h2, route, counts = _ffn_input_and_route(x1, g_ffn[l], mod, w_route, b_route, bsz)
        blk_e, n_used, pos, n_blocks = _dispatch_indices(route, counts, n_act)
        xs = _dispatch(h2, pos, n_blocks * MOE_BLOCK)
        ys = _moe(xs, blk_e, n_used, w_gate_e, w_up_e, w_down_e, l)
        x_all = _combine(ys, pos, x1, route, mod, g_final, bsz, last)
        out = x_all
    return out.reshape(bsz, SEQ, D_MODEL)
```

```python
import functools

import jax
import jax.numpy as jnp
from jax import lax
from jax.experimental import pallas as pl
from jax.experimental.pallas import tpu as pltpu

F32 = jnp.float32
BF16 = jnp.bfloat16

D_MODEL = 2048
SEQ = 2048
DEPTH = 2
CTX_LEN = 256
GRID_W = 64
ROPE_THETA = 10000.0
NORM_EPS = 1e-6
MOD_CHUNKS = 6

MLA_HEADS = 4
MLA_Q_LORA = 512
MLA_KV_LORA = 512
MLA_NOPE = 128
MLA_ROPE = 64
MLA_V = 128
MLA_SCALE = (MLA_NOPE + MLA_ROPE) ** -0.5
MLA_DK_PAD = 256

GQA_HEADS = 4
GQA_KV_HEADS = 2
GQA_HEAD_DIM = 128
GQA_SCALE = GQA_HEAD_DIM ** -0.5

NA_HEADS = 4
NA_HEAD_DIM = 128
NA_WIN_H = 8
NA_WIN_W = 16
NA_SCALE = NA_HEAD_DIM ** -0.5
NA_Q_ROWS = 8
NA_K_ROWS = 16

CONV_CH = 512
CONV_WIDTH = 3
BRANCH_WIDTH = 512
N_BRANCHES = 4

A_COLS = MLA_Q_LORA + MLA_KV_LORA + MLA_ROPE
B_COLS = (GQA_HEADS + 2 * GQA_KV_HEADS) * GQA_HEAD_DIM
C_COLS = 3 * NA_HEADS * NA_HEAD_DIM
D_COLS = 3 * CONV_CH
G_COLS = N_BRANCHES * D_MODEL

P_B = 0
P_C = P_B + B_COLS
P_D = P_C + C_COLS
P_G = P_D + D_COLS
P_CQ = P_G + G_COLS
P_CKV = P_CQ + MLA_Q_LORA
P_COLS = P_CKV + MLA_KV_LORA
IN_TILE = 1024

N_GROUPS = 8
EXPERTS_PER_GROUP = 8
N_EXPERTS = 64
TOP_K = 2
EXPERT_HIDDEN = 512
MOE_BLOCK = 128

LANES = 128
VMEM_LIMIT = 56 * 1024 * 1024


def _params(*sem):
    return pltpu.CompilerParams(dimension_semantics=sem, vmem_limit_bytes=VMEM_LIMIT)


def _pick(n, cands):
    for c in cands:
        if n % c == 0:
            return c
    raise ValueError(f"no tile for {n}")


def _rms(x, g):
    return x * lax.rsqrt(jnp.mean(x * x, axis=-1, keepdims=True) + NORM_EPS) * g


def _mod_row(i, tm, bsz):
    return jnp.minimum(i * tm // SEQ, bsz)


def _mod_kernel(c_ref, w_ref, b_ref, o_ref):
    c = c_ref[...]
    s = c * jax.nn.sigmoid(c)
    o_ref[0] = jnp.dot(s, w_ref[0], precision=lax.Precision.HIGHEST,
                       preferred_element_type=F32) + b_ref[0]


def _modulation(c_all, w_mod, b_mod):
    rows = c_all.shape[0]
    n = MOD_CHUNKS * D_MODEL
    tn = 1024
    return pl.pallas_call(
        _mod_kernel,
        grid=(DEPTH, n // tn),
        in_specs=[
            pl.BlockSpec((rows, D_MODEL), lambda l, j: (0, 0)),
            pl.BlockSpec((1, D_MODEL, tn), lambda l, j: (l, 0, j)),
            pl.BlockSpec((1, 1, tn), lambda l, j: (l, 0, j)),
        ],
        out_specs=pl.BlockSpec((1, rows, tn), lambda l, j: (l, 0, j)),
        out_shape=jax.ShapeDtypeStruct((DEPTH, rows, n), F32),
        compiler_params=_params("arbitrary", "arbitrary"),
        name="modulation",
    )(c_all, w_mod, b_mod.reshape(DEPTH, 1, n))


def _hmod_kernel(x_ref, g_ref, sh_ref, sc_ref, o_ref, *, tm, bsz):
    r = _mod_row(pl.program_id(0), tm, bsz)
    y = _rms(x_ref[...], g_ref[...])
    o_ref[...] = (y * (1.0 + sc_ref[pl.ds(r, 1), :]) + sh_ref[pl.ds(r, 1), :]).astype(o_ref.dtype)


def _hmod(x_all, gain, mod, bsz):
    n = x_all.shape[0]
    tm = _pick(n, (512, 256))
    rows = mod.shape[0]
    return pl.pallas_call(
        functools.partial(_hmod_kernel, tm=tm, bsz=bsz),
        grid=(n // tm,),
        in_specs=[
            pl.BlockSpec((tm, D_MODEL), lambda i: (i, 0)),
            pl.BlockSpec((1, D_MODEL), lambda i: (0, 0)),
            pl.BlockSpec((rows, D_MODEL), lambda i: (0, 0)),
            pl.BlockSpec((rows, D_MODEL), lambda i: (0, 1)),
        ],
        out_specs=pl.BlockSpec((tm, D_MODEL), lambda i: (i, 0)),
        out_shape=jax.ShapeDtypeStruct((n, D_MODEL), BF16),
        compiler_params=_params("arbitrary"),
        name="hmod",
    )(x_all, gain.reshape(1, D_MODEL), mod, mod)


NT_DIMS = (((1,), (1,)), ((), ()))


def _mm_nt_kernel(a_ref, w_ref, o_ref):
    o_ref[...] = lax.dot_general(a_ref[...], w_ref[0], NT_DIMS, preferred_element_type=F32).astype(o_ref.dtype)


def _matmul_nt(a, wt, layer, out_dtype, name):
    m, k = a.shape
    n = wt.shape[1]
    tm = _pick(m, (1024, 512, 256))
    return pl.pallas_call(
        _mm_nt_kernel,
        grid=(m // tm,),
        in_specs=[pl.BlockSpec((tm, k), lambda i: (i, 0)), pl.BlockSpec((1, n, k), lambda i: (layer, 0, 0))],
        out_specs=pl.BlockSpec((tm, n), lambda i: (i, 0)),
        out_shape=jax.ShapeDtypeStruct((m, n), out_dtype),
        compiler_params=_params("arbitrary"),
        name=name,
    )(a, wt)


def _in_proj_kernel(a_ref, wt_hbm, o_ref, wbuf, wbf, sem, *, layer):
    j = pl.program_id(0)
    n_tiles = pl.num_programs(0)
    n_shift = P_CQ // IN_TILE

    def w_copy(jj, slot):
        row = pl.multiple_of(jnp.where(jj < n_shift, A_COLS + jj * IN_TILE, 0), MLA_ROPE)
        return pltpu.make_async_copy(wt_hbm.at[layer, pl.ds(row, IN_TILE), :], wbuf.at[slot], sem.at[slot])

    @pl.when(pl.program_id(1) == 0)
    def _():
        @pl.when(j == 0)
        def _():
            w_copy(0, 0).start()

        @pl.when(j + 1 < n_tiles)
        def _():
            w_copy(j + 1, (j + 1) % 2).start()

        w_copy(j, j % 2).wait()
        wbf[...] = wbuf[j % 2].astype(BF16)

    o_ref[...] = lax.dot_general(a_ref[...], wbf[...], NT_DIMS, preferred_element_type=F32).astype(o_ref.dtype)


def _in_proj(h, wt, layer):
    m, k = h.shape
    tm = _pick(m, (1024, 512, 256))
    return pl.pallas_call(
        functools.partial(_in_proj_kernel, layer=layer),
        grid=(P_COLS // IN_TILE, m // tm),
        in_specs=[pl.BlockSpec((tm, k), lambda j, i: (i, 0)), pl.BlockSpec(memory_space=pl.ANY)],
        out_specs=pl.BlockSpec((tm, IN_TILE), lambda j, i: (i, j)),
        out_shape=jax.ShapeDtypeStruct((m, P_COLS), BF16),
        scratch_shapes=[pltpu.VMEM((2, IN_TILE, k), F32), pltpu.VMEM((IN_TILE, k), BF16),
                        pltpu.SemaphoreType.DMA((2,))],
        compiler_params=_params("arbitrary", "arbitrary"),
        name="in_proj",
    )(h, wt)


def _mm_res_kernel(a_ref, w_ref, x_ref, ga_ref, o_ref, *, tm, bsz):
    r = _mod_row(pl.program_id(1), tm, bsz)
    acc = jnp.dot(a_ref[...], w_ref[0], preferred_element_type=F32)
    o_ref[...] = x_ref[...] + ga_ref[pl.ds(r, 1), :] * acc


def _out_proj_residual(mixed, w_o, layer, x_all, mod, bsz):
    m = mixed.shape[0]
    tm = _pick(m, (1024, 512, 256))
    tn = 512
    rows = mod.shape[0]
    ga_blk = 2 * D_MODEL // tn
    return pl.pallas_call(
        functools.partial(_mm_res_kernel, tm=tm, bsz=bsz),
        grid=(D_MODEL // tn, m // tm),
        in_specs=[
            pl.BlockSpec((tm, D_MODEL), lambda j, i: (i, 0)),
            pl.BlockSpec((1, D_MODEL, tn), lambda j, i: (layer, 0, j)),
            pl.BlockSpec((tm, tn), lambda j, i: (i, j)),
            pl.BlockSpec((rows, tn), lambda j, i: (0, ga_blk + j)),
        ],
        out_specs=pl.BlockSpec((tm, tn), lambda j, i: (i, j)),
        out_shape=jax.ShapeDtypeStruct((m, D_MODEL), F32),
        compiler_params=_params("arbitrary", "arbitrary"),
        name="out_proj_residual",
    )(mixed, w_o, x_all, mod)


def _prep_kernel(cq_ref, ckv_ref, kpe_ref, bq_ref, bk_ref,
                 cosa_ref, sina_ref, csa_ref, cosb_ref, sinb_ref,
                 wuq_ref, wukv_ref, gqa_ref, gkva_ref, gqn_ref, gkn_ref,
                 qa_ref, ka_ref, va_ref, qb_ref, kb_ref):
    tm = cq_ref.shape[0]
    lane = lax.broadcasted_iota(jnp.int32, (tm, LANES), 1)

    nq = _rms(cq_ref[...].astype(F32), gqa_ref[...]).astype(BF16)
    qf = jnp.dot(nq, wuq_ref[...], preferred_element_type=F32)
    cosa = cosa_ref[...]
    sina = sina_ref[...]
    for h in range(MLA_HEADS):
        base = h * 3 * LANES
        nope = qf[:, base:base + LANES]
        roped = qf[:, base + LANES:base + 2 * LANES] * cosa + qf[:, base + 2 * LANES:base + 3 * LANES] * sina
        qa_ref[:, h * MLA_DK_PAD:h * MLA_DK_PAD + LANES] = (nope * MLA_SCALE).astype(BF16)
        qa_ref[:, h * MLA_DK_PAD + LANES:(h + 1) * MLA_DK_PAD] = (roped * MLA_SCALE).astype(BF16)

    nkv = _rms(ckv_ref[...].astype(F32), gkva_ref[...]).astype(BF16)
    kvf = jnp.dot(nkv, wukv_ref[...], preferred_element_type=F32)
    t = kpe_ref[...].astype(F32) * csa_ref[...]
    kpe = jnp.where(lane < MLA_ROPE, t + pltpu.roll(t, MLA_ROPE, axis=1), 0.0).astype(BF16)
    for h in range(MLA_HEADS):
        ka_ref[:, h * MLA_DK_PAD:h * MLA_DK_PAD + LANES] = kvf[:, h * LANES:(h + 1) * LANES].astype(BF16)
        ka_ref[:, h * MLA_DK_PAD + LANES:(h + 1) * MLA_DK_PAD] = kpe
    va_ref[...] = kvf[:, MLA_HEADS * MLA_NOPE:].astype(BF16)

    cosb = cosb_ref[...]
    sinb = sinb_ref[...]
    first = (lane % (GQA_HEAD_DIM // 2)) < (GQA_HEAD_DIM // 4)

    def rope_b(x):
        rot = jnp.where(first, pltpu.roll(x, LANES - GQA_HEAD_DIM // 4, axis=1),
                        pltpu.roll(x, GQA_HEAD_DIM // 4, axis=1))
        return x * cosb + rot * sinb

    gqn = gqn_ref[...]
    gkn = gkn_ref[...]
    for h in range(GQA_HEADS):
        x = bq_ref[:, h * LANES:(h + 1) * LANES].astype(F32)
        qb_ref[:, h * LANES:(h + 1) * LANES] = (rope_b(_rms(x, gqn)) * GQA_SCALE).astype(BF16)
    for h in range(GQA_KV_HEADS):
        x = bk_ref[:, h * LANES:(h + 1) * LANES].astype(F32)
        kb_ref[:, h * LANES:(h + 1) * LANES] = rope_b(_rms(x, gkn)).astype(BF16)


def _prep(p, pk, tables, w_uq_x, w_ukv_r, g_qa, g_kva, g_qn, g_kn):
    n = p.shape[0]
    tm = _pick(n, (512, 256))
    n_pos = SEQ // tm
    n_lat_tiles = (n // (SEQ + CTX_LEN)) * n_pos

    def tab(i):
        return (jnp.where(i < n_lat_tiles, i % n_pos, n_pos), 0)

    def col(width, off):
        return pl.BlockSpec((tm, width), lambda i: (i, off // width))

    def whole(a):
        return pl.BlockSpec(a.shape, lambda i: (0,) * a.ndim)

    tab_spec = pl.BlockSpec((tm, LANES), tab)
    cosa, sina, csa, cosb, sinb = tables
    outs = [(MLA_HEADS * MLA_DK_PAD, "qa"), (MLA_HEADS * MLA_DK_PAD, "ka"), (MLA_HEADS * MLA_V, "va"),
            (GQA_HEADS * GQA_HEAD_DIM, "qb"), (GQA_KV_HEADS * GQA_HEAD_DIM, "kb")]
    return pl.pallas_call(
        _prep_kernel,
        grid=(n // tm,),
        in_specs=[col(512, P_CQ), col(512, P_CKV), col(LANES, 0), col(512, P_B), col(256, P_B + 512),
                  tab_spec, tab_spec, tab_spec, tab_spec, tab_spec,
                  whole(w_uq_x), whole(w_ukv_r), whole(g_qa), whole(g_kva), whole(g_qn), whole(g_kn)],
        out_specs=[pl.BlockSpec((tm, w), lambda i: (i, 0)) for w, _ in outs],
        out_shape=[jax.ShapeDtypeStruct((n, w), BF16) for w, _ in outs],
        compiler_params=_params("arbitrary"),
        name="attn_prep",
    )(p, p, pk, p, p, cosa, sina, csa, cosb, sinb, w_uq_x, w_ukv_r, g_qa, g_kva, g_qn, g_kn)


def _attn_kernel(*refs, n_seg, n_groups, rep, dk, dv, scale):
    q_ref = refs[0]
    segs = [(refs[1 + 2 * s], refs[2 + 2 * s]) for s in range(n_seg)]
    o_ref = refs[1 + 2 * n_seg]
    nt = (((1,), (1,)), ((), ()))
    for g in range(n_groups):
        ks = [k_ref[:, g * dk:(g + 1) * dk] for k_ref, _ in segs]
        vs = []
        for _, v_ref in segs:
            one = (lax.broadcasted_iota(jnp.int32, (v_ref.shape[0], LANES), 1) == 0).astype(BF16)
            vs.append(jnp.concatenate([v_ref[:, g * dv:(g + 1) * dv], one], axis=1))
        for r in range(rep):
            hq = g * rep + r
            q = q_ref[:, hq * dk:(hq + 1) * dk]
            ss = [lax.dot_general(q, k, nt, preferred_element_type=F32) for k in ks]
            if scale != 1.0:
                ss = [s * scale for s in ss]
            m = functools.reduce(jnp.maximum, [jnp.max(s, axis=-1, keepdims=True) for s in ss])
            acc = None
            for s, v in zip(ss, vs):
                a = jnp.dot(jnp.exp(s - m).astype(BF16), v, preferred_element_type=F32)
                acc = a if acc is None else acc + a
            o_ref[:, hq * dv:(hq + 1) * dv] = (acc[:, :dv] / acc[:, dv:dv + 1]).astype(o_ref.dtype)


def _attention(q, q_off, segs, bsz, n_q_per_batch, q_row0, *, n_groups, rep, dk, dv, scale, name):
    tq = min(512, n_q_per_batch)
    nq = n_q_per_batch // tq
    qw = n_groups * rep * dk
    ow = n_groups * rep * dv

    def q_map(b, i):
        return ((q_row0 + b * n_q_per_batch) // tq + i, q_off // qw)

    in_specs = [pl.BlockSpec((tq, qw), q_map)]
    args = [q]
    for k_arr, k_off, v_arr, v_off, length, row0 in segs:
        kw, vw = n_groups * dk, n_groups * dv
        in_specs.append(pl.BlockSpec((length, kw), functools.partial(
            lambda b, i, length, row0, blk: (row0 // length + b, blk), length=length, row0=row0, blk=k_off // kw)))
        in_specs.append(pl.BlockSpec((length, vw), functools.partial(
            lambda b, i, length, row0, blk: (row0 // length + b, blk), length=length, row0=row0, blk=v_off // vw)))
        args += [k_arr, v_arr]
    return pl.pallas_call(
        functools.partial(_attn_kernel, n_seg=len(segs), n_groups=n_groups, rep=rep, dk=dk, dv=dv, scale=scale),
        grid=(bsz, nq),
        in_specs=in_specs,
        out_specs=pl.BlockSpec((tq, ow), lambda b, i: (b * nq + i, 0)),
        out_shape=jax.ShapeDtypeStruct((bsz * n_q_per_batch, ow), BF16),
        compiler_params=_params("arbitrary", "arbitrary"),
        name=name,
    )(*args)


def _na_kernel(q_ref, k_ref, v_ref, kc_ref, vc_ref, bias_ref, o_ref):
    j = pl.program_id(0)
    k_rows = NA_K_ROWS * GRID_W
    start = jnp.clip(NA_Q_ROWS * j - NA_WIN_H // 2, 0, SEQ // GRID_W - NA_K_ROWS) * GRID_W
    start = pl.multiple_of(start, 256)
    nt = (((1,), (1,)), ((), ()))
    q = q_ref[...]
    s1 = lax.dot_general(q, k_ref[pl.ds(start, k_rows), :], nt, preferred_element_type=F32) * NA_SCALE
    s1 = s1 + bias_ref[0, 0]
    s2 = lax.dot_general(q, kc_ref[...], nt, preferred_element_type=F32) * NA_SCALE
    m = jnp.maximum(jnp.max(s1, axis=-1, keepdims=True), jnp.max(s2, axis=-1, keepdims=True))
    e1 = jnp.exp(s1 - m)
    e2 = jnp.exp(s2 - m)
    den = jnp.sum(e1, axis=-1, keepdims=True) + jnp.sum(e2, axis=-1, keepdims=True)
    acc = jnp.dot(e1.astype(BF16), v_ref[pl.ds(start, k_rows), :], preferred_element_type=F32)
    acc = acc + jnp.dot(e2.astype(BF16), vc_ref[...], preferred_element_type=F32)
    o_ref[...] = (acc / den).astype(o_ref.dtype)


def _na_bias_kernel(tp_ref, o_ref):
    t = pl.program_id(0)
    rows = SEQ // GRID_W
    k0 = jnp.clip(NA_Q_ROWS * t - NA_WIN_H // 2, 0, rows - NA_K_ROWS)
    left = lax.broadcasted_iota(jnp.int32, (GRID_W, 2 * GRID_W), 1) < GRID_W
    neg = -jnp.inf
    for a in range(NA_Q_ROWS):
        rq = NA_Q_ROWS * t + a
        r0 = jnp.clip(rq - NA_WIN_H // 2, 0, rows - NA_WIN_H)
        for b2 in range(NA_K_ROWS // 2):
            rk = k0 + 2 * b2
            ok_l = (rk >= r0) & (rk < r0 + NA_WIN_H)
            ok_r = (rk + 1 >= r0) & (rk + 1 < r0 + NA_WIN_H)
            pair = tp_ref[0, jnp.clip(rk - rq + NA_WIN_H, 0, 2 * NA_WIN_H - 1)]
            keep = jnp.where(left, ok_l.astype(jnp.int32), ok_r.astype(jnp.int32)) > 0
            o_ref[0, 0, a * GRID_W:(a + 1) * GRID_W, b2 * 2 * GRID_W:(b2 + 1) * 2 * GRID_W] = jnp.where(keep, pair, neg)


def _na_bias(rpb):
    n_dr = 2 * NA_WIN_H - 1
    cols = jnp.arange(GRID_W)
    c0 = jnp.clip(cols - NA_WIN_W // 2, 0, GRID_W - NA_WIN_W)
    col_ok = (cols[None, :] >= c0[:, None]) & (cols[None, :] < c0[:, None] + NA_WIN_W)
    dc = jnp.clip(cols[None, :] - cols[:, None] + NA_WIN_W - 1, 0, 2 * NA_WIN_W - 2)
    oh_c = (dc[:, :, None] == jnp.arange(2 * NA_WIN_W - 1)).astype(F32)
    by_col = jnp.einsum('hdc,qkc->hdqk', rpb.astype(F32), oh_c, precision=lax.Precision.HIGHEST)
    by_col = jnp.where(col_ok[None, None], by_col, -jnp.inf)
    zero = jnp.zeros((NA_HEADS, 1, GRID_W, GRID_W), F32)
    padded = jnp.concatenate([zero, by_col, zero], axis=1)
    pairs = jnp.concatenate([padded[:, :n_dr + 1], padded[:, 1:]], axis=-1)
    n_tiles = SEQ // GRID_W // NA_Q_ROWS
    tq, tk = NA_Q_ROWS * GRID_W, NA_K_ROWS * GRID_W
    return pl.pallas_call(
        _na_bias_kernel,
        grid=(n_tiles, NA_HEADS),
        in_specs=[pl.BlockSpec((1, n_dr + 1, GRID_W, 2 * GRID_W), lambda t, h: (h, 0, 0, 0))],
        out_specs=pl.BlockSpec((1, 1, tq, tk), lambda t, h: (t, h, 0, 0)),
        out_shape=jax.ShapeDtypeStruct((n_tiles, NA_HEADS, tq, tk), F32),
        compiler_params=_params("arbitrary", "arbitrary"),
        name="na_bias",
    )(pairs)


def _na_attention(p, bias, bsz, n_lat):
    tq = NA_Q_ROWS * GRID_W
    n_tiles = SEQ // tq
    hd = NA_HEAD_DIM
    qb, kb, vb = P_C // hd, (P_C + 512) // hd, (P_C + 1024) // hd
    return pl.pallas_call(
        _na_kernel,
        grid=(n_tiles, NA_HEADS, bsz),
        in_specs=[
            pl.BlockSpec((tq, hd), lambda j, h, b: (b * n_tiles + j, qb + h)),
            pl.BlockSpec((SEQ, hd), lambda j, h, b: (b, kb + h)),
            pl.BlockSpec((SEQ, hd), lambda j, h, b: (b, vb + h)),
            pl.BlockSpec((CTX_LEN, hd), lambda j, h, b: (n_lat // CTX_LEN + b, kb + h)),
            pl.BlockSpec((CTX_LEN, hd), lambda j, h, b: (n_lat // CTX_LEN + b, vb + h)),
            pl.BlockSpec((1, 1, tq, NA_K_ROWS * GRID_W), lambda j, h, b: (j, h, 0, 0)),
        ],
        out_specs=pl.BlockSpec((tq, hd), lambda j, h, b: (b * n_tiles + j, h)),
        out_shape=jax.ShapeDtypeStruct((n_lat, NA_HEADS * hd), BF16),
        compiler_params=_params("arbitrary", "arbitrary", "arbitrary"),
        name="na_attention",
    )(p, p, p, p, p, bias)


def _conv_kernel(b_ref, c_ref, u_ref, w_ref, o_ref):
    length = c_ref.shape[0]
    z = c_ref[...].astype(F32) * u_ref[...].astype(F32)
    row = lax.broadcasted_iota(jnp.int32, z.shape, 0)
    z_prev = jnp.where(row == 0, 0.0, pltpu.roll(z, 1, axis=0))
    z_next = jnp.where(row == length - 1, 0.0, pltpu.roll(z, length - 1, axis=0))
    conv = z_prev * w_ref[0:1, :] + z * w_ref[1:2, :] + z_next * w_ref[2:3, :]
    o_ref[...] = (b_ref[...].astype(F32) * conv).astype(o_ref.dtype)


def _short_conv(p, w_conv8, bsz, length, row0):
    blk = P_D // CONV_CH

    def rows(b):
        return row0 // length + b

    return pl.pallas_call(
        _conv_kernel,
        grid=(bsz,),
        in_specs=[
            pl.BlockSpec((length, CONV_CH), lambda b: (rows(b), blk)),
            pl.BlockSpec((length, CONV_CH), lambda b: (rows(b), blk + 1)),
            pl.BlockSpec((length, CONV_CH), lambda b: (rows(b), blk + 2)),
            pl.BlockSpec((8, CONV_CH), lambda b: (0, 0)),
        ],
        out_specs=pl.BlockSpec((length, CONV_CH), lambda b: (b, 0)),
        out_shape=jax.ShapeDtypeStruct((bsz * length, CONV_CH), BF16),
        compiler_params=_params("arbitrary"),
        name="short_conv",
    )(p, p, p, w_conv8)


def _merge_kernel(ya_ref, yb_ref, yc_ref, yd_ref, g0_ref, g1_ref, g2_ref, g3_ref, wb_ref, o_ref):
    acc = None
    for k, (y_ref, g_ref) in enumerate(((ya_ref, g0_ref), (yb_ref, g1_ref), (yc_ref, g2_ref), (yd_ref, g3_ref))):
        proj = jnp.dot(y_ref[...], wb_ref[0, k], preferred_element_type=F32)
        term = jax.nn.sigmoid(g_ref[...].astype(F32)) * proj
        acc = term if acc is None else acc + term
    o_ref[...] = acc.astype(o_ref.dtype)


def _merge(ys, p, w_branch, layer, n_act):
    tm = _pick(n_act, (1024, 512, 256))
    tn = 512
    g_specs = [pl.BlockSpec((tm, tn), functools.partial(
        lambda j, i, k: (i, (P_G + k * D_MODEL) // tn + j), k=k)) for k in range(N_BRANCHES)]
    return pl.pallas_call(
        _merge_kernel,
        grid=(D_MODEL // tn, n_act // tm),
        in_specs=[pl.BlockSpec((tm, BRANCH_WIDTH), lambda j, i: (i, 0))] * N_BRANCHES + g_specs + [
            pl.BlockSpec((1, N_BRANCHES, BRANCH_WIDTH, tn), lambda j, i: (layer, 0, 0, j))],
        out_specs=pl.BlockSpec((tm, tn), lambda j, i: (i, j)),
        out_shape=jax.ShapeDtypeStruct((n_act, D_MODEL), BF16),
        compiler_params=_params("arbitrary", "arbitrary"),
        name="branch_merge",
    )(*ys, p, p, p, p, w_branch)


R_E, R_W, R_RANK = 0, 2, 4


def _router_kernel(x_ref, g_ref, sh_ref, sc_ref, wr_ref, br_ref, h_ref, route_ref, counts_ref, run_ref,
                   *, tm, bsz):
    step = pl.program_id(0)
    r = _mod_row(step, tm, bsz)
    h = _rms(x_ref[...], g_ref[...]) * (1.0 + sc_ref[pl.ds(r, 1), :]) + sh_ref[pl.ds(r, 1), :]
    h_ref[...] = h
    logits = jnp.dot(h, wr_ref[...], precision=lax.Precision.HIGHEST, preferred_element_type=F32)
    bias = br_ref[...]
    lane = lax.broadcasted_iota(jnp.int32, logits.shape, 1)
    lane_f = lane.astype(F32)
    neg = -jnp.inf

    def rmax(x, mask):
        return jnp.max(jnp.where(mask, x, neg), axis=-1, keepdims=True)

    def first_at(x, val, mask):
        return jnp.min(jnp.where(mask & (x == val), lane_f, 2.0 * LANES), axis=-1, keepdims=True)

    def pick(x, idx):
        return jnp.sum(jnp.where(lane_f == idx, x, 0.0), axis=-1, keepdims=True)

    gmask = lane < N_GROUPS
    ge = jnp.where(gmask, jnp.exp(logits - rmax(logits, gmask)), 0.0)
    g_prob = ge / jnp.sum(ge, axis=-1, keepdims=True)
    g_score = g_prob + bias
    g_sel = first_at(g_score, rmax(g_score, gmask), gmask)
    p_group = pick(g_prob, g_sel)

    group_of_lane = lax.shift_right_arithmetic(lane - N_EXPERTS, 3).astype(F32)
    emask = (lane >= N_EXPERTS) & (group_of_lane == g_sel)
    ee = jnp.where(emask, jnp.exp(logits - rmax(logits, emask)), 0.0)
    e_prob = ee / jnp.sum(ee, axis=-1, keepdims=True)
    e_score = e_prob + bias
    i1 = first_at(e_score, rmax(e_score, emask), emask)
    mask2 = emask & (lane_f != i1)
    i2 = first_at(e_score, rmax(e_score, mask2), mask2)
    p1 = pick(e_prob, i1)
    p2 = pick(e_prob, i2)
    w1 = p_group * p1 / (p1 + p2)
    w2 = p_group * p2 / (p1 + p2)
    e1 = i1 - N_EXPERTS
    e2 = i2 - N_EXPERTS

    @pl.when(step == 0)
    def _():
        run_ref[...] = jnp.zeros_like(run_ref)

    oh1 = lane_f == e1
    oh2 = lane_f == e2
    both = oh1.astype(F32) + oh2.astype(F32)
    tri = (lax.broadcasted_iota(jnp.int32, (tm, tm), 1) < lax.broadcasted_iota(jnp.int32, (tm, tm), 0)).astype(BF16)
    before = jnp.dot(tri, both.astype(BF16), preferred_element_type=F32) + run_ref[0:1, :]
    rank1 = jnp.sum(jnp.where(oh1, before, 0.0), axis=-1, keepdims=True)
    rank2 = jnp.sum(jnp.where(oh2, before + oh1.astype(F32), 0.0), axis=-1, keepdims=True)
    run_ref[...] = run_ref[...] + jnp.sum(both, axis=0, keepdims=True)
    counts_ref[...] = run_ref[...]

    vals = (e1, e2, w1, w2, rank1, rank2)
    out = jnp.zeros_like(logits)
    for k, v in enumerate(vals):
        out = jnp.where(lane == k, v, out)
    route_ref[...] = out


def _ffn_input_and_route(x1, gain, mod, w_route, b_route, bsz):
    n = x1.shape[0]
    tm = _pick(n, (512, 256))
    rows = mod.shape[0]
    return pl.pallas_call(
        functools.partial(_router_kernel, tm=tm, bsz=bsz),
        grid=(n // tm,),
        in_specs=[
            pl.BlockSpec((tm, D_MODEL), lambda i: (i, 0)),
            pl.BlockSpec((1, D_MODEL), lambda i: (0, 0)),
            pl.BlockSpec((rows, D_MODEL), lambda i: (0, 3)),
            pl.BlockSpec((rows, D_MODEL), lambda i: (0, 4)),
            pl.BlockSpec((D_MODEL, LANES), lambda i: (0, 0)),
            pl.BlockSpec((1, LANES), lambda i: (0, 0)),
        ],
        out_specs=[pl.BlockSpec((tm, D_MODEL), lambda i: (i, 0)),
                   pl.BlockSpec((tm, LANES), lambda i: (i, 0)),
                   pl.BlockSpec((8, LANES), lambda i: (0, 0))],
        out_shape=[jax.ShapeDtypeStruct((n, D_MODEL), F32),
                   jax.ShapeDtypeStruct((n, LANES), F32),
                   jax.ShapeDtypeStruct((8, LANES), F32)],
        scratch_shapes=[pltpu.VMEM((8, LANES), F32)],
        compiler_params=_params("arbitrary"),
        name="ffn_input_route",
    )(x1, gain.reshape(1, D_MODEL), mod, mod, w_route, b_route)


def _dispatch_indices(route, counts_row, n_tok):
    counts = counts_row[0, :N_EXPERTS].astype(jnp.int32)
    padded = (counts + MOE_BLOCK - 1) // MOE_BLOCK * MOE_BLOCK
    pad_end = jnp.cumsum(padded)
    pad_start = pad_end - padded
    eid = route[:, R_E:R_E + TOP_K].astype(jnp.int32)
    rank = route[:, R_RANK:R_RANK + TOP_K].astype(jnp.int32)
    onehot = eid[:, :, None] == jnp.arange(N_EXPERTS, dtype=jnp.int32)[None, None, :]
    pos = (jnp.sum(jnp.where(onehot, pad_start[None, None, :], 0), axis=-1) + rank).reshape(-1)
    n_blocks = (n_tok * TOP_K + N_EXPERTS * (MOE_BLOCK - 1)) // MOE_BLOCK
    blk_start = jnp.arange(n_blocks, dtype=jnp.int32) * MOE_BLOCK
    blk_e = jnp.minimum(jnp.sum(pad_end[None, :] <= blk_start[:, None], axis=1), N_EXPERTS - 1).astype(jnp.int32)
    n_used = (pad_end[-1:] // MOE_BLOCK).astype(jnp.int32)
    last_blk = jnp.where(counts > 0, pad_end // MOE_BLOCK - 1, -1).astype(jnp.int32)
    blk = jnp.arange(n_blocks, dtype=jnp.int32)
    prev_e = jnp.concatenate([jnp.full((1,), -1, jnp.int32), blk_e[:-1]])
    first = (blk < n_used) & (blk_e != prev_e)
    slot = (jnp.cumsum(first.astype(jnp.int32)) - 1) % 2
    later_first = jnp.where(first[None, :] & (blk[None, :] > blk[:, None]), blk[None, :], n_blocks)
    nxt_blk = jnp.min(later_first, axis=1)
    nxt_e = jnp.sum(jnp.where(blk[None, :] == nxt_blk[:, None], blk_e[None, :], 0), axis=1)
    nxt_e = jnp.where(nxt_blk < n_blocks, nxt_e, -1)
    sched = (blk_e, n_used, first.astype(jnp.int32), slot.astype(jnp.int32), nxt_e.astype(jnp.int32))
    return sched, last_blk, pos.astype(jnp.int32), n_blocks


def _row_copy(src, src_row, dst, dst_row, sem):
    return pltpu.make_async_copy(src.at[pl.ds(src_row, 1), :], dst.at[pl.ds(dst_row, 1), :], sem)


DISPATCH_CHUNK = 512
DMA_UNROLL = 8


def _dispatch_kernel(pos_ref, last_ref, nused_ref, h_ref, xs_hbm, zbuf, sem, zsem):
    base = pl.program_id(0) * DISPATCH_CHUNK
    n_blocks = xs_hbm.shape[0] // MOE_BLOCK

    @pl.when(pl.program_id(0) == 0)
    def _():
        zbuf[...] = jnp.zeros_like(zbuf)

        def fill(b):
            return pltpu.make_async_copy(zbuf, xs_hbm.at[pl.ds(pl.multiple_of(b * MOE_BLOCK, MOE_BLOCK), MOE_BLOCK), :],
                                         zsem.at[0])

        def each(fn):
            def expert(e, carry):
                @pl.when(last_ref[e] >= 0)
                def _():
                    fn(fill(last_ref[e]))
                return carry

            def unused(b, carry):
                fn(fill(b))
                return carry

            lax.fori_loop(0, N_EXPERTS, expert, 0)
            lax.fori_loop(nused_ref[0], n_blocks, unused, 0)

        each(lambda c: c.start())
        each(lambda c: c.wait())

    def body(r, carry):
        for k in range(TOP_K):
            _row_copy(h_ref, r, xs_hbm, pos_ref[(base + r) * TOP_K + k], sem.at[0]).start()
        return carry

    lax.fori_loop(0, DISPATCH_CHUNK, body, 0, unroll=DMA_UNROLL)
    for k in range(TOP_K):
        pltpu.make_async_copy(h_ref, xs_hbm.at[pl.ds(0, DISPATCH_CHUNK), :], sem.at[0]).wait()


def _dispatch(h2, pos, last_blk, n_used, n_rows):
    n, width = h2.shape
    assert n % DISPATCH_CHUNK == 0
    grid_spec = pltpu.PrefetchScalarGridSpec(
        num_scalar_prefetch=3,
        grid=(n // DISPATCH_CHUNK,),
        in_specs=[pl.BlockSpec((DISPATCH_CHUNK, width), lambda i, *_: (i, 0))],
        out_specs=pl.BlockSpec(memory_space=pl.ANY),
        scratch_shapes=[pltpu.VMEM((MOE_BLOCK, width), h2.dtype), pltpu.SemaphoreType.DMA((1,)),
                        pltpu.SemaphoreType.DMA((1,))],
    )
    return pl.pallas_call(
        _dispatch_kernel,
        grid_spec=grid_spec,
        out_shape=jax.ShapeDtypeStruct((n_rows, width), h2.dtype),
        compiler_params=_params("arbitrary"),
        name="moe_dispatch",
    )(pos, last_blk, n_used, h2)


def _moe_kernel(be_ref, nused_ref, first_ref, slot_ref, nxt_ref, x_ref, wg_hbm, wu_hbm, wd_hbm, o_ref,
                wgf, wuf, wdf, wgb, wub, wdb, sem, *, layer):
    i = pl.program_id(0)

    def w_copies(e, s):
        return (pltpu.make_async_copy(wg_hbm.at[layer, e], wgf.at[s], sem.at[s]),
                pltpu.make_async_copy(wu_hbm.at[layer, e], wuf.at[s], sem.at[s]),
                pltpu.make_async_copy(wd_hbm.at[layer, e], wdf.at[s], sem.at[s]))

    @pl.when(first_ref[i] == 1)
    def _():
        s = slot_ref[i]

        @pl.when(i == 0)
        def _():
            for c in w_copies(be_ref[i], s):
                c.start()

        @pl.when(nxt_ref[i] >= 0)
        def _():
            for c in w_copies(nxt_ref[i], 1 - s):
                c.start()

        for c in w_copies(be_ref[i], s):
            c.wait()
        wgb[...] = wgf[s].astype(BF16)
        wub[...] = wuf[s].astype(BF16)
        wdb[...] = wdf[s].astype(BF16)

    @pl.when(i < nused_ref[0])
    def _():
        x = x_ref[...].astype(BF16)
        gate = jnp.dot(x, wgb[...], preferred_element_type=F32)
        up = jnp.dot(x, wub[...], preferred_element_type=F32)
        hid = (gate * jax.nn.sigmoid(gate) * up).astype(BF16)
        o_ref[...] = jnp.dot(hid, wdb[...], preferred_element_type=F32)

    @pl.when(i >= nused_ref[0])
    def _():
        o_ref[...] = jnp.zeros_like(o_ref)


def _moe(xs, sched, w_gate, w_up, w_down, layer):
    n_blocks = sched[0].shape[0]
    grid_spec = pltpu.PrefetchScalarGridSpec(
        num_scalar_prefetch=len(sched),
        grid=(n_blocks,),
        in_specs=[pl.BlockSpec((MOE_BLOCK, D_MODEL), lambda i, *_: (i, 0))] + [pl.BlockSpec(memory_space=pl.ANY)] * 3,
        out_specs=pl.BlockSpec((MOE_BLOCK, D_MODEL), lambda i, *_: (i, 0)),
        scratch_shapes=[
            pltpu.VMEM((2, D_MODEL, EXPERT_HIDDEN), F32),
            pltpu.VMEM((2, D_MODEL, EXPERT_HIDDEN), F32),
            pltpu.VMEM((2, EXPERT_HIDDEN, D_MODEL), F32),
            pltpu.VMEM((D_MODEL, EXPERT_HIDDEN), BF16),
            pltpu.VMEM((D_MODEL, EXPERT_HIDDEN), BF16),
            pltpu.VMEM((EXPERT_HIDDEN, D_MODEL), BF16),
            pltpu.SemaphoreType.DMA((2,)),
        ],
    )
    return pl.pallas_call(
        functools.partial(_moe_kernel, layer=layer),
        grid_spec=grid_spec,
        out_shape=jax.ShapeDtypeStruct((n_blocks * MOE_BLOCK, D_MODEL), F32),
        compiler_params=_params("arbitrary"),
        name="moe_experts",
    )(*sched, xs, w_gate, w_up, w_down)


def _combine_kernel(pos_ref, ys_hbm, x_ref, route_ref, ga_ref, gf_ref, o_ref, rbuf, sem, *, tm, bsz, final):
    i = pl.program_id(0)
    n = pl.num_programs(0)
    slot = i % 2

    def gather(blk, s):
        def body(r, carry):
            for k in range(TOP_K):
                _row_copy(ys_hbm, pos_ref[(blk * tm + r) * TOP_K + k], rbuf.at[s], k * tm + r, sem.at[s]).start()
            return carry
        lax.fori_loop(0, tm, body, 0, unroll=DMA_UNROLL)

    @pl.when(i == 0)
    def _():
        gather(0, 0)

    @pl.when(i + 1 < n)
    def _():
        gather(i + 1, 1 - slot)

    pltpu.make_async_copy(ys_hbm.at[pl.ds(0, TOP_K * tm), :], rbuf.at[slot], sem.at[slot]).wait()
    y = rbuf[slot, 0:tm, :] * route_ref[:, R_W:R_W + 1] + rbuf[slot, tm:2 * tm, :] * route_ref[:, R_W + 1:R_W + 2]
    r = _mod_row(i, tm, bsz)
    x = x_ref[...] + ga_ref[pl.ds(r, 1), :] * y
    if final:
        x = _rms(x, gf_ref[...])
    o_ref[...] = x


def _combine(ys, pos, x1, route, mod, g_final, bsz, final):
    n = x1.shape[0]
    tm = 128
    rows = mod.shape[0]
    grid_spec = pltpu.PrefetchScalarGridSpec(
        num_scalar_prefetch=1,
        grid=(n // tm,),
        in_specs=[
            pl.BlockSpec(memory_space=pl.ANY),
            pl.BlockSpec((tm, D_MODEL), lambda i, pos: (i, 0)),
            pl.BlockSpec((tm, LANES), lambda i, pos: (i, 0)),
            pl.BlockSpec((rows, D_MODEL), lambda i, pos: (0, 5)),
            pl.BlockSpec((1, D_MODEL), lambda i, pos: (0, 0)),
        ],
        out_specs=pl.BlockSpec((tm, D_MODEL), lambda i, pos: (i, 0)),
        scratch_shapes=[pltpu.VMEM((2, TOP_K * tm, D_MODEL), F32), pltpu.SemaphoreType.DMA((2,))],
    )
    return pl.pallas_call(
        functools.partial(_combine_kernel, tm=tm, bsz=bsz, final=final),
        grid_spec=grid_spec,
        out_shape=jax.ShapeDtypeStruct((n, D_MODEL), F32),
        compiler_params=_params("arbitrary"),
        name="moe_combine",
    )(pos, ys, x1, route, mod, g_final.reshape(1, D_MODEL))


def _rotate_half_cols(w, quarter):
    r1, r2, c1, c2 = (w[..., k * quarter:(k + 1) * quarter] for k in range(4))
    return jnp.concatenate([-r2, r1, -c2, c1], axis=-1)


def _kpe_rows(wt):
    kpe = jnp.swapaxes(wt[:, MLA_Q_LORA + MLA_KV_LORA:A_COLS, :], 1, 2)
    both = jnp.concatenate([kpe, _rotate_half_cols(kpe, MLA_ROPE // 4)], axis=-1)
    return jnp.swapaxes(both, 1, 2).astype(BF16)


def _relayout_w_uq(w_uq):
    w = w_uq.reshape(MLA_Q_LORA, MLA_HEADS, MLA_NOPE + MLA_ROPE)
    nope, pe = w[..., :MLA_NOPE], w[..., MLA_NOPE:]
    z = jnp.zeros_like(pe)
    out = jnp.concatenate([nope, pe, z, _rotate_half_cols(pe, MLA_ROPE // 4), z], axis=-1)
    return out.reshape(MLA_Q_LORA, MLA_HEADS * 3 * LANES).astype(BF16)


def _relayout_w_ukv(w_ukv):
    w = w_ukv.reshape(MLA_KV_LORA, MLA_HEADS, MLA_NOPE + MLA_V)
    return jnp.concatenate([w[..., :MLA_NOPE].reshape(MLA_KV_LORA, -1),
                            w[..., MLA_NOPE:].reshape(MLA_KV_LORA, -1)], axis=-1).astype(BF16)


def _rope_tables(tm):
    t = jnp.arange(SEQ)
    rows = (t // GRID_W).astype(F32)
    cols = (t % GRID_W).astype(F32)

    def cs(rot_dim):
        quarter = rot_dim // 4
        inv_freq = ROPE_THETA ** (-jnp.arange(quarter, dtype=F32) / quarter)
        ang_r = rows[:, None] * inv_freq[None, :]
        ang_c = cols[:, None] * inv_freq[None, :]
        cos = jnp.concatenate([jnp.cos(ang_r)] * 2 + [jnp.cos(ang_c)] * 2, axis=-1)
        sin = jnp.concatenate([jnp.sin(ang_r)] * 2 + [jnp.sin(ang_c)] * 2, axis=-1)
        return cos, sin

    def with_identity(tab, ident):
        return jnp.concatenate([tab, jnp.broadcast_to(ident, (tm, LANES))], axis=0)

    cos_a, sin_a = cs(MLA_ROPE)
    zeros = jnp.zeros((SEQ, LANES - MLA_ROPE), F32)
    lane = jnp.arange(LANES)
    one_lo = (lane < MLA_ROPE).astype(F32)[None, :]
    cosa = with_identity(jnp.concatenate([cos_a, zeros], axis=1), one_lo)
    sina = with_identity(jnp.concatenate([sin_a, zeros], axis=1), jnp.zeros((1, LANES), F32))
    csa = with_identity(jnp.concatenate([cos_a, sin_a], axis=1), one_lo)
    cos_b, sin_b = cs(GQA_HEAD_DIM)
    sign = jnp.where((lane % (GQA_HEAD_DIM // 2)) < GQA_HEAD_DIM // 4, -1.0, 1.0).astype(F32)[None, :]
    cosb = with_identity(cos_b, jnp.ones((1, LANES), F32))
    sinb = with_identity(sin_b * sign, jnp.zeros((1, LANES), F32))
    return cosa, sina, csa, cosb, sinb


def kernel(x, c, ctx, c_ctx, w_mod, b_mod, g_mix, g_ffn, w_in, w_uq, g_qa, w_ukv, g_kva, g_qn, g_kn, rpb, w_conv,
           w_branch, w_o, w_group, b_group, w_router, b_router, w_gate_e, w_up_e, w_down_e, g_final):
    bsz = x.shape[0]
    n_lat = bsz * SEQ
    n_ctx = bsz * CTX_LEN
    n_all = n_lat + n_ctx
    mod_rows = -(-(bsz + 1) // 8) * 8

    x_all = jnp.concatenate([x.reshape(n_lat, D_MODEL), ctx.reshape(n_ctx, D_MODEL)], axis=0)
    c_all = jnp.concatenate([c, c_ctx[None, :], jnp.zeros((mod_rows - bsz - 1, D_MODEL), F32)], axis=0)
    mods = _modulation(c_all, w_mod, b_mod)
    tables = _rope_tables(_pick(n_all, (512, 256)))

    w_in_t = jnp.swapaxes(w_in, 1, 2)
    w_kpe_t = _kpe_rows(w_in_t)
    w_branch_b = w_branch.astype(BF16)
    w_o_b = w_o.astype(BF16)

    out = None
    for l in range(DEPTH):
        last = l == DEPTH - 1
        n_act = n_lat if last else n_all
        mod = mods[l]
        w_uq_x = _relayout_w_uq(w_uq[l])
        w_ukv_r = _relayout_w_ukv(w_ukv[l])
        w_route = jnp.concatenate([w_group[l], jnp.zeros((D_MODEL, N_EXPERTS - N_GROUPS), F32), w_router[l]], axis=1)
        b_route = jnp.concatenate([b_group[l], jnp.zeros((N_EXPERTS - N_GROUPS,), F32), b_router[l]])[None, :]
        w_conv8 = jnp.concatenate([w_conv[l], jnp.zeros((8 - CONV_WIDTH, CONV_CH), F32)], axis=0)

        h = _hmod(x_all, g_mix[l], mod, bsz)
        p = _in_proj(h, w_in_t, l)
        pk = _matmul_nt(h, w_kpe_t, l, BF16, "kpe_proj")
        qa, ka, va, qb, kb = _prep(p, pk, tables, w_uq_x, w_ukv_r, g_qa[l][None, :], g_kva[l][None, :],
                                   g_qn[l][None, :], g_kn[l][None, :])

        lat_ctx = lambda k_arr, k_off, v_arr, v_off: [(k_arr, k_off, v_arr, v_off, SEQ, 0),
                                                      (k_arr, k_off, v_arr, v_off, CTX_LEN, n_lat)]
        ctx_only = lambda k_arr, k_off, v_arr, v_off: [(k_arr, k_off, v_arr, v_off, CTX_LEN, n_lat)]
        mla = dict(n_groups=MLA_HEADS, rep=1, dk=MLA_DK_PAD, dv=MLA_V, scale=1.0)
        gqa = dict(n_groups=GQA_KV_HEADS, rep=GQA_HEADS // GQA_KV_HEADS, dk=GQA_HEAD_DIM, dv=GQA_HEAD_DIM, scale=1.0)
        nac = dict(n_groups=NA_HEADS, rep=1, dk=NA_HEAD_DIM, dv=NA_HEAD_DIM, scale=NA_SCALE)

        ys_lat = [
            _attention(qa, 0, lat_ctx(ka, 0, va, 0), bsz, SEQ, 0, name="mla_attention", **mla),
            _attention(qb, 0, lat_ctx(kb, 0, p, P_B + 768), bsz, SEQ, 0, name="gqa_attention", **gqa),
            _na_attention(p, _na_bias(rpb[l]), bsz, n_lat),
            _short_conv(p, w_conv8, bsz, SEQ, 0),
        ]
        if last:
            ys_all = ys_lat
        else:
            ys_ctx = [
                _attention(qa, 0, ctx_only(ka, 0, va, 0), bsz, CTX_LEN, n_lat, name="mla_attention_ctx", **mla),
                _attention(qb, 0, ctx_only(kb, 0, p, P_B + 768), bsz, CTX_LEN, n_lat,
                           name="gqa_attention_ctx", **gqa),
                _attention(p, P_C, ctx_only(p, P_C + 512, p, P_C + 1024), bsz, CTX_LEN, n_lat,
                           name="na_attention_ctx", **nac),
                _short_conv(p, w_conv8, bsz, CTX_LEN, n_lat),
            ]
            ys_all = [jnp.concatenate([a, b], axis=0) for a, b in zip(ys_lat, ys_ctx)]

        mixed = _merge(ys_all, p, w_branch_b, l, n_act)
        x1 = _out_proj_residual(mixed, w_o_b, l, x_all, mod, bsz)
        h2, route, counts = _ffn_input_and_route(x1, g_ffn[l], mod, w_route, b_route, bsz)
        sched, last_blk, pos, n_blocks = _dispatch_indices(route, counts, n_act)
        xs = _dispatch(h2, pos, last_blk, sched[1], n_blocks * MOE_BLOCK)
        ys = _moe(xs, sched, w_gate_e, w_up_e, w_down_e, l)
        x_all = _combine(ys, pos, x1, route, mod, g_final, bsz, last)
        out = x_all
    return out.reshape(bsz, SEQ, D_MODEL)
```

```python
import functools

import jax
import jax.numpy as jnp
from jax import lax
from jax.experimental import pallas as pl
from jax.experimental.pallas import tpu as pltpu

F32 = jnp.float32
BF16 = jnp.bfloat16

D_MODEL = 2048
SEQ = 2048
DEPTH = 2
CTX_LEN = 256
GRID_W = 64
ROPE_THETA = 10000.0
NORM_EPS = 1e-6
MOD_CHUNKS = 6

MLA_HEADS = 4
MLA_Q_LORA = 512
MLA_KV_LORA = 512
MLA_NOPE = 128
MLA_ROPE = 64
MLA_V = 128
MLA_SCALE = (MLA_NOPE + MLA_ROPE) ** -0.5
MLA_DK_PAD = 256

GQA_HEADS = 4
GQA_KV_HEADS = 2
GQA_HEAD_DIM = 128
GQA_SCALE = GQA_HEAD_DIM ** -0.5

NA_HEADS = 4
NA_HEAD_DIM = 128
NA_WIN_H = 8
NA_WIN_W = 16
NA_SCALE = NA_HEAD_DIM ** -0.5
NA_Q_ROWS = 8
NA_K_ROWS = 16

CONV_CH = 512
CONV_WIDTH = 3
BRANCH_WIDTH = 512
N_BRANCHES = 4

A_COLS = MLA_Q_LORA + MLA_KV_LORA + MLA_ROPE
B_COLS = (GQA_HEADS + 2 * GQA_KV_HEADS) * GQA_HEAD_DIM
C_COLS = 3 * NA_HEADS * NA_HEAD_DIM
D_COLS = 3 * CONV_CH
G_COLS = N_BRANCHES * D_MODEL

P_B = 0
P_C = P_B + B_COLS
P_D = P_C + C_COLS
P_G = P_D + D_COLS
P_CQ = P_G + G_COLS
P_CKV = P_CQ + MLA_Q_LORA
P_COLS = P_CKV + MLA_KV_LORA
IN_TILE = 1024

N_GROUPS = 8
EXPERTS_PER_GROUP = 8
N_EXPERTS = 64
TOP_K = 2
EXPERT_HIDDEN = 512
MOE_BLOCK = 128
MOE_W_SLOTS = 3

LANES = 128
VMEM_LIMIT = 56 * 1024 * 1024


def _params(*sem):
    return pltpu.CompilerParams(dimension_semantics=sem, vmem_limit_bytes=VMEM_LIMIT)


def _pick(n, cands):
    for c in cands:
        if n % c == 0:
            return c
    raise ValueError(f"no tile for {n}")


def _rms(x, g):
    return x * lax.rsqrt(jnp.mean(x * x, axis=-1, keepdims=True) + NORM_EPS) * g


def _mod_row(i, tm, bsz):
    return jnp.minimum(i * tm // SEQ, bsz)


def _mod_kernel(c_ref, w_ref, b_ref, o_ref):
    c = c_ref[...]
    s = c * jax.nn.sigmoid(c)
    o_ref[0] = jnp.dot(s, w_ref[0], precision=lax.Precision.HIGHEST,
                       preferred_element_type=F32) + b_ref[0]


def _modulation(c_all, w_mod, b_mod):
    rows = c_all.shape[0]
    n = MOD_CHUNKS * D_MODEL
    tn = 1024
    return pl.pallas_call(
        _mod_kernel,
        grid=(DEPTH, n // tn),
        in_specs=[
            pl.BlockSpec((rows, D_MODEL), lambda l, j: (0, 0)),
            pl.BlockSpec((1, D_MODEL, tn), lambda l, j: (l, 0, j)),
            pl.BlockSpec((1, 1, tn), lambda l, j: (l, 0, j)),
        ],
        out_specs=pl.BlockSpec((1, rows, tn), lambda l, j: (l, 0, j)),
        out_shape=jax.ShapeDtypeStruct((DEPTH, rows, n), F32),
        compiler_params=_params("arbitrary", "arbitrary"),
        name="modulation",
    )(c_all, w_mod, b_mod.reshape(DEPTH, 1, n))


def _hmod_kernel(x_ref, g_ref, sh_ref, sc_ref, o_ref, *, tm, bsz):
    r = _mod_row(pl.program_id(0), tm, bsz)
    y = _rms(x_ref[...], g_ref[...])
    o_ref[...] = (y * (1.0 + sc_ref[pl.ds(r, 1), :]) + sh_ref[pl.ds(r, 1), :]).astype(o_ref.dtype)


def _hmod(x_all, gain, mod, bsz):
    n = x_all.shape[0]
    tm = _pick(n, (512, 256))
    rows = mod.shape[0]
    return pl.pallas_call(
        functools.partial(_hmod_kernel, tm=tm, bsz=bsz),
        grid=(n // tm,),
        in_specs=[
            pl.BlockSpec((tm, D_MODEL), lambda i: (i, 0)),
            pl.BlockSpec((1, D_MODEL), lambda i: (0, 0)),
            pl.BlockSpec((rows, D_MODEL), lambda i: (0, 0)),
            pl.BlockSpec((rows, D_MODEL), lambda i: (0, 1)),
        ],
        out_specs=pl.BlockSpec((tm, D_MODEL), lambda i: (i, 0)),
        out_shape=jax.ShapeDtypeStruct((n, D_MODEL), BF16),
        compiler_params=_params("arbitrary"),
        name="hmod",
    )(x_all, gain.reshape(1, D_MODEL), mod, mod)


NT_DIMS = (((1,), (1,)), ((), ()))


def _mm_nt_kernel(a_ref, w_ref, o_ref):
    o_ref[...] = lax.dot_general(a_ref[...], w_ref[0], NT_DIMS, preferred_element_type=F32).astype(o_ref.dtype)


def _matmul_nt(a, wt, layer, out_dtype, name):
    m, k = a.shape
    n = wt.shape[1]
    tm = _pick(m, (1024, 512, 256))
    return pl.pallas_call(
        _mm_nt_kernel,
        grid=(m // tm,),
        in_specs=[pl.BlockSpec((tm, k), lambda i: (i, 0)), pl.BlockSpec((1, n, k), lambda i: (layer, 0, 0))],
        out_specs=pl.BlockSpec((tm, n), lambda i: (i, 0)),
        out_shape=jax.ShapeDtypeStruct((m, n), out_dtype),
        compiler_params=_params("arbitrary"),
        name=name,
    )(a, wt)


def _in_proj_kernel(a_ref, wt_hbm, o_ref, wbuf, wbf, sem, *, layer):
    j = pl.program_id(0)
    n_tiles = pl.num_programs(0)
    n_shift = P_CQ // IN_TILE

    def w_copy(jj, slot):
        row = pl.multiple_of(jnp.where(jj < n_shift, A_COLS + jj * IN_TILE, 0), MLA_ROPE)
        return pltpu.make_async_copy(wt_hbm.at[layer, pl.ds(row, IN_TILE), :], wbuf.at[slot], sem.at[slot])

    @pl.when(pl.program_id(1) == 0)
    def _():
        @pl.when(j == 0)
        def _():
            w_copy(0, 0).start()

        @pl.when(j + 1 < n_tiles)
        def _():
            w_copy(j + 1, (j + 1) % 2).start()

        w_copy(j, j % 2).wait()
        wbf[...] = wbuf[j % 2].astype(BF16)

    o_ref[...] = lax.dot_general(a_ref[...], wbf[...], NT_DIMS, preferred_element_type=F32).astype(o_ref.dtype)


def _in_proj(h, wt, layer):
    m, k = h.shape
    tm = _pick(m, (1024, 512, 256))
    return pl.pallas_call(
        functools.partial(_in_proj_kernel, layer=layer),
        grid=(P_COLS // IN_TILE, m // tm),
        in_specs=[pl.BlockSpec((tm, k), lambda j, i: (i, 0)), pl.BlockSpec(memory_space=pl.ANY)],
        out_specs=pl.BlockSpec((tm, IN_TILE), lambda j, i: (i, j)),
        out_shape=jax.ShapeDtypeStruct((m, P_COLS), BF16),
        scratch_shapes=[pltpu.VMEM((2, IN_TILE, k), F32), pltpu.VMEM((IN_TILE, k), BF16),
                        pltpu.SemaphoreType.DMA((2,))],
        compiler_params=_params("arbitrary", "arbitrary"),
        name="in_proj",
    )(h, wt)


def _mm_res_kernel(a_ref, w_ref, x_ref, ga_ref, o_ref, *, tm, bsz):
    r = _mod_row(pl.program_id(1), tm, bsz)
    acc = jnp.dot(a_ref[...], w_ref[0], preferred_element_type=F32)
    o_ref[...] = x_ref[...] + ga_ref[pl.ds(r, 1), :] * acc


def _out_proj_residual(mixed, w_o, layer, x_all, mod, bsz):
    m = mixed.shape[0]
    tm = _pick(m, (1024, 512, 256))
    tn = 1024
    rows = mod.shape[0]
    ga_blk = 2 * D_MODEL // tn
    return pl.pallas_call(
        functools.partial(_mm_res_kernel, tm=tm, bsz=bsz),
        grid=(D_MODEL // tn, m // tm),
        in_specs=[
            pl.BlockSpec((tm, D_MODEL), lambda j, i: (i, 0)),
            pl.BlockSpec((1, D_MODEL, tn), lambda j, i: (layer, 0, j)),
            pl.BlockSpec((tm, tn), lambda j, i: (i, j)),
            pl.BlockSpec((rows, tn), lambda j, i: (0, ga_blk + j)),
        ],
        out_specs=pl.BlockSpec((tm, tn), lambda j, i: (i, j)),
        out_shape=jax.ShapeDtypeStruct((m, D_MODEL), F32),
        compiler_params=_params("arbitrary", "arbitrary"),
        name="out_proj_residual",
    )(mixed, w_o, x_all, mod)


def _prep_kernel(cq_ref, ckv_ref, kpe_ref, bq_ref, bk_ref,
                 cosa_ref, sina_ref, csa_ref, cosb_ref, sinb_ref,
                 wuq_ref, wukv_ref, gqa_ref, gkva_ref, gqn_ref, gkn_ref,
                 qa_ref, ka_ref, va_ref, qb_ref, kb_ref):
    tm = cq_ref.shape[0]
    lane = lax.broadcasted_iota(jnp.int32, (tm, LANES), 1)

    nq = _rms(cq_ref[...].astype(F32), gqa_ref[...]).astype(BF16)
    qf = jnp.dot(nq, wuq_ref[...], preferred_element_type=F32)
    cosa = cosa_ref[...]
    sina = sina_ref[...]
    for h in range(MLA_HEADS):
        base = h * 3 * LANES
        nope = qf[:, base:base + LANES]
        roped = qf[:, base + LANES:base + 2 * LANES] * cosa + qf[:, base + 2 * LANES:base + 3 * LANES] * sina
        qa_ref[:, h * MLA_DK_PAD:h * MLA_DK_PAD + LANES] = (nope * MLA_SCALE).astype(BF16)
        qa_ref[:, h * MLA_DK_PAD + LANES:(h + 1) * MLA_DK_PAD] = (roped * MLA_SCALE).astype(BF16)

    nkv = _rms(ckv_ref[...].astype(F32), gkva_ref[...]).astype(BF16)
    kvf = jnp.dot(nkv, wukv_ref[...], preferred_element_type=F32)
    t = kpe_ref[...].astype(F32) * csa_ref[...]
    kpe = jnp.where(lane < MLA_ROPE, t + pltpu.roll(t, MLA_ROPE, axis=1), 0.0).astype(BF16)
    for h in range(MLA_HEADS):
        ka_ref[:, h * MLA_DK_PAD:h * MLA_DK_PAD + LANES] = kvf[:, h * LANES:(h + 1) * LANES].astype(BF16)
        ka_ref[:, h * MLA_DK_PAD + LANES:(h + 1) * MLA_DK_PAD] = kpe
    va_ref[...] = kvf[:, MLA_HEADS * MLA_NOPE:].astype(BF16)

    cosb = cosb_ref[...]
    sinb = sinb_ref[...]
    first = (lane % (GQA_HEAD_DIM // 2)) < (GQA_HEAD_DIM // 4)

    def rope_b(x):
        rot = jnp.where(first, pltpu.roll(x, LANES - GQA_HEAD_DIM // 4, axis=1),
                        pltpu.roll(x, GQA_HEAD_DIM // 4, axis=1))
        return x * cosb + rot * sinb

    gqn = gqn_ref[...]
    gkn = gkn_ref[...]
    for h in range(GQA_HEADS):
        x = bq_ref[:, h * LANES:(h + 1) * LANES].astype(F32)
        qb_ref[:, h * LANES:(h + 1) * LANES] = (rope_b(_rms(x, gqn)) * GQA_SCALE).astype(BF16)
    for h in range(GQA_KV_HEADS):
        x = bk_ref[:, h * LANES:(h + 1) * LANES].astype(F32)
        kb_ref[:, h * LANES:(h + 1) * LANES] = rope_b(_rms(x, gkn)).astype(BF16)


def _prep(p, pk, tables, w_uq_x, w_ukv_r, g_qa, g_kva, g_qn, g_kn):
    n = p.shape[0]
    tm = _pick(n, (512, 256))
    n_pos = SEQ // tm
    n_lat_tiles = (n // (SEQ + CTX_LEN)) * n_pos

    def tab(i):
        return (jnp.where(i < n_lat_tiles, i % n_pos, n_pos), 0)

    def col(width, off):
        return pl.BlockSpec((tm, width), lambda i: (i, off // width))

    def whole(a):
        return pl.BlockSpec(a.shape, lambda i: (0,) * a.ndim)

    tab_spec = pl.BlockSpec((tm, LANES), tab)
    cosa, sina, csa, cosb, sinb = tables
    outs = [(MLA_HEADS * MLA_DK_PAD, "qa"), (MLA_HEADS * MLA_DK_PAD, "ka"), (MLA_HEADS * MLA_V, "va"),
            (GQA_HEADS * GQA_HEAD_DIM, "qb"), (GQA_KV_HEADS * GQA_HEAD_DIM, "kb")]
    return pl.pallas_call(
        _prep_kernel,
        grid=(n // tm,),
        in_specs=[col(512, P_CQ), col(512, P_CKV), col(LANES, 0), col(512, P_B), col(256, P_B + 512),
                  tab_spec, tab_spec, tab_spec, tab_spec, tab_spec,
                  whole(w_uq_x), whole(w_ukv_r), whole(g_qa), whole(g_kva), whole(g_qn), whole(g_kn)],
        out_specs=[pl.BlockSpec((tm, w), lambda i: (i, 0)) for w, _ in outs],
        out_shape=[jax.ShapeDtypeStruct((n, w), BF16) for w, _ in outs],
        compiler_params=_params("arbitrary"),
        name="attn_prep",
    )(p, p, pk, p, p, cosa, sina, csa, cosb, sinb, w_uq_x, w_ukv_r, g_qa, g_kva, g_qn, g_kn)


def _attn_kernel(*refs, n_seg, n_groups, rep, dk, dv, scale):
    q_ref = refs[0]
    segs = [(refs[1 + 2 * s], refs[2 + 2 * s]) for s in range(n_seg)]
    o_ref = refs[1 + 2 * n_seg]
    nt = (((1,), (1,)), ((), ()))
    for g in range(n_groups):
        ks = [k_ref[:, g * dk:(g + 1) * dk] for k_ref, _ in segs]
        vs = []
        for _, v_ref in segs:
            one = (lax.broadcasted_iota(jnp.int32, (v_ref.shape[0], LANES), 1) == 0).astype(BF16)
            vs.append(jnp.concatenate([v_ref[:, g * dv:(g + 1) * dv], one], axis=1))
        for r in range(rep):
            hq = g * rep + r
            q = q_ref[:, hq * dk:(hq + 1) * dk]
            ss = [lax.dot_general(q, k, nt, preferred_element_type=F32) for k in ks]
            if scale != 1.0:
                ss = [s * scale for s in ss]
            m = functools.reduce(jnp.maximum, [jnp.max(s, axis=-1, keepdims=True) for s in ss])
            acc = None
            for s, v in zip(ss, vs):
                a = jnp.dot(jnp.exp(s - m).astype(BF16), v, preferred_element_type=F32)
                acc = a if acc is None else acc + a
            o_ref[:, hq * dv:(hq + 1) * dv] = (acc[:, :dv] / acc[:, dv:dv + 1]).astype(o_ref.dtype)


def _attention(q, q_off, segs, bsz, n_q_per_batch, q_row0, *, n_groups, rep, dk, dv, scale, name):
    tq = min(512, n_q_per_batch)
    nq = n_q_per_batch // tq
    qw = n_groups * rep * dk
    ow = n_groups * rep * dv

    def q_map(b, i):
        return ((q_row0 + b * n_q_per_batch) // tq + i, q_off // qw)

    in_specs = [pl.BlockSpec((tq, qw), q_map)]
    args = [q]
    for k_arr, k_off, v_arr, v_off, length, row0 in segs:
        kw, vw = n_groups * dk, n_groups * dv
        in_specs.append(pl.BlockSpec((length, kw), functools.partial(
            lambda b, i, length, row0, blk: (row0 // length + b, blk), length=length, row0=row0, blk=k_off // kw)))
        in_specs.append(pl.BlockSpec((length, vw), functools.partial(
            lambda b, i, length, row0, blk: (row0 // length + b, blk), length=length, row0=row0, blk=v_off // vw)))
        args += [k_arr, v_arr]
    return pl.pallas_call(
        functools.partial(_attn_kernel, n_seg=len(segs), n_groups=n_groups, rep=rep, dk=dk, dv=dv, scale=scale),
        grid=(bsz, nq),
        in_specs=in_specs,
        out_specs=pl.BlockSpec((tq, ow), lambda b, i: (b * nq + i, 0)),
        out_shape=jax.ShapeDtypeStruct((bsz * n_q_per_batch, ow), BF16),
        compiler_params=_params("arbitrary", "arbitrary"),
        name=name,
    )(*args)


def _na_kernel(q_ref, k_ref, v_ref, kc_ref, vc_ref, bias_ref, o_ref):
    j = pl.program_id(0)
    k_rows = NA_K_ROWS * GRID_W
    start = jnp.clip(NA_Q_ROWS * j - NA_WIN_H // 2, 0, SEQ // GRID_W - NA_K_ROWS) * GRID_W
    start = pl.multiple_of(start, 256)
    hd = NA_HEAD_DIM

    def with_ones(v):
        one = (lax.broadcasted_iota(jnp.int32, (v.shape[0], LANES), 1) == 0).astype(BF16)
        return jnp.concatenate([v, one], axis=1)

    for h in range(NA_HEADS):
        cols = slice(h * hd, (h + 1) * hd)
        q = q_ref[:, cols]
        s1 = lax.dot_general(q, k_ref[pl.ds(start, k_rows), cols], NT_DIMS, preferred_element_type=F32) * NA_SCALE
        s1 = s1 + bias_ref[0, h]
        s2 = lax.dot_general(q, kc_ref[:, cols], NT_DIMS, preferred_element_type=F32) * NA_SCALE
        m = jnp.maximum(jnp.max(s1, axis=-1, keepdims=True), jnp.max(s2, axis=-1, keepdims=True))
        acc = jnp.dot(jnp.exp(s1 - m).astype(BF16), with_ones(v_ref[pl.ds(start, k_rows), cols]),
                      preferred_element_type=F32)
        acc = acc + jnp.dot(jnp.exp(s2 - m).astype(BF16), with_ones(vc_ref[:, cols]), preferred_element_type=F32)
        o_ref[:, cols] = (acc[:, :hd] / acc[:, hd:hd + 1]).astype(o_ref.dtype)


def _na_bias_kernel(tp_ref, o_ref):
    t = pl.program_id(0)
    rows = SEQ // GRID_W
    k0 = jnp.clip(NA_Q_ROWS * t - NA_WIN_H // 2, 0, rows - NA_K_ROWS)
    left = lax.broadcasted_iota(jnp.int32, (GRID_W, 2 * GRID_W), 1) < GRID_W
    neg = -jnp.inf
    for a in range(NA_Q_ROWS):
        rq = NA_Q_ROWS * t + a
        r0 = jnp.clip(rq - NA_WIN_H // 2, 0, rows - NA_WIN_H)
        for b2 in range(NA_K_ROWS // 2):
            rk = k0 + 2 * b2
            ok_l = (rk >= r0) & (rk < r0 + NA_WIN_H)
            ok_r = (rk + 1 >= r0) & (rk + 1 < r0 + NA_WIN_H)
            pair = tp_ref[0, jnp.clip(rk - rq + NA_WIN_H, 0, 2 * NA_WIN_H - 1)]
            keep = jnp.where(left, ok_l.astype(jnp.int32), ok_r.astype(jnp.int32)) > 0
            o_ref[0, 0, a * GRID_W:(a + 1) * GRID_W, b2 * 2 * GRID_W:(b2 + 1) * 2 * GRID_W] = jnp.where(keep, pair, neg)


def _na_bias(rpb):
    n_dr = 2 * NA_WIN_H - 1
    cols = jnp.arange(GRID_W)
    c0 = jnp.clip(cols - NA_WIN_W // 2, 0, GRID_W - NA_WIN_W)
    col_ok = (cols[None, :] >= c0[:, None]) & (cols[None, :] < c0[:, None] + NA_WIN_W)
    dc = jnp.clip(cols[None, :] - cols[:, None] + NA_WIN_W - 1, 0, 2 * NA_WIN_W - 2)
    oh_c = (dc[:, :, None] == jnp.arange(2 * NA_WIN_W - 1)).astype(F32)
    by_col = jnp.einsum('hdc,qkc->hdqk', rpb.astype(F32), oh_c, precision=lax.Precision.HIGHEST)
    by_col = jnp.where(col_ok[None, None], by_col, -jnp.inf)
    zero = jnp.zeros((NA_HEADS, 1, GRID_W, GRID_W), F32)
    padded = jnp.concatenate([zero, by_col, zero], axis=1)
    pairs = jnp.concatenate([padded[:, :n_dr + 1], padded[:, 1:]], axis=-1)
    n_tiles = SEQ // GRID_W // NA_Q_ROWS
    tq, tk = NA_Q_ROWS * GRID_W, NA_K_ROWS * GRID_W
    return pl.pallas_call(
        _na_bias_kernel,
        grid=(n_tiles, NA_HEADS),
        in_specs=[pl.BlockSpec((1, n_dr + 1, GRID_W, 2 * GRID_W), lambda t, h: (h, 0, 0, 0))],
        out_specs=pl.BlockSpec((1, 1, tq, tk), lambda t, h: (t, h, 0, 0)),
        out_shape=jax.ShapeDtypeStruct((n_tiles, NA_HEADS, tq, tk), F32),
        compiler_params=_params("arbitrary", "arbitrary"),
        name="na_bias",
    )(pairs)


def _na_attention(p, bias, bsz, n_lat):
    tq = NA_Q_ROWS * GRID_W
    n_tiles = SEQ // tq
    w = NA_HEADS * NA_HEAD_DIM
    qb, kb, vb = P_C // w, (P_C + w) // w, (P_C + 2 * w) // w
    return pl.pallas_call(
        _na_kernel,
        grid=(n_tiles, bsz),
        in_specs=[
            pl.BlockSpec((tq, w), lambda j, b: (b * n_tiles + j, qb)),
            pl.BlockSpec((SEQ, w), lambda j, b: (b, kb)),
            pl.BlockSpec((SEQ, w), lambda j, b: (b, vb)),
            pl.BlockSpec((CTX_LEN, w), lambda j, b: (n_lat // CTX_LEN + b, kb)),
            pl.BlockSpec((CTX_LEN, w), lambda j, b: (n_lat // CTX_LEN + b, vb)),
            pl.BlockSpec((1, NA_HEADS, tq, NA_K_ROWS * GRID_W), lambda j, b: (j, 0, 0, 0)),
        ],
        out_specs=pl.BlockSpec((tq, w), lambda j, b: (b * n_tiles + j, 0)),
        out_shape=jax.ShapeDtypeStruct((n_lat, w), BF16),
        compiler_params=_params("arbitrary", "arbitrary"),
        name="na_attention",
    )(p, p, p, p, p, bias)


def _conv_kernel(b_ref, c_ref, u_ref, w_ref, o_ref):
    length = c_ref.shape[0]
    z = c_ref[...].astype(F32) * u_ref[...].astype(F32)
    row = lax.broadcasted_iota(jnp.int32, z.shape, 0)
    z_prev = jnp.where(row == 0, 0.0, pltpu.roll(z, 1, axis=0))
    z_next = jnp.where(row == length - 1, 0.0, pltpu.roll(z, length - 1, axis=0))
    conv = z_prev * w_ref[0:1, :] + z * w_ref[1:2, :] + z_next * w_ref[2:3, :]
    o_ref[...] = (b_ref[...].astype(F32) * conv).astype(o_ref.dtype)


def _short_conv(p, w_conv8, bsz, length, row0):
    blk = P_D // CONV_CH

    def rows(b):
        return row0 // length + b

    return pl.pallas_call(
        _conv_kernel,
        grid=(bsz,),
        in_specs=[
            pl.BlockSpec((length, CONV_CH), lambda b: (rows(b), blk)),
            pl.BlockSpec((length, CONV_CH), lambda b: (rows(b), blk + 1)),
            pl.BlockSpec((length, CONV_CH), lambda b: (rows(b), blk + 2)),
            pl.BlockSpec((8, CONV_CH), lambda b: (0, 0)),
        ],
        out_specs=pl.BlockSpec((length, CONV_CH), lambda b: (b, 0)),
        out_shape=jax.ShapeDtypeStruct((bsz * length, CONV_CH), BF16),
        compiler_params=_params("arbitrary"),
        name="short_conv",
    )(p, p, p, w_conv8)


MERGE_CHUNK = 512


def _merge_kernel(ya_ref, yb_ref, yc_ref, yd_ref, g0_ref, g1_ref, g2_ref, g3_ref, wb_ref, o_ref):
    pairs = ((ya_ref, g0_ref), (yb_ref, g1_ref), (yc_ref, g2_ref), (yd_ref, g3_ref))
    for c in range(D_MODEL // MERGE_CHUNK):
        cols = slice(c * MERGE_CHUNK, (c + 1) * MERGE_CHUNK)
        acc = None
        for k, (y_ref, g_ref) in enumerate(pairs):
            proj = jnp.dot(y_ref[...], wb_ref[0, k, :, cols], preferred_element_type=F32)
            gate = 0.5 * jnp.tanh(0.5 * g_ref[:, cols].astype(F32)) + 0.5
            acc = gate * proj if acc is None else acc + gate * proj
        o_ref[:, cols] = acc.astype(o_ref.dtype)


def _merge(ys, p, w_branch, layer, n_act):
    tm = _pick(n_act, (512, 256))
    g_specs = [pl.BlockSpec((tm, D_MODEL), functools.partial(
        lambda i, k: (i, P_G // D_MODEL + k), k=k)) for k in range(N_BRANCHES)]
    return pl.pallas_call(
        _merge_kernel,
        grid=(n_act // tm,),
        in_specs=[pl.BlockSpec((tm, BRANCH_WIDTH), lambda i: (i, 0))] * N_BRANCHES + g_specs + [
            pl.BlockSpec((1, N_BRANCHES, BRANCH_WIDTH, D_MODEL), lambda i: (layer, 0, 0, 0))],
        out_specs=pl.BlockSpec((tm, D_MODEL), lambda i: (i, 0)),
        out_shape=jax.ShapeDtypeStruct((n_act, D_MODEL), BF16),
        compiler_params=_params("arbitrary"),
        name="branch_merge",
    )(*ys, p, p, p, p, w_branch)


R_E, R_W, R_RANK = 0, 2, 4


def _router_kernel(x_ref, g_ref, sh_ref, sc_ref, wr_ref, br_ref, h_ref, route_ref, counts_ref, run_ref,
                   *, tm, bsz):
    step = pl.program_id(0)
    r = _mod_row(step, tm, bsz)
    h = _rms(x_ref[...], g_ref[...]) * (1.0 + sc_ref[pl.ds(r, 1), :]) + sh_ref[pl.ds(r, 1), :]
    h_ref[...] = h
    logits = jnp.dot(h, wr_ref[...], precision=lax.Precision.HIGHEST, preferred_element_type=F32)
    bias = br_ref[...]
    lane = lax.broadcasted_iota(jnp.int32, logits.shape, 1)
    lane_f = lane.astype(F32)
    neg = -jnp.inf

    def rmax(x, mask):
        return jnp.max(jnp.where(mask, x, neg), axis=-1, keepdims=True)

    def first_at(x, val, mask):
        return jnp.min(jnp.where(mask & (x == val), lane_f, 2.0 * LANES), axis=-1, keepdims=True)

    def pick(x, idx):
        return jnp.sum(jnp.where(lane_f == idx, x, 0.0), axis=-1, keepdims=True)

    gmask = lane < N_GROUPS
    ge = jnp.where(gmask, jnp.exp(logits - rmax(logits, gmask)), 0.0)
    g_prob = ge / jnp.sum(ge, axis=-1, keepdims=True)
    g_score = g_prob + bias
    g_sel = first_at(g_score, rmax(g_score, gmask), gmask)
    p_group = pick(g_prob, g_sel)

    group_of_lane = lax.shift_right_arithmetic(lane - N_EXPERTS, 3).astype(F32)
    emask = (lane >= N_EXPERTS) & (group_of_lane == g_sel)
    ee = jnp.where(emask, jnp.exp(logits - rmax(logits, emask)), 0.0)
    e_prob = ee / jnp.sum(ee, axis=-1, keepdims=True)
    e_score = e_prob + bias
    i1 = first_at(e_score, rmax(e_score, emask), emask)
    mask2 = emask & (lane_f != i1)
    i2 = first_at(e_score, rmax(e_score, mask2), mask2)
    p1 = pick(e_prob, i1)
    p2 = pick(e_prob, i2)
    w1 = p_group * p1 / (p1 + p2)
    w2 = p_group * p2 / (p1 + p2)
    e1 = i1 - N_EXPERTS
    e2 = i2 - N_EXPERTS

    @pl.when(step == 0)
    def _():
        run_ref[...] = jnp.zeros_like(run_ref)

    oh1 = lane_f == e1
    oh2 = lane_f == e2
    both = oh1.astype(F32) + oh2.astype(F32)
    tri = (lax.broadcasted_iota(jnp.int32, (tm, tm), 1) < lax.broadcasted_iota(jnp.int32, (tm, tm), 0)).astype(BF16)
    before = jnp.dot(tri, both.astype(BF16), preferred_element_type=F32) + run_ref[0:1, :]
    rank1 = jnp.sum(jnp.where(oh1, before, 0.0), axis=-1, keepdims=True)
    rank2 = jnp.sum(jnp.where(oh2, before + oh1.astype(F32), 0.0), axis=-1, keepdims=True)
    run_ref[...] = run_ref[...] + jnp.sum(both, axis=0, keepdims=True)
    counts_ref[...] = run_ref[...]

    vals = (e1, e2, w1, w2, rank1, rank2)
    out = jnp.zeros_like(logits)
    for k, v in enumerate(vals):
        out = jnp.where(lane == k, v, out)
    route_ref[...] = out


def _ffn_input_and_route(x1, gain, mod, w_route, b_route, bsz):
    n = x1.shape[0]
    tm = _pick(n, (512, 256))
    rows = mod.shape[0]
    return pl.pallas_call(
        functools.partial(_router_kernel, tm=tm, bsz=bsz),
        grid=(n // tm,),
        in_specs=[
            pl.BlockSpec((tm, D_MODEL), lambda i: (i, 0)),
            pl.BlockSpec((1, D_MODEL), lambda i: (0, 0)),
            pl.BlockSpec((rows, D_MODEL), lambda i: (0, 3)),
            pl.BlockSpec((rows, D_MODEL), lambda i: (0, 4)),
            pl.BlockSpec((D_MODEL, LANES), lambda i: (0, 0)),
            pl.BlockSpec((1, LANES), lambda i: (0, 0)),
        ],
        out_specs=[pl.BlockSpec((tm, D_MODEL), lambda i: (i, 0)),
                   pl.BlockSpec((tm, LANES), lambda i: (i, 0)),
                   pl.BlockSpec((8, LANES), lambda i: (0, 0))],
        out_shape=[jax.ShapeDtypeStruct((n, D_MODEL), F32),
                   jax.ShapeDtypeStruct((n, LANES), F32),
                   jax.ShapeDtypeStruct((8, LANES), F32)],
        scratch_shapes=[pltpu.VMEM((8, LANES), F32)],
        compiler_params=_params("arbitrary"),
        name="ffn_input_route",
    )(x1, gain.reshape(1, D_MODEL), mod, mod, w_route, b_route)


def _dispatch_indices(route, counts_row, n_tok):
    counts = counts_row[0, :N_EXPERTS].astype(jnp.int32)
    padded = (counts + MOE_BLOCK - 1) // MOE_BLOCK * MOE_BLOCK
    pad_end = jnp.cumsum(padded)
    pad_start = pad_end - padded
    eid = route[:, R_E:R_E + TOP_K].astype(jnp.int32)
    rank = route[:, R_RANK:R_RANK + TOP_K].astype(jnp.int32)
    onehot = eid[:, :, None] == jnp.arange(N_EXPERTS, dtype=jnp.int32)[None, None, :]
    pos = (jnp.sum(jnp.where(onehot, pad_start[None, None, :], 0), axis=-1) + rank).reshape(-1)
    n_blocks = (n_tok * TOP_K + N_EXPERTS * (MOE_BLOCK - 1)) // MOE_BLOCK
    blk_start = jnp.arange(n_blocks, dtype=jnp.int32) * MOE_BLOCK
    blk_e = jnp.minimum(jnp.sum(pad_end[None, :] <= blk_start[:, None], axis=1), N_EXPERTS - 1).astype(jnp.int32)
    n_used = (pad_end[-1:] // MOE_BLOCK).astype(jnp.int32)
    last_blk = jnp.where(counts > 0, pad_end // MOE_BLOCK - 1, -1).astype(jnp.int32)
    blk = jnp.arange(n_blocks, dtype=jnp.int32)
    prev_e = jnp.concatenate([jnp.full((1,), -1, jnp.int32), blk_e[:-1]])
    first = (blk < n_used) & (blk_e != prev_e)
    ordinal = jnp.cumsum(first.astype(jnp.int32)) - 1
    n_first = jnp.sum(first.astype(jnp.int32))
    k = jnp.arange(N_EXPERTS, dtype=jnp.int32)
    expert_at = jnp.sum(jnp.where(first[None, :] & (ordinal[None, :] == k[:, None]), blk_e[None, :], 0), axis=1)

    def expert_after(d):
        target = ordinal + d
        e = jnp.sum(jnp.where(k[None, :] == target[:, None], expert_at[None, :], 0), axis=1)
        return jnp.where(target < n_first, e, -1).astype(jnp.int32)

    sched = (blk_e, n_used, first.astype(jnp.int32), (ordinal % MOE_W_SLOTS).astype(jnp.int32),
             expert_after(1), expert_after(2))
    return sched, last_blk, pos.astype(jnp.int32), n_blocks


def _row_copy(src, src_row, dst, dst_row, sem):
    return pltpu.make_async_copy(src.at[pl.ds(src_row, 1), :], dst.at[pl.ds(dst_row, 1), :], sem)


DISPATCH_CHUNK = 512
DMA_UNROLL = 8


def _dispatch_kernel(pos_ref, last_ref, nused_ref, h_ref, xs_hbm, zbuf, sem, zsem):
    base = pl.program_id(0) * DISPATCH_CHUNK
    n_blocks = xs_hbm.shape[0] // MOE_BLOCK

    @pl.when(pl.program_id(0) == 0)
    def _():
        zbuf[...] = jnp.zeros_like(zbuf)

        def fill(b):
            return pltpu.make_async_copy(zbuf, xs_hbm.at[pl.ds(pl.multiple_of(b * MOE_BLOCK, MOE_BLOCK), MOE_BLOCK), :],
                                         zsem.at[0])

        def each(fn):
            def expert(e, carry):
                @pl.when(last_ref[e] >= 0)
                def _():
                    fn(fill(last_ref[e]))
                return carry

            def unused(b, carry):
                fn(fill(b))
                return carry

            lax.fori_loop(0, N_EXPERTS, expert, 0)
            lax.fori_loop(nused_ref[0], n_blocks, unused, 0)

        each(lambda c: c.start())
        each(lambda c: c.wait())

    def body(r, carry):
        for k in range(TOP_K):
            _row_copy(h_ref, r, xs_hbm, pos_ref[(base + r) * TOP_K + k], sem.at[0]).start()
        return carry

    lax.fori_loop(0, DISPATCH_CHUNK, body, 0, unroll=DMA_UNROLL)
    for k in range(TOP_K):
        pltpu.make_async_copy(h_ref, xs_hbm.at[pl.ds(0, DISPATCH_CHUNK), :], sem.at[0]).wait()


def _dispatch(h2, pos, last_blk, n_used, n_rows):
    n, width = h2.shape
    assert n % DISPATCH_CHUNK == 0
    grid_spec = pltpu.PrefetchScalarGridSpec(
        num_scalar_prefetch=3,
        grid=(n // DISPATCH_CHUNK,),
        in_specs=[pl.BlockSpec((DISPATCH_CHUNK, width), lambda i, *_: (i, 0))],
        out_specs=pl.BlockSpec(memory_space=pl.ANY),
        scratch_shapes=[pltpu.VMEM((MOE_BLOCK, width), h2.dtype), pltpu.SemaphoreType.DMA((1,)),
                        pltpu.SemaphoreType.DMA((1,))],
    )
    return pl.pallas_call(
        _dispatch_kernel,
        grid_spec=grid_spec,
        out_shape=jax.ShapeDtypeStruct((n_rows, width), h2.dtype),
        compiler_params=_params("arbitrary"),
        name="moe_dispatch",
    )(pos, last_blk, n_used, h2)


def _moe_kernel(be_ref, nused_ref, first_ref, slot_ref, nxt1_ref, nxt2_ref, x_ref, wg_hbm, wu_hbm, wd_hbm, o_ref,
                wgf, wuf, wdf, wgb, wub, wdb, sem, *, layer):
    i = pl.program_id(0)

    def w_copies(e, s):
        return (pltpu.make_async_copy(wg_hbm.at[layer, e], wgf.at[s], sem.at[s]),
                pltpu.make_async_copy(wu_hbm.at[layer, e], wuf.at[s], sem.at[s]),
                pltpu.make_async_copy(wd_hbm.at[layer, e], wdf.at[s], sem.at[s]))

    def start(e, s):
        for c in w_copies(e, s):
            c.start()

    @pl.when(first_ref[i] == 1)
    def _():
        s = slot_ref[i]

        @pl.when(i == 0)
        def _():
            start(be_ref[i], s)

            @pl.when(nxt1_ref[i] >= 0)
            def _():
                start(nxt1_ref[i], (s + 1) % MOE_W_SLOTS)

        @pl.when(nxt2_ref[i] >= 0)
        def _():
            start(nxt2_ref[i], (s + 2) % MOE_W_SLOTS)

        for c in w_copies(be_ref[i], s):
            c.wait()
        wgb[...] = wgf[s].astype(BF16)
        wub[...] = wuf[s].astype(BF16)
        wdb[...] = wdf[s].astype(BF16)

    @pl.when(i < nused_ref[0])
    def _():
        x = x_ref[...].astype(BF16)
        gate = jnp.dot(x, wgb[...], preferred_element_type=F32)
        up = jnp.dot(x, wub[...], preferred_element_type=F32)
        hid = (gate * jax.nn.sigmoid(gate) * up).astype(BF16)
        o_ref[...] = jnp.dot(hid, wdb[...], preferred_element_type=F32)

    @pl.when(i >= nused_ref[0])
    def _():
        o_ref[...] = jnp.zeros_like(o_ref)


def _moe(xs, sched, w_gate, w_up, w_down, layer):
    n_blocks = sched[0].shape[0]
    grid_spec = pltpu.PrefetchScalarGridSpec(
        num_scalar_prefetch=len(sched),
        grid=(n_blocks,),
        in_specs=[pl.BlockSpec((MOE_BLOCK, D_MODEL), lambda i, *_: (i, 0))] + [pl.BlockSpec(memory_space=pl.ANY)] * 3,
        out_specs=pl.BlockSpec((MOE_BLOCK, D_MODEL), lambda i, *_: (i, 0)),
        scratch_shapes=[
            pltpu.VMEM((MOE_W_SLOTS, D_MODEL, EXPERT_HIDDEN), F32),
            pltpu.VMEM((MOE_W_SLOTS, D_MODEL, EXPERT_HIDDEN), F32),
            pltpu.VMEM((MOE_W_SLOTS, EXPERT_HIDDEN, D_MODEL), F32),
            pltpu.VMEM((D_MODEL, EXPERT_HIDDEN), BF16),
            pltpu.VMEM((D_MODEL, EXPERT_HIDDEN), BF16),
            pltpu.VMEM((EXPERT_HIDDEN, D_MODEL), BF16),
            pltpu.SemaphoreType.DMA((MOE_W_SLOTS,)),
        ],
    )
    return pl.pallas_call(
        functools.partial(_moe_kernel, layer=layer),
        grid_spec=grid_spec,
        out_shape=jax.ShapeDtypeStruct((n_blocks * MOE_BLOCK, D_MODEL), F32),
        compiler_params=_params("arbitrary"),
        name="moe_experts",
    )(*sched, xs, w_gate, w_up, w_down)


def _combine_kernel(pos_ref, ys_hbm, x_ref, route_ref, ga_ref, gf_ref, o_ref, rbuf, sem, *, tm, bsz, final):
    i = pl.program_id(0)
    n = pl.num_programs(0)
    slot = i % 2

    def gather(blk, s):
        def body(r, carry):
            for k in range(TOP_K):
                _row_copy(ys_hbm, pos_ref[(blk * tm + r) * TOP_K + k], rbuf.at[s], k * tm + r, sem.at[s]).start()
            return carry
        lax.fori_loop(0, tm, body, 0, unroll=DMA_UNROLL)

    @pl.when(i == 0)
    def _():
        gather(0, 0)

    @pl.when(i + 1 < n)
    def _():
        gather(i + 1, 1 - slot)

    pltpu.make_async_copy(ys_hbm.at[pl.ds(0, TOP_K * tm), :], rbuf.at[slot], sem.at[slot]).wait()
    y = rbuf[slot, 0:tm, :] * route_ref[:, R_W:R_W + 1] + rbuf[slot, tm:2 * tm, :] * route_ref[:, R_W + 1:R_W + 2]
    r = _mod_row(i, tm, bsz)
    x = x_ref[...] + ga_ref[pl.ds(r, 1), :] * y
    if final:
        x = _rms(x, gf_ref[...])
    o_ref[...] = x


def _combine(ys, pos, x1, route, mod, g_final, bsz, final):
    n = x1.shape[0]
    tm = 128
    rows = mod.shape[0]
    grid_spec = pltpu.PrefetchScalarGridSpec(
        num_scalar_prefetch=1,
        grid=(n // tm,),
        in_specs=[
            pl.BlockSpec(memory_space=pl.ANY),
            pl.BlockSpec((tm, D_MODEL), lambda i, pos: (i, 0)),
            pl.BlockSpec((tm, LANES), lambda i, pos: (i, 0)),
            pl.BlockSpec((rows, D_MODEL), lambda i, pos: (0, 5)),
            pl.BlockSpec((1, D_MODEL), lambda i, pos: (0, 0)),
        ],
        out_specs=pl.BlockSpec((tm, D_MODEL), lambda i, pos: (i, 0)),
        scratch_shapes=[pltpu.VMEM((2, TOP_K * tm, D_MODEL), F32), pltpu.SemaphoreType.DMA((2,))],
    )
    return pl.pallas_call(
        functools.partial(_combine_kernel, tm=tm, bsz=bsz, final=final),
        grid_spec=grid_spec,
        out_shape=jax.ShapeDtypeStruct((n, D_MODEL), F32),
        compiler_params=_params("arbitrary"),
        name="moe_combine",
    )(pos, ys, x1, route, mod, g_final.reshape(1, D_MODEL))


def _rotate_half_cols(w, quarter):
    r1, r2, c1, c2 = (w[..., k * quarter:(k + 1) * quarter] for k in range(4))
    return jnp.concatenate([-r2, r1, -c2, c1], axis=-1)


def _kpe_rows(wt):
    kpe = jnp.swapaxes(wt[:, MLA_Q_LORA + MLA_KV_LORA:A_COLS, :], 1, 2)
    both = jnp.concatenate([kpe, _rotate_half_cols(kpe, MLA_ROPE // 4)], axis=-1)
    return jnp.swapaxes(both, 1, 2).astype(BF16)


def _relayout_w_uq(w_uq):
    w = w_uq.reshape(MLA_Q_LORA, MLA_HEADS, MLA_NOPE + MLA_ROPE)
    nope, pe = w[..., :MLA_NOPE], w[..., MLA_NOPE:]
    z = jnp.zeros_like(pe)
    out = jnp.concatenate([nope, pe, z, _rotate_half_cols(pe, MLA_ROPE // 4), z], axis=-1)
    return out.reshape(MLA_Q_LORA, MLA_HEADS * 3 * LANES).astype(BF16)


def _relayout_w_ukv(w_ukv):
    w = w_ukv.reshape(MLA_KV_LORA, MLA_HEADS, MLA_NOPE + MLA_V)
    return jnp.concatenate([w[..., :MLA_NOPE].reshape(MLA_KV_LORA, -1),
                            w[..., MLA_NOPE:].reshape(MLA_KV_LORA, -1)], axis=-1).astype(BF16)


def _rope_tables(tm):
    t = jnp.arange(SEQ)
    rows = (t // GRID_W).astype(F32)
    cols = (t % GRID_W).astype(F32)

    def cs(rot_dim):
        quarter = rot_dim // 4
        inv_freq = ROPE_THETA ** (-jnp.arange(quarter, dtype=F32) / quarter)
        ang_r = rows[:, None] * inv_freq[None, :]
        ang_c = cols[:, None] * inv_freq[None, :]
        cos = jnp.concatenate([jnp.cos(ang_r)] * 2 + [jnp.cos(ang_c)] * 2, axis=-1)
        sin = jnp.concatenate([jnp.sin(ang_r)] * 2 + [jnp.sin(ang_c)] * 2, axis=-1)
        return cos, sin

    def with_identity(tab, ident):
        return jnp.concatenate([tab, jnp.broadcast_to(ident, (tm, LANES))], axis=0)

    cos_a, sin_a = cs(MLA_ROPE)
    zeros = jnp.zeros((SEQ, LANES - MLA_ROPE), F32)
    lane = jnp.arange(LANES)
    one_lo = (lane < MLA_ROPE).astype(F32)[None, :]
    cosa = with_identity(jnp.concatenate([cos_a, zeros], axis=1), one_lo)
    sina = with_identity(jnp.concatenate([sin_a, zeros], axis=1), jnp.zeros((1, LANES), F32))
    csa = with_identity(jnp.concatenate([cos_a, sin_a], axis=1), one_lo)
    cos_b, sin_b = cs(GQA_HEAD_DIM)
    sign = jnp.where((lane % (GQA_HEAD_DIM // 2)) < GQA_HEAD_DIM // 4, -1.0, 1.0).astype(F32)[None, :]
    cosb = with_identity(cos_b, jnp.ones((1, LANES), F32))
    sinb = with_identity(sin_b * sign, jnp.zeros((1, LANES), F32))
    return cosa, sina, csa, cosb, sinb


def kernel(x, c, ctx, c_ctx, w_mod, b_mod, g_mix, g_ffn, w_in, w_uq, g_qa, w_ukv, g_kva, g_qn, g_kn, rpb, w_conv,
           w_branch, w_o, w_group, b_group, w_router, b_router, w_gate_e, w_up_e, w_down_e, g_final):
    bsz = x.shape[0]
    n_lat = bsz * SEQ
    n_ctx = bsz * CTX_LEN
    n_all = n_lat + n_ctx
    mod_rows = -(-(bsz + 1) // 8) * 8

    x_all = jnp.concatenate([x.reshape(n_lat, D_MODEL), ctx.reshape(n_ctx, D_MODEL)], axis=0)
    c_all = jnp.concatenate([c, c_ctx[None, :], jnp.zeros((mod_rows - bsz - 1, D_MODEL), F32)], axis=0)
    mods = _modulation(c_all, w_mod, b_mod)
    tables = _rope_tables(_pick(n_all, (512, 256)))

    w_in_t = jnp.swapaxes(w_in, 1, 2)
    w_kpe_t = _kpe_rows(w_in_t)
    w_branch_b = w_branch.astype(BF16)
    w_o_b = w_o.astype(BF16)

    out = None
    for l in range(DEPTH):
        last = l == DEPTH - 1
        n_act = n_lat if last else n_all
        mod = mods[l]
        w_uq_x = _relayout_w_uq(w_uq[l])
        w_ukv_r = _relayout_w_ukv(w_ukv[l])
        w_route = jnp.concatenate([w_group[l], jnp.zeros((D_MODEL, N_EXPERTS - N_GROUPS), F32), w_router[l]], axis=1)
        b_route = jnp.concatenate([b_group[l], jnp.zeros((N_EXPERTS - N_GROUPS,), F32), b_router[l]])[None, :]
        w_conv8 = jnp.concatenate([w_conv[l], jnp.zeros((8 - CONV_WIDTH, CONV_CH), F32)], axis=0)

        h = _hmod(x_all, g_mix[l], mod, bsz)
        p = _in_proj(h, w_in_t, l)
        pk = _matmul_nt(h, w_kpe_t, l, BF16, "kpe_proj")
        qa, ka, va, qb, kb = _prep(p, pk, tables, w_uq_x, w_ukv_r, g_qa[l][None, :], g_kva[l][None, :],
                                   g_qn[l][None, :], g_kn[l][None, :])

        lat_ctx = lambda k_arr, k_off, v_arr, v_off: [(k_arr, k_off, v_arr, v_off, SEQ, 0),
                                                      (k_arr, k_off, v_arr, v_off, CTX_LEN, n_lat)]
        ctx_only = lambda k_arr, k_off, v_arr, v_off: [(k_arr, k_off, v_arr, v_off, CTX_LEN, n_lat)]
        mla = dict(n_groups=MLA_HEADS, rep=1, dk=MLA_DK_PAD, dv=MLA_V, scale=1.0)
        gqa = dict(n_groups=GQA_KV_HEADS, rep=GQA_HEADS // GQA_KV_HEADS, dk=GQA_HEAD_DIM, dv=GQA_HEAD_DIM, scale=1.0)
        nac = dict(n_groups=NA_HEADS, rep=1, dk=NA_HEAD_DIM, dv=NA_HEAD_DIM, scale=NA_SCALE)

        ys_lat = [
            _attention(qa, 0, lat_ctx(ka, 0, va, 0), bsz, SEQ, 0, name="mla_attention", **mla),
            _attention(qb, 0, lat_ctx(kb, 0, p, P_B + 768), bsz, SEQ, 0, name="gqa_attention", **gqa),
            _na_attention(p, _na_bias(rpb[l]), bsz, n_lat),
            _short_conv(p, w_conv8, bsz, SEQ, 0),
        ]
        if last:
            ys_all = ys_lat
        else:
            ys_ctx = [
                _attention(qa, 0, ctx_only(ka, 0, va, 0), bsz, CTX_LEN, n_lat, name="mla_attention_ctx", **mla),
                _attention(qb, 0, ctx_only(kb, 0, p, P_B + 768), bsz, CTX_LEN, n_lat,
                           name="gqa_attention_ctx", **gqa),
                _attention(p, P_C, ctx_only(p, P_C + 512, p, P_C + 1024), bsz, CTX_LEN, n_lat,
                           name="na_attention_ctx", **nac),
                _short_conv(p, w_conv8, bsz, CTX_LEN, n_lat),
            ]
            ys_all = [jnp.concatenate([a, b], axis=0) for a, b in zip(ys_lat, ys_ctx)]

        mixed = _merge(ys_all, p, w_branch_b, l, n_act)
        x1 = _out_proj_residual(mixed, w_o_b, l, x_all, mod, bsz)
        h2, route, counts = _ffn_input_and_route(x1, g_ffn[l], mod, w_route, b_route, bsz)
        sched, last_blk, pos, n_blocks = _dispatch_indices(route, counts, n_act)
        xs = _dispatch(h2, pos, last_blk, sched[1], n_blocks * MOE_BLOCK)
        ys = _moe(xs, sched, w_gate_e, w_up_e, w_down_e, l)
        x_all = _combine(ys, pos, x1, route, mod, g_final, bsz, last)
        out = x_all
    return out.reshape(bsz, SEQ, D_MODEL)
```

```python
import functools

import jax
import jax.numpy as jnp
from jax import lax
from jax.experimental import pallas as pl
from jax.experimental.pallas import tpu as pltpu

F32 = jnp.float32
BF16 = jnp.bfloat16

D_MODEL = 2048
SEQ = 2048
DEPTH = 2
CTX_LEN = 256
GRID_W = 64
ROPE_THETA = 10000.0
NORM_EPS = 1e-6
MOD_CHUNKS = 6

MLA_HEADS = 4
MLA_Q_LORA = 512
MLA_KV_LORA = 512
MLA_NOPE = 128
MLA_ROPE = 64
MLA_V = 128
MLA_SCALE = (MLA_NOPE + MLA_ROPE) ** -0.5
MLA_DK_PAD = 256

GQA_HEADS = 4
GQA_KV_HEADS = 2
GQA_HEAD_DIM = 128
GQA_SCALE = GQA_HEAD_DIM ** -0.5

NA_HEADS = 4
NA_HEAD_DIM = 128
NA_WIN_H = 8
NA_WIN_W = 16
NA_SCALE = NA_HEAD_DIM ** -0.5
NA_Q_ROWS = 8
NA_K_ROWS = 16

CONV_CH = 512
CONV_WIDTH = 3
BRANCH_WIDTH = 512
N_BRANCHES = 4

A_COLS = MLA_Q_LORA + MLA_KV_LORA + MLA_ROPE
B_COLS = (GQA_HEADS + 2 * GQA_KV_HEADS) * GQA_HEAD_DIM
C_COLS = 3 * NA_HEADS * NA_HEAD_DIM
D_COLS = 3 * CONV_CH
G_COLS = N_BRANCHES * D_MODEL

P_B = 0
P_C = P_B + B_COLS
P_D = P_C + C_COLS
P_G = P_D + D_COLS
P_CQ = P_G + G_COLS
P_CKV = P_CQ + MLA_Q_LORA
P_COLS = P_CKV + MLA_KV_LORA
IN_TILE = 1024

N_GROUPS = 8
EXPERTS_PER_GROUP = 8
N_EXPERTS = 64
TOP_K = 2
EXPERT_HIDDEN = 512
MOE_BLOCK = 128
MOE_W_SLOTS = 3

LANES = 128
BULK_DMA_PRIORITY = 1
VMEM_LIMIT = 56 * 1024 * 1024


def _params(*sem):
    return pltpu.CompilerParams(dimension_semantics=sem, vmem_limit_bytes=VMEM_LIMIT)


def _pick(n, cands):
    for c in cands:
        if n % c == 0:
            return c
    raise ValueError(f"no tile for {n}")


def _rms(x, g):
    return x * lax.rsqrt(jnp.mean(x * x, axis=-1, keepdims=True) + NORM_EPS) * g


def _mod_row(i, tm, bsz):
    return jnp.minimum(i * tm // SEQ, bsz)


def _mod_kernel(c_ref, w_ref, b_ref, o_ref):
    c = c_ref[...]
    s = c * jax.nn.sigmoid(c)
    o_ref[0] = jnp.dot(s, w_ref[0], precision=lax.Precision.HIGHEST,
                       preferred_element_type=F32) + b_ref[0]


def _modulation(c_all, w_mod, b_mod):
    rows = c_all.shape[0]
    n = MOD_CHUNKS * D_MODEL
    tn = 1024
    return pl.pallas_call(
        _mod_kernel,
        grid=(DEPTH, n // tn),
        in_specs=[
            pl.BlockSpec((rows, D_MODEL), lambda l, j: (0, 0)),
            pl.BlockSpec((1, D_MODEL, tn), lambda l, j: (l, 0, j)),
            pl.BlockSpec((1, 1, tn), lambda l, j: (l, 0, j)),
        ],
        out_specs=pl.BlockSpec((1, rows, tn), lambda l, j: (l, 0, j)),
        out_shape=jax.ShapeDtypeStruct((DEPTH, rows, n), F32),
        compiler_params=_params("arbitrary", "arbitrary"),
        name="modulation",
    )(c_all, w_mod, b_mod.reshape(DEPTH, 1, n))


def _hmod_kernel(x_ref, g_ref, sh_ref, sc_ref, o_ref, *, tm, bsz):
    r = _mod_row(pl.program_id(0), tm, bsz)
    y = _rms(x_ref[...], g_ref[...])
    o_ref[...] = (y * (1.0 + sc_ref[pl.ds(r, 1), :]) + sh_ref[pl.ds(r, 1), :]).astype(o_ref.dtype)


def _hmod(x_all, gain, mod, bsz):
    n = x_all.shape[0]
    tm = _pick(n, (512, 256))
    rows = mod.shape[0]
    return pl.pallas_call(
        functools.partial(_hmod_kernel, tm=tm, bsz=bsz),
        grid=(n // tm,),
        in_specs=[
            pl.BlockSpec((tm, D_MODEL), lambda i: (i, 0)),
            pl.BlockSpec((1, D_MODEL), lambda i: (0, 0)),
            pl.BlockSpec((rows, D_MODEL), lambda i: (0, 0)),
            pl.BlockSpec((rows, D_MODEL), lambda i: (0, 1)),
        ],
        out_specs=pl.BlockSpec((tm, D_MODEL), lambda i: (i, 0)),
        out_shape=jax.ShapeDtypeStruct((n, D_MODEL), BF16),
        compiler_params=_params("arbitrary"),
        name="hmod",
    )(x_all, gain.reshape(1, D_MODEL), mod, mod)


NT_DIMS = (((1,), (1,)), ((), ()))


def _mm_nt_kernel(a_ref, w_ref, o_ref):
    o_ref[...] = lax.dot_general(a_ref[...], w_ref[0], NT_DIMS, preferred_element_type=F32).astype(o_ref.dtype)


def _matmul_nt(a, wt, layer, out_dtype, name):
    m, k = a.shape
    n = wt.shape[1]
    tm = _pick(m, (1024, 512, 256))
    return pl.pallas_call(
        _mm_nt_kernel,
        grid=(m // tm,),
        in_specs=[pl.BlockSpec((tm, k), lambda i: (i, 0)), pl.BlockSpec((1, n, k), lambda i: (layer, 0, 0))],
        out_specs=pl.BlockSpec((tm, n), lambda i: (i, 0)),
        out_shape=jax.ShapeDtypeStruct((m, n), out_dtype),
        compiler_params=_params("arbitrary"),
        name=name,
    )(a, wt)


def _in_proj_kernel(a_ref, wt_hbm, o_ref, wbuf, wbf, sem, *, layer):
    j = pl.program_id(0)
    n_tiles = pl.num_programs(0)
    n_shift = P_CQ // IN_TILE

    def w_copy(jj, slot):
        row = pl.multiple_of(jnp.where(jj < n_shift, A_COLS + jj * IN_TILE, 0), MLA_ROPE)
        return pltpu.make_async_copy(wt_hbm.at[layer, pl.ds(row, IN_TILE), :], wbuf.at[slot], sem.at[slot])

    @pl.when(pl.program_id(1) == 0)
    def _():
        @pl.when(j == 0)
        def _():
            w_copy(0, 0).start(priority=BULK_DMA_PRIORITY)

        @pl.when(j + 1 < n_tiles)
        def _():
            w_copy(j + 1, (j + 1) % 2).start(priority=BULK_DMA_PRIORITY)

        w_copy(j, j % 2).wait()
        wbf[...] = wbuf[j % 2].astype(BF16)

    o_ref[...] = lax.dot_general(a_ref[...], wbf[...], NT_DIMS, preferred_element_type=F32).astype(o_ref.dtype)


def _in_proj(h, wt, layer):
    m, k = h.shape
    tm = _pick(m, (1024, 512, 256))
    return pl.pallas_call(
        functools.partial(_in_proj_kernel, layer=layer),
        grid=(P_COLS // IN_TILE, m // tm),
        in_specs=[pl.BlockSpec((tm, k), lambda j, i: (i, 0)), pl.BlockSpec(memory_space=pl.ANY)],
        out_specs=pl.BlockSpec((tm, IN_TILE), lambda j, i: (i, j)),
        out_shape=jax.ShapeDtypeStruct((m, P_COLS), BF16),
        scratch_shapes=[pltpu.VMEM((2, IN_TILE, k), F32), pltpu.VMEM((IN_TILE, k), BF16),
                        pltpu.SemaphoreType.DMA((2,))],
        compiler_params=_params("arbitrary", "arbitrary"),
        name="in_proj",
    )(h, wt)


def _mm_res_kernel(a_ref, w_ref, x_ref, ga_ref, o_ref, *, tm, bsz):
    r = _mod_row(pl.program_id(1), tm, bsz)
    acc = jnp.dot(a_ref[...], w_ref[0], preferred_element_type=F32)
    o_ref[...] = x_ref[...] + ga_ref[pl.ds(r, 1), :] * acc


def _out_proj_residual(mixed, w_o, layer, x_all, mod, bsz):
    m = mixed.shape[0]
    tm = _pick(m, (1024, 512, 256))
    tn = 1024
    rows = mod.shape[0]
    ga_blk = 2 * D_MODEL // tn
    return pl.pallas_call(
        functools.partial(_mm_res_kernel, tm=tm, bsz=bsz),
        grid=(D_MODEL // tn, m // tm),
        in_specs=[
            pl.BlockSpec((tm, D_MODEL), lambda j, i: (i, 0)),
            pl.BlockSpec((1, D_MODEL, tn), lambda j, i: (layer, 0, j)),
            pl.BlockSpec((tm, tn), lambda j, i: (i, j)),
            pl.BlockSpec((rows, tn), lambda j, i: (0, ga_blk + j)),
        ],
        out_specs=pl.BlockSpec((tm, tn), lambda j, i: (i, j)),
        out_shape=jax.ShapeDtypeStruct((m, D_MODEL), F32),
        compiler_params=_params("arbitrary", "arbitrary"),
        name="out_proj_residual",
    )(mixed, w_o, x_all, mod)


def _prep_kernel(cq_ref, ckv_ref, kpe_ref, bq_ref, bk_ref,
                 cosa_ref, sina_ref, csa_ref, cosb_ref, sinb_ref,
                 wuq_ref, wukv_ref, gqa_ref, gkva_ref, gqn_ref, gkn_ref,
                 qa_ref, ka_ref, va_ref, qb_ref, kb_ref):
    tm = cq_ref.shape[0]
    lane = lax.broadcasted_iota(jnp.int32, (tm, LANES), 1)

    nq = _rms(cq_ref[...].astype(F32), gqa_ref[...]).astype(BF16)
    qf = jnp.dot(nq, wuq_ref[...], preferred_element_type=F32)
    cosa = cosa_ref[...]
    sina = sina_ref[...]
    for h in range(MLA_HEADS):
        base = h * 3 * LANES
        nope = qf[:, base:base + LANES]
        roped = qf[:, base + LANES:base + 2 * LANES] * cosa + qf[:, base + 2 * LANES:base + 3 * LANES] * sina
        qa_ref[:, h * MLA_DK_PAD:h * MLA_DK_PAD + LANES] = (nope * MLA_SCALE).astype(BF16)
        qa_ref[:, h * MLA_DK_PAD + LANES:(h + 1) * MLA_DK_PAD] = (roped * MLA_SCALE).astype(BF16)

    nkv = _rms(ckv_ref[...].astype(F32), gkva_ref[...]).astype(BF16)
    kvf = jnp.dot(nkv, wukv_ref[...], preferred_element_type=F32)
    t = kpe_ref[...].astype(F32) * csa_ref[...]
    kpe = jnp.where(lane < MLA_ROPE, t + pltpu.roll(t, MLA_ROPE, axis=1), 0.0).astype(BF16)
    for h in range(MLA_HEADS):
        ka_ref[:, h * MLA_DK_PAD:h * MLA_DK_PAD + LANES] = kvf[:, h * LANES:(h + 1) * LANES].astype(BF16)
        ka_ref[:, h * MLA_DK_PAD + LANES:(h + 1) * MLA_DK_PAD] = kpe
    va_ref[...] = kvf[:, MLA_HEADS * MLA_NOPE:].astype(BF16)

    cosb = cosb_ref[...]
    sinb = sinb_ref[...]
    first = (lane % (GQA_HEAD_DIM // 2)) < (GQA_HEAD_DIM // 4)

    def rope_b(x):
        rot = jnp.where(first, pltpu.roll(x, LANES - GQA_HEAD_DIM // 4, axis=1),
                        pltpu.roll(x, GQA_HEAD_DIM // 4, axis=1))
        return x * cosb + rot * sinb

    gqn = gqn_ref[...]
    gkn = gkn_ref[...]
    for h in range(GQA_HEADS):
        x = bq_ref[:, h * LANES:(h + 1) * LANES].astype(F32)
        qb_ref[:, h * LANES:(h + 1) * LANES] = (rope_b(_rms(x, gqn)) * GQA_SCALE).astype(BF16)
    for h in range(GQA_KV_HEADS):
        x = bk_ref[:, h * LANES:(h + 1) * LANES].astype(F32)
        kb_ref[:, h * LANES:(h + 1) * LANES] = rope_b(_rms(x, gkn)).astype(BF16)


def _prep(p, pk, tables, w_uq_x, w_ukv_r, g_qa, g_kva, g_qn, g_kn):
    n = p.shape[0]
    tm = _pick(n, (512, 256))
    n_pos = SEQ // tm
    n_lat_tiles = (n // (SEQ + CTX_LEN)) * n_pos

    def tab(i):
        return (jnp.where(i < n_lat_tiles, i % n_pos, n_pos), 0)

    def col(width, off):
        return pl.BlockSpec((tm, width), lambda i: (i, off // width))

    def whole(a):
        return pl.BlockSpec(a.shape, lambda i: (0,) * a.ndim)

    tab_spec = pl.BlockSpec((tm, LANES), tab)
    cosa, sina, csa, cosb, sinb = tables
    outs = [(MLA_HEADS * MLA_DK_PAD, "qa"), (MLA_HEADS * MLA_DK_PAD, "ka"), (MLA_HEADS * MLA_V, "va"),
            (GQA_HEADS * GQA_HEAD_DIM, "qb"), (GQA_KV_HEADS * GQA_HEAD_DIM, "kb")]
    return pl.pallas_call(
        _prep_kernel,
        grid=(n // tm,),
        in_specs=[col(512, P_CQ), col(512, P_CKV), col(LANES, 0), col(512, P_B), col(256, P_B + 512),
                  tab_spec, tab_spec, tab_spec, tab_spec, tab_spec,
                  whole(w_uq_x), whole(w_ukv_r), whole(g_qa), whole(g_kva), whole(g_qn), whole(g_kn)],
        out_specs=[pl.BlockSpec((tm, w), lambda i: (i, 0)) for w, _ in outs],
        out_shape=[jax.ShapeDtypeStruct((n, w), BF16) for w, _ in outs],
        compiler_params=_params("arbitrary"),
        name="attn_prep",
    )(p, p, pk, p, p, cosa, sina, csa, cosb, sinb, w_uq_x, w_ukv_r, g_qa, g_kva, g_qn, g_kn)


def _attn_kernel(*refs, n_seg, n_groups, rep, dk, dv, scale):
    q_ref = refs[0]
    segs = [(refs[1 + 2 * s], refs[2 + 2 * s]) for s in range(n_seg)]
    o_ref = refs[1 + 2 * n_seg]
    nt = (((1,), (1,)), ((), ()))
    for g in range(n_groups):
        ks = [k_ref[:, g * dk:(g + 1) * dk] for k_ref, _ in segs]
        vs = []
        for _, v_ref in segs:
            one = (lax.broadcasted_iota(jnp.int32, (v_ref.shape[0], LANES), 1) == 0).astype(BF16)
            vs.append(jnp.concatenate([v_ref[:, g * dv:(g + 1) * dv], one], axis=1))
        for r in range(rep):
            hq = g * rep + r
            q = q_ref[:, hq * dk:(hq + 1) * dk]
            ss = [lax.dot_general(q, k, nt, preferred_element_type=F32) for k in ks]
            if scale != 1.0:
                ss = [s * scale for s in ss]
            m = functools.reduce(jnp.maximum, [jnp.max(s, axis=-1, keepdims=True) for s in ss])
            acc = None
            for s, v in zip(ss, vs):
                a = jnp.dot(jnp.exp(s - m).astype(BF16), v, preferred_element_type=F32)
                acc = a if acc is None else acc + a
            o_ref[:, hq * dv:(hq + 1) * dv] = (acc[:, :dv] / acc[:, dv:dv + 1]).astype(o_ref.dtype)


def _attention(q, q_off, segs, bsz, n_q_per_batch, q_row0, *, n_groups, rep, dk, dv, scale, name):
    tq = min(512, n_q_per_batch)
    nq = n_q_per_batch // tq
    qw = n_groups * rep * dk
    ow = n_groups * rep * dv

    def q_map(b, i):
        return ((q_row0 + b * n_q_per_batch) // tq + i, q_off // qw)

    in_specs = [pl.BlockSpec((tq, qw), q_map)]
    args = [q]
    for k_arr, k_off, v_arr, v_off, length, row0 in segs:
        kw, vw = n_groups * dk, n_groups * dv
        in_specs.append(pl.BlockSpec((length, kw), functools.partial(
            lambda b, i, length, row0, blk: (row0 // length + b, blk), length=length, row0=row0, blk=k_off // kw)))
        in_specs.append(pl.BlockSpec((length, vw), functools.partial(
            lambda b, i, length, row0, blk: (row0 // length + b, blk), length=length, row0=row0, blk=v_off // vw)))
        args += [k_arr, v_arr]
    return pl.pallas_call(
        functools.partial(_attn_kernel, n_seg=len(segs), n_groups=n_groups, rep=rep, dk=dk, dv=dv, scale=scale),
        grid=(bsz, nq),
        in_specs=in_specs,
        out_specs=pl.BlockSpec((tq, ow), lambda b, i: (b * nq + i, 0)),
        out_shape=jax.ShapeDtypeStruct((bsz * n_q_per_batch, ow), BF16),
        compiler_params=_params("arbitrary", "arbitrary"),
        name=name,
    )(*args)


def _na_kernel(q_ref, k_ref, v_ref, kc_ref, vc_ref, bias_ref, o_ref):
    j = pl.program_id(0)
    k_rows = NA_K_ROWS * GRID_W
    start = jnp.clip(NA_Q_ROWS * j - NA_WIN_H // 2, 0, SEQ // GRID_W - NA_K_ROWS) * GRID_W
    start = pl.multiple_of(start, 256)
    hd = NA_HEAD_DIM

    def with_ones(v):
        one = (lax.broadcasted_iota(jnp.int32, (v.shape[0], LANES), 1) == 0).astype(BF16)
        return jnp.concatenate([v, one], axis=1)

    for h in range(NA_HEADS):
        cols = slice(h * hd, (h + 1) * hd)
        q = q_ref[:, cols]
        s1 = lax.dot_general(q, k_ref[pl.ds(start, k_rows), cols], NT_DIMS, preferred_element_type=F32) * NA_SCALE
        s1 = s1 + bias_ref[0, h]
        s2 = lax.dot_general(q, kc_ref[:, cols], NT_DIMS, preferred_element_type=F32) * NA_SCALE
        m = jnp.maximum(jnp.max(s1, axis=-1, keepdims=True), jnp.max(s2, axis=-1, keepdims=True))
        acc = jnp.dot(jnp.exp(s1 - m).astype(BF16), with_ones(v_ref[pl.ds(start, k_rows), cols]),
                      preferred_element_type=F32)
        acc = acc + jnp.dot(jnp.exp(s2 - m).astype(BF16), with_ones(vc_ref[:, cols]), preferred_element_type=F32)
        o_ref[:, cols] = (acc[:, :hd] / acc[:, hd:hd + 1]).astype(o_ref.dtype)


def _na_bias_kernel(tp_ref, o_ref):
    t = pl.program_id(0)
    rows = SEQ // GRID_W
    k0 = jnp.clip(NA_Q_ROWS * t - NA_WIN_H // 2, 0, rows - NA_K_ROWS)
    left = lax.broadcasted_iota(jnp.int32, (GRID_W, 2 * GRID_W), 1) < GRID_W
    neg = -jnp.inf
    for a in range(NA_Q_ROWS):
        rq = NA_Q_ROWS * t + a
        r0 = jnp.clip(rq - NA_WIN_H // 2, 0, rows - NA_WIN_H)
        for b2 in range(NA_K_ROWS // 2):
            rk = k0 + 2 * b2
            ok_l = (rk >= r0) & (rk < r0 + NA_WIN_H)
            ok_r = (rk + 1 >= r0) & (rk + 1 < r0 + NA_WIN_H)
            pair = tp_ref[0, jnp.clip(rk - rq + NA_WIN_H, 0, 2 * NA_WIN_H - 1)]
            keep = jnp.where(left, ok_l.astype(jnp.int32), ok_r.astype(jnp.int32)) > 0
            o_ref[0, 0, a * GRID_W:(a + 1) * GRID_W, b2 * 2 * GRID_W:(b2 + 1) * 2 * GRID_W] = jnp.where(keep, pair, neg)


def _na_bias(rpb):
    n_dr = 2 * NA_WIN_H - 1
    cols = jnp.arange(GRID_W)
    c0 = jnp.clip(cols - NA_WIN_W // 2, 0, GRID_W - NA_WIN_W)
    col_ok = (cols[None, :] >= c0[:, None]) & (cols[None, :] < c0[:, None] + NA_WIN_W)
    dc = jnp.clip(cols[None, :] - cols[:, None] + NA_WIN_W - 1, 0, 2 * NA_WIN_W - 2)
    oh_c = (dc[:, :, None] == jnp.arange(2 * NA_WIN_W - 1)).astype(F32)
    by_col = jnp.einsum('hdc,qkc->hdqk', rpb.astype(F32), oh_c, precision=lax.Precision.HIGHEST)
    by_col = jnp.where(col_ok[None, None], by_col, -jnp.inf)
    zero = jnp.zeros((NA_HEADS, 1, GRID_W, GRID_W), F32)
    padded = jnp.concatenate([zero, by_col, zero], axis=1)
    pairs = jnp.concatenate([padded[:, :n_dr + 1], padded[:, 1:]], axis=-1)
    n_tiles = SEQ // GRID_W // NA_Q_ROWS
    tq, tk = NA_Q_ROWS * GRID_W, NA_K_ROWS * GRID_W
    return pl.pallas_call(
        _na_bias_kernel,
        grid=(n_tiles, NA_HEADS),
        in_specs=[pl.BlockSpec((1, n_dr + 1, GRID_W, 2 * GRID_W), lambda t, h: (h, 0, 0, 0))],
        out_specs=pl.BlockSpec((1, 1, tq, tk), lambda t, h: (t, h, 0, 0)),
        out_shape=jax.ShapeDtypeStruct((n_tiles, NA_HEADS, tq, tk), F32),
        compiler_params=_params("arbitrary", "arbitrary"),
        name="na_bias",
    )(pairs)


def _na_attention(p, bias, bsz, n_lat):
    tq = NA_Q_ROWS * GRID_W
    n_tiles = SEQ // tq
    w = NA_HEADS * NA_HEAD_DIM
    qb, kb, vb = P_C // w, (P_C + w) // w, (P_C + 2 * w) // w
    return pl.pallas_call(
        _na_kernel,
        grid=(n_tiles, bsz),
        in_specs=[
            pl.BlockSpec((tq, w), lambda j, b: (b * n_tiles + j, qb)),
            pl.BlockSpec((SEQ, w), lambda j, b: (b, kb)),
            pl.BlockSpec((SEQ, w), lambda j, b: (b, vb)),
            pl.BlockSpec((CTX_LEN, w), lambda j, b: (n_lat // CTX_LEN + b, kb)),
            pl.BlockSpec((CTX_LEN, w), lambda j, b: (n_lat // CTX_LEN + b, vb)),
            pl.BlockSpec((1, NA_HEADS, tq, NA_K_ROWS * GRID_W), lambda j, b: (j, 0, 0, 0)),
        ],
        out_specs=pl.BlockSpec((tq, w), lambda j, b: (b * n_tiles + j, 0)),
        out_shape=jax.ShapeDtypeStruct((n_lat, w), BF16),
        compiler_params=_params("arbitrary", "arbitrary"),
        name="na_attention",
    )(p, p, p, p, p, bias)


def _conv_kernel(b_ref, c_ref, u_ref, w_ref, o_ref):
    length = c_ref.shape[0]
    z = c_ref[...].astype(F32) * u_ref[...].astype(F32)
    row = lax.broadcasted_iota(jnp.int32, z.shape, 0)
    z_prev = jnp.where(row == 0, 0.0, pltpu.roll(z, 1, axis=0))
    z_next = jnp.where(row == length - 1, 0.0, pltpu.roll(z, length - 1, axis=0))
    conv = z_prev * w_ref[0:1, :] + z * w_ref[1:2, :] + z_next * w_ref[2:3, :]
    o_ref[...] = (b_ref[...].astype(F32) * conv).astype(o_ref.dtype)


def _short_conv(p, w_conv8, bsz, length, row0):
    blk = P_D // CONV_CH

    def rows(b):
        return row0 // length + b

    return pl.pallas_call(
        _conv_kernel,
        grid=(bsz,),
        in_specs=[
            pl.BlockSpec((length, CONV_CH), lambda b: (rows(b), blk)),
            pl.BlockSpec((length, CONV_CH), lambda b: (rows(b), blk + 1)),
            pl.BlockSpec((length, CONV_CH), lambda b: (rows(b), blk + 2)),
            pl.BlockSpec((8, CONV_CH), lambda b: (0, 0)),
        ],
        out_specs=pl.BlockSpec((length, CONV_CH), lambda b: (b, 0)),
        out_shape=jax.ShapeDtypeStruct((bsz * length, CONV_CH), BF16),
        compiler_params=_params("arbitrary"),
        name="short_conv",
    )(p, p, p, w_conv8)


MERGE_CHUNK = 512


def _merge_kernel(ya_ref, yb_ref, yc_ref, yd_ref, g0_ref, g1_ref, g2_ref, g3_ref, wb_ref, o_ref):
    pairs = ((ya_ref, g0_ref), (yb_ref, g1_ref), (yc_ref, g2_ref), (yd_ref, g3_ref))
    for c in range(D_MODEL // MERGE_CHUNK):
        cols = slice(c * MERGE_CHUNK, (c + 1) * MERGE_CHUNK)
        acc = None
        for k, (y_ref, g_ref) in enumerate(pairs):
            proj = jnp.dot(y_ref[...], wb_ref[0, k, :, cols], preferred_element_type=F32)
            gate = 0.5 * jnp.tanh(0.5 * g_ref[:, cols].astype(F32)) + 0.5
            acc = gate * proj if acc is None else acc + gate * proj
        o_ref[:, cols] = acc.astype(o_ref.dtype)


def _merge(ys, p, w_branch, layer, n_act):
    tm = _pick(n_act, (512, 256))
    g_specs = [pl.BlockSpec((tm, D_MODEL), functools.partial(
        lambda i, k: (i, P_G // D_MODEL + k), k=k)) for k in range(N_BRANCHES)]
    return pl.pallas_call(
        _merge_kernel,
        grid=(n_act // tm,),
        in_specs=[pl.BlockSpec((tm, BRANCH_WIDTH), lambda i: (i, 0))] * N_BRANCHES + g_specs + [
            pl.BlockSpec((1, N_BRANCHES, BRANCH_WIDTH, D_MODEL), lambda i: (layer, 0, 0, 0))],
        out_specs=pl.BlockSpec((tm, D_MODEL), lambda i: (i, 0)),
        out_shape=jax.ShapeDtypeStruct((n_act, D_MODEL), BF16),
        compiler_params=_params("arbitrary"),
        name="branch_merge",
    )(*ys, p, p, p, p, w_branch)


R_E, R_W, R_RANK = 0, 2, 4


def _router_kernel(x_ref, g_ref, sh_ref, sc_ref, wr_ref, br_ref, h_ref, route_ref, counts_ref, run_ref,
                   *, tm, bsz):
    step = pl.program_id(0)
    r = _mod_row(step, tm, bsz)
    h = _rms(x_ref[...], g_ref[...]) * (1.0 + sc_ref[pl.ds(r, 1), :]) + sh_ref[pl.ds(r, 1), :]
    h_ref[...] = h
    logits = jnp.dot(h, wr_ref[...], precision=lax.Precision.HIGHEST, preferred_element_type=F32)
    bias = br_ref[...]
    lane = lax.broadcasted_iota(jnp.int32, logits.shape, 1)
    lane_f = lane.astype(F32)
    neg = -jnp.inf

    def rmax(x, mask):
        return jnp.max(jnp.where(mask, x, neg), axis=-1, keepdims=True)

    def first_at(x, val, mask):
        return jnp.min(jnp.where(mask & (x == val), lane_f, 2.0 * LANES), axis=-1, keepdims=True)

    def pick(x, idx):
        return jnp.sum(jnp.where(lane_f == idx, x, 0.0), axis=-1, keepdims=True)

    gmask = lane < N_GROUPS
    ge = jnp.where(gmask, jnp.exp(logits - rmax(logits, gmask)), 0.0)
    g_prob = ge / jnp.sum(ge, axis=-1, keepdims=True)
    g_score = g_prob + bias
    g_sel = first_at(g_score, rmax(g_score, gmask), gmask)
    p_group = pick(g_prob, g_sel)

    group_of_lane = lax.shift_right_arithmetic(lane - N_EXPERTS, 3).astype(F32)
    emask = (lane >= N_EXPERTS) & (group_of_lane == g_sel)
    ee = jnp.where(emask, jnp.exp(logits - rmax(logits, emask)), 0.0)
    e_prob = ee / jnp.sum(ee, axis=-1, keepdims=True)
    e_score = e_prob + bias
    i1 = first_at(e_score, rmax(e_score, emask), emask)
    mask2 = emask & (lane_f != i1)
    i2 = first_at(e_score, rmax(e_score, mask2), mask2)
    p1 = pick(e_prob, i1)
    p2 = pick(e_prob, i2)
    w1 = p_group * p1 / (p1 + p2)
    w2 = p_group * p2 / (p1 + p2)
    e1 = i1 - N_EXPERTS
    e2 = i2 - N_EXPERTS

    @pl.when(step == 0)
    def _():
        run_ref[...] = jnp.zeros_like(run_ref)

    oh1 = lane_f == e1
    oh2 = lane_f == e2
    both = oh1.astype(F32) + oh2.astype(F32)
    tri = (lax.broadcasted_iota(jnp.int32, (tm, tm), 1) < lax.broadcasted_iota(jnp.int32, (tm, tm), 0)).astype(BF16)
    before = jnp.dot(tri, both.astype(BF16), preferred_element_type=F32) + run_ref[0:1, :]
    rank1 = jnp.sum(jnp.where(oh1, before, 0.0), axis=-1, keepdims=True)
    rank2 = jnp.sum(jnp.where(oh2, before + oh1.astype(F32), 0.0), axis=-1, keepdims=True)
    run_ref[...] = run_ref[...] + jnp.sum(both, axis=0, keepdims=True)
    counts_ref[...] = run_ref[...]

    vals = (e1, e2, w1, w2, rank1, rank2)
    out = jnp.zeros_like(logits)
    for k, v in enumerate(vals):
        out = jnp.where(lane == k, v, out)
    route_ref[...] = out


def _ffn_input_and_route(x1, gain, mod, w_route, b_route, bsz):
    n = x1.shape[0]
    tm = _pick(n, (512, 256))
    rows = mod.shape[0]
    return pl.pallas_call(
        functools.partial(_router_kernel, tm=tm, bsz=bsz),
        grid=(n // tm,),
        in_specs=[
            pl.BlockSpec((tm, D_MODEL), lambda i: (i, 0)),
            pl.BlockSpec((1, D_MODEL), lambda i: (0, 0)),
            pl.BlockSpec((rows, D_MODEL), lambda i: (0, 3)),
            pl.BlockSpec((rows, D_MODEL), lambda i: (0, 4)),
            pl.BlockSpec((D_MODEL, LANES), lambda i: (0, 0)),
            pl.BlockSpec((1, LANES), lambda i: (0, 0)),
        ],
        out_specs=[pl.BlockSpec((tm, D_MODEL), lambda i: (i, 0)),
                   pl.BlockSpec((tm, LANES), lambda i: (i, 0)),
                   pl.BlockSpec((8, LANES), lambda i: (0, 0))],
        out_shape=[jax.ShapeDtypeStruct((n, D_MODEL), F32),
                   jax.ShapeDtypeStruct((n, LANES), F32),
                   jax.ShapeDtypeStruct((8, LANES), F32)],
        scratch_shapes=[pltpu.VMEM((8, LANES), F32)],
        compiler_params=_params("arbitrary"),
        name="ffn_input_route",
    )(x1, gain.reshape(1, D_MODEL), mod, mod, w_route, b_route)


def _dispatch_indices(route, counts_row, n_tok):
    counts = counts_row[0, :N_EXPERTS].astype(jnp.int32)
    padded = (counts + MOE_BLOCK - 1) // MOE_BLOCK * MOE_BLOCK
    pad_end = jnp.cumsum(padded)
    pad_start = pad_end - padded
    eid = route[:, R_E:R_E + TOP_K].astype(jnp.int32)
    rank = route[:, R_RANK:R_RANK + TOP_K].astype(jnp.int32)
    onehot = eid[:, :, None] == jnp.arange(N_EXPERTS, dtype=jnp.int32)[None, None, :]
    pos = (jnp.sum(jnp.where(onehot, pad_start[None, None, :], 0), axis=-1) + rank).reshape(-1)
    n_blocks = (n_tok * TOP_K + N_EXPERTS * (MOE_BLOCK - 1)) // MOE_BLOCK
    blk_start = jnp.arange(n_blocks, dtype=jnp.int32) * MOE_BLOCK
    blk_e = jnp.minimum(jnp.sum(pad_end[None, :] <= blk_start[:, None], axis=1), N_EXPERTS - 1).astype(jnp.int32)
    n_used = (pad_end[-1:] // MOE_BLOCK).astype(jnp.int32)
    last_blk = jnp.where(counts > 0, pad_end // MOE_BLOCK - 1, -1).astype(jnp.int32)
    blk = jnp.arange(n_blocks, dtype=jnp.int32)
    prev_e = jnp.concatenate([jnp.full((1,), -1, jnp.int32), blk_e[:-1]])
    first = (blk < n_used) & (blk_e != prev_e)
    ordinal = jnp.cumsum(first.astype(jnp.int32)) - 1
    n_first = jnp.sum(first.astype(jnp.int32))
    k = jnp.arange(N_EXPERTS, dtype=jnp.int32)
    expert_at = jnp.sum(jnp.where(first[None, :] & (ordinal[None, :] == k[:, None]), blk_e[None, :], 0), axis=1)

    def expert_after(d):
        target = ordinal + d
        e = jnp.sum(jnp.where(k[None, :] == target[:, None], expert_at[None, :], 0), axis=1)
        return jnp.where(target < n_first, e, -1).astype(jnp.int32)

    sched = (blk_e, n_used, first.astype(jnp.int32), (ordinal % MOE_W_SLOTS).astype(jnp.int32),
             expert_after(1), expert_after(2))
    return sched, last_blk, pos.astype(jnp.int32), n_blocks


def _row_copy(src, src_row, dst, dst_row, sem):
    return pltpu.make_async_copy(src.at[pl.ds(src_row, 1), :], dst.at[pl.ds(dst_row, 1), :], sem)


DISPATCH_CHUNK = 512
DMA_UNROLL = 8


def _dispatch_kernel(pos_ref, last_ref, nused_ref, h_ref, xs_hbm, zbuf, sem, zsem):
    base = pl.program_id(0) * DISPATCH_CHUNK
    n_blocks = xs_hbm.shape[0] // MOE_BLOCK

    @pl.when(pl.program_id(0) == 0)
    def _():
        zbuf[...] = jnp.zeros_like(zbuf)

        def fill(b):
            return pltpu.make_async_copy(zbuf, xs_hbm.at[pl.ds(pl.multiple_of(b * MOE_BLOCK, MOE_BLOCK), MOE_BLOCK), :],
                                         zsem.at[0])

        def each(fn):
            def expert(e, carry):
                @pl.when(last_ref[e] >= 0)
                def _():
                    fn(fill(last_ref[e]))
                return carry

            def unused(b, carry):
                fn(fill(b))
                return carry

            lax.fori_loop(0, N_EXPERTS, expert, 0)
            lax.fori_loop(nused_ref[0], n_blocks, unused, 0)

        each(lambda c: c.start())
        each(lambda c: c.wait())

    def body(r, carry):
        for k in range(TOP_K):
            _row_copy(h_ref, r, xs_hbm, pos_ref[(base + r) * TOP_K + k], sem.at[0]).start(priority=k % 2)
        return carry

    lax.fori_loop(0, DISPATCH_CHUNK, body, 0, unroll=DMA_UNROLL)
    for k in range(TOP_K):
        pltpu.make_async_copy(h_ref, xs_hbm.at[pl.ds(0, DISPATCH_CHUNK), :], sem.at[0]).wait()


def _dispatch(h2, pos, last_blk, n_used, n_rows):
    n, width = h2.shape
    assert n % DISPATCH_CHUNK == 0
    grid_spec = pltpu.PrefetchScalarGridSpec(
        num_scalar_prefetch=3,
        grid=(n // DISPATCH_CHUNK,),
        in_specs=[pl.BlockSpec((DISPATCH_CHUNK, width), lambda i, *_: (i, 0))],
        out_specs=pl.BlockSpec(memory_space=pl.ANY),
        scratch_shapes=[pltpu.VMEM((MOE_BLOCK, width), h2.dtype), pltpu.SemaphoreType.DMA((1,)),
                        pltpu.SemaphoreType.DMA((1,))],
    )
    return pl.pallas_call(
        _dispatch_kernel,
        grid_spec=grid_spec,
        out_shape=jax.ShapeDtypeStruct((n_rows, width), h2.dtype),
        compiler_params=_params("arbitrary"),
        name="moe_dispatch",
    )(pos, last_blk, n_used, h2)


def _moe_kernel(be_ref, nused_ref, first_ref, slot_ref, nxt1_ref, nxt2_ref, x_ref, wg_hbm, wu_hbm, wd_hbm, o_ref,
                wgf, wuf, wdf, wgb, wub, wdb, sem, *, layer):
    i = pl.program_id(0)

    def w_copies(e, s):
        return (pltpu.make_async_copy(wg_hbm.at[layer, e], wgf.at[s], sem.at[s]),
                pltpu.make_async_copy(wu_hbm.at[layer, e], wuf.at[s], sem.at[s]),
                pltpu.make_async_copy(wd_hbm.at[layer, e], wdf.at[s], sem.at[s]))

    def start(e, s):
        for c in w_copies(e, s):
            c.start(priority=BULK_DMA_PRIORITY)

    @pl.when(first_ref[i] == 1)
    def _():
        s = slot_ref[i]

        @pl.when(i == 0)
        def _():
            start(be_ref[i], s)

            @pl.when(nxt1_ref[i] >= 0)
            def _():
                start(nxt1_ref[i], (s + 1) % MOE_W_SLOTS)

        @pl.when(nxt2_ref[i] >= 0)
        def _():
            start(nxt2_ref[i], (s + 2) % MOE_W_SLOTS)

        for c in w_copies(be_ref[i], s):
            c.wait()
        wgb[...] = wgf[s].astype(BF16)
        wub[...] = wuf[s].astype(BF16)
        wdb[...] = wdf[s].astype(BF16)

    @pl.when(i < nused_ref[0])
    def _():
        x = x_ref[...].astype(BF16)
        gate = jnp.dot(x, wgb[...], preferred_element_type=F32)
        up = jnp.dot(x, wub[...], preferred_element_type=F32)
        hid = (gate * jax.nn.sigmoid(gate) * up).astype(BF16)
        o_ref[...] = jnp.dot(hid, wdb[...], preferred_element_type=F32)

    @pl.when(i >= nused_ref[0])
    def _():
        o_ref[...] = jnp.zeros_like(o_ref)


def _moe(xs, sched, w_gate, w_up, w_down, layer):
    n_blocks = sched[0].shape[0]
    grid_spec = pltpu.PrefetchScalarGridSpec(
        num_scalar_prefetch=len(sched),
        grid=(n_blocks,),
        in_specs=[pl.BlockSpec((MOE_BLOCK, D_MODEL), lambda i, *_: (i, 0))] + [pl.BlockSpec(memory_space=pl.ANY)] * 3,
        out_specs=pl.BlockSpec((MOE_BLOCK, D_MODEL), lambda i, *_: (i, 0)),
        scratch_shapes=[
            pltpu.VMEM((MOE_W_SLOTS, D_MODEL, EXPERT_HIDDEN), F32),
            pltpu.VMEM((MOE_W_SLOTS, D_MODEL, EXPERT_HIDDEN), F32),
            pltpu.VMEM((MOE_W_SLOTS, EXPERT_HIDDEN, D_MODEL), F32),
            pltpu.VMEM((D_MODEL, EXPERT_HIDDEN), BF16),
            pltpu.VMEM((D_MODEL, EXPERT_HIDDEN), BF16),
            pltpu.VMEM((EXPERT_HIDDEN, D_MODEL), BF16),
            pltpu.SemaphoreType.DMA((MOE_W_SLOTS,)),
        ],
    )
    return pl.pallas_call(
        functools.partial(_moe_kernel, layer=layer),
        grid_spec=grid_spec,
        out_shape=jax.ShapeDtypeStruct((n_blocks * MOE_BLOCK, D_MODEL), F32),
        compiler_params=_params("arbitrary"),
        name="moe_experts",
    )(*sched, xs, w_gate, w_up, w_down)


def _combine_kernel(pos_ref, ys_hbm, x_ref, route_ref, ga_ref, gf_ref, o_ref, rbuf, sem, *, tm, bsz, final):
    i = pl.program_id(0)
    n = pl.num_programs(0)
    slot = i % 2

    def gather(blk, s):
        def body(r, carry):
            for k in range(TOP_K):
                _row_copy(ys_hbm, pos_ref[(blk * tm + r) * TOP_K + k], rbuf.at[s], k * tm + r,
                          sem.at[s]).start(priority=k % 2)
            return carry
        lax.fori_loop(0, tm, body, 0, unroll=DMA_UNROLL)

    @pl.when(i == 0)
    def _():
        gather(0, 0)

    @pl.when(i + 1 < n)
    def _():
        gather(i + 1, 1 - slot)

    pltpu.make_async_copy(ys_hbm.at[pl.ds(0, TOP_K * tm), :], rbuf.at[slot], sem.at[slot]).wait()
    y = rbuf[slot, 0:tm, :] * route_ref[:, R_W:R_W + 1] + rbuf[slot, tm:2 * tm, :] * route_ref[:, R_W + 1:R_W + 2]
    r = _mod_row(i, tm, bsz)
    x = x_ref[...] + ga_ref[pl.ds(r, 1), :] * y
    if final:
        x = _rms(x, gf_ref[...])
    o_ref[...] = x


def _combine(ys, pos, x1, route, mod, g_final, bsz, final):
    n = x1.shape[0]
    tm = 128
    rows = mod.shape[0]
    grid_spec = pltpu.PrefetchScalarGridSpec(
        num_scalar_prefetch=1,
        grid=(n // tm,),
        in_specs=[
            pl.BlockSpec(memory_space=pl.ANY),
            pl.BlockSpec((tm, D_MODEL), lambda i, pos: (i, 0)),
            pl.BlockSpec((tm, LANES), lambda i, pos: (i, 0)),
            pl.BlockSpec((rows, D_MODEL), lambda i, pos: (0, 5)),
            pl.BlockSpec((1, D_MODEL), lambda i, pos: (0, 0)),
        ],
        out_specs=pl.BlockSpec((tm, D_MODEL), lambda i, pos: (i, 0)),
        scratch_shapes=[pltpu.VMEM((2, TOP_K * tm, D_MODEL), F32), pltpu.SemaphoreType.DMA((2,))],
    )
    return pl.pallas_call(
        functools.partial(_combine_kernel, tm=tm, bsz=bsz, final=final),
        grid_spec=grid_spec,
        out_shape=jax.ShapeDtypeStruct((n, D_MODEL), F32),
        compiler_params=_params("arbitrary"),
        name="moe_combine",
    )(pos, ys, x1, route, mod, g_final.reshape(1, D_MODEL))


def _rotate_half_cols(w, quarter):
    r1, r2, c1, c2 = (w[..., k * quarter:(k + 1) * quarter] for k in range(4))
    return jnp.concatenate([-r2, r1, -c2, c1], axis=-1)


def _kpe_rows(wt):
    kpe = jnp.swapaxes(wt[:, MLA_Q_LORA + MLA_KV_LORA:A_COLS, :], 1, 2)
    both = jnp.concatenate([kpe, _rotate_half_cols(kpe, MLA_ROPE // 4)], axis=-1)
    return jnp.swapaxes(both, 1, 2).astype(BF16)


def _relayout_w_uq(w_uq):
    w = w_uq.reshape(MLA_Q_LORA, MLA_HEADS, MLA_NOPE + MLA_ROPE)
    nope, pe = w[..., :MLA_NOPE], w[..., MLA_NOPE:]
    z = jnp.zeros_like(pe)
    out = jnp.concatenate([nope, pe, z, _rotate_half_cols(pe, MLA_ROPE // 4), z], axis=-1)
    return out.reshape(MLA_Q_LORA, MLA_HEADS * 3 * LANES).astype(BF16)


def _relayout_w_ukv(w_ukv):
    w = w_ukv.reshape(MLA_KV_LORA, MLA_HEADS, MLA_NOPE + MLA_V)
    return jnp.concatenate([w[..., :MLA_NOPE].reshape(MLA_KV_LORA, -1),
                            w[..., MLA_NOPE:].reshape(MLA_KV_LORA, -1)], axis=-1).astype(BF16)


def _rope_tables(tm):
    t = jnp.arange(SEQ)
    rows = (t // GRID_W).astype(F32)
    cols = (t % GRID_W).astype(F32)

    def cs(rot_dim):
        quarter = rot_dim // 4
        inv_freq = ROPE_THETA ** (-jnp.arange(quarter, dtype=F32) / quarter)
        ang_r = rows[:, None] * inv_freq[None, :]
        ang_c = cols[:, None] * inv_freq[None, :]
        cos = jnp.concatenate([jnp.cos(ang_r)] * 2 + [jnp.cos(ang_c)] * 2, axis=-1)
        sin = jnp.concatenate([jnp.sin(ang_r)] * 2 + [jnp.sin(ang_c)] * 2, axis=-1)
        return cos, sin

    def with_identity(tab, ident):
        return jnp.concatenate([tab, jnp.broadcast_to(ident, (tm, LANES))], axis=0)

    cos_a, sin_a = cs(MLA_ROPE)
    zeros = jnp.zeros((SEQ, LANES - MLA_ROPE), F32)
    lane = jnp.arange(LANES)
    one_lo = (lane < MLA_ROPE).astype(F32)[None, :]
    cosa = with_identity(jnp.concatenate([cos_a, zeros], axis=1), one_lo)
    sina = with_identity(jnp.concatenate([sin_a, zeros], axis=1), jnp.zeros((1, LANES), F32))
    csa = with_identity(jnp.concatenate([cos_a, sin_a], axis=1), one_lo)
    cos_b, sin_b = cs(GQA_HEAD_DIM)
    sign = jnp.where((lane % (GQA_HEAD_DIM // 2)) < GQA_HEAD_DIM // 4, -1.0, 1.0).astype(F32)[None, :]
    cosb = with_identity(cos_b, jnp.ones((1, LANES), F32))
    sinb = with_identity(sin_b * sign, jnp.zeros((1, LANES), F32))
    return cosa, sina, csa, cosb, sinb


def kernel(x, c, ctx, c_ctx, w_mod, b_mod, g_mix, g_ffn, w_in, w_uq, g_qa, w_ukv, g_kva, g_qn, g_kn, rpb, w_conv,
           w_branch, w_o, w_group, b_group, w_router, b_router, w_gate_e, w_up_e, w_down_e, g_final):
    bsz = x.shape[0]
    n_lat = bsz * SEQ
    n_ctx = bsz * CTX_LEN
    n_all = n_lat + n_ctx
    mod_rows = -(-(bsz + 1) // 8) * 8

    x_all = jnp.concatenate([x.reshape(n_lat, D_MODEL), ctx.reshape(n_ctx, D_MODEL)], axis=0)
    c_all = jnp.concatenate([c, c_ctx[None, :], jnp.zeros((mod_rows - bsz - 1, D_MODEL), F32)], axis=0)
    mods = _modulation(c_all, w_mod, b_mod)
    tables = _rope_tables(_pick(n_all, (512, 256)))

    w_in_t = jnp.swapaxes(w_in, 1, 2)
    w_kpe_t = _kpe_rows(w_in_t)
    w_branch_b = w_branch.astype(BF16)
    w_o_b = w_o.astype(BF16)

    out = None
    for l in range(DEPTH):
        last = l == DEPTH - 1
        n_act = n_lat if last else n_all
        mod = mods[l]
        w_uq_x = _relayout_w_uq(w_uq[l])
        w_ukv_r = _relayout_w_ukv(w_ukv[l])
        w_route = jnp.concatenate([w_group[l], jnp.zeros((D_MODEL, N_EXPERTS - N_GROUPS), F32), w_router[l]], axis=1)
        b_route = jnp.concatenate([b_group[l], jnp.zeros((N_EXPERTS - N_GROUPS,), F32), b_router[l]])[None, :]
        w_conv8 = jnp.concatenate([w_conv[l], jnp.zeros((8 - CONV_WIDTH, CONV_CH), F32)], axis=0)

        h = _hmod(x_all, g_mix[l], mod, bsz)
        p = _in_proj(h, w_in_t, l)
        pk = _matmul_nt(h, w_kpe_t, l, BF16, "kpe_proj")
        qa, ka, va, qb, kb = _prep(p, pk, tables, w_uq_x, w_ukv_r, g_qa[l][None, :], g_kva[l][None, :],
                                   g_qn[l][None, :], g_kn[l][None, :])

        lat_ctx = lambda k_arr, k_off, v_arr, v_off: [(k_arr, k_off, v_arr, v_off, SEQ, 0),
                                                      (k_arr, k_off, v_arr, v_off, CTX_LEN, n_lat)]
        ctx_only = lambda k_arr, k_off, v_arr, v_off: [(k_arr, k_off, v_arr, v_off, CTX_LEN, n_lat)]
        mla = dict(n_groups=MLA_HEADS, rep=1, dk=MLA_DK_PAD, dv=MLA_V, scale=1.0)
        gqa = dict(n_groups=GQA_KV_HEADS, rep=GQA_HEADS // GQA_KV_HEADS, dk=GQA_HEAD_DIM, dv=GQA_HEAD_DIM, scale=1.0)
        nac = dict(n_groups=NA_HEADS, rep=1, dk=NA_HEAD_DIM, dv=NA_HEAD_DIM, scale=NA_SCALE)

        ys_lat = [
            _attention(qa, 0, lat_ctx(ka, 0, va, 0), bsz, SEQ, 0, name="mla_attention", **mla),
            _attention(qb, 0, lat_ctx(kb, 0, p, P_B + 768), bsz, SEQ, 0, name="gqa_attention", **gqa),
            _na_attention(p, _na_bias(rpb[l]), bsz, n_lat),
            _short_conv(p, w_conv8, bsz, SEQ, 0),
        ]
        if last:
            ys_all = ys_lat
        else:
            ys_ctx = [
                _attention(qa, 0, ctx_only(ka, 0, va, 0), bsz, CTX_LEN, n_lat, name="mla_attention_ctx", **mla),
                _attention(qb, 0, ctx_only(kb, 0, p, P_B + 768), bsz, CTX_LEN, n_lat,
                           name="gqa_attention_ctx", **gqa),
                _attention(p, P_C, ctx_only(p, P_C + 512, p, P_C + 1024), bsz, CTX_LEN, n_lat,
                           name="na_attention_ctx", **nac),
                _short_conv(p, w_conv8, bsz, CTX_LEN, n_lat),
            ]
            ys_all = [jnp.concatenate([a, b], axis=0) for a, b in zip(ys_lat, ys_ctx)]

        mixed = _merge(ys_all, p, w_branch_b, l, n_act)
        x1 = _out_proj_residual(mixed, w_o_b, l, x_all, mod, bsz)
        h2, route, counts = _ffn_input_and_route(x1, g_ffn[l], mod, w_route, b_route, bsz)
        sched, last_blk, pos, n_blocks = _dispatch_indices(route, counts, n_act)
        xs = _dispatch(h2, pos, last_blk, sched[1], n_blocks * MOE_BLOCK)
        ys = _moe(xs, sched, w_gate_e, w_up_e, w_down_e, l)
        x_all = _combine(ys, pos, x1, route, mod, g_final, bsz, last)
        out = x_all
    return out.reshape(bsz, SEQ, D_MODEL)
```

```python
import functools

import jax
import jax.numpy as jnp
from jax import lax
from jax.experimental import pallas as pl
from jax.experimental.pallas import tpu as pltpu

F32 = jnp.float32
BF16 = jnp.bfloat16

D_MODEL = 2048
SEQ = 2048
DEPTH = 2
CTX_LEN = 256
GRID_W = 64
ROPE_THETA = 10000.0
NORM_EPS = 1e-6
MOD_CHUNKS = 6

MLA_HEADS = 4
MLA_Q_LORA = 512
MLA_KV_LORA = 512
MLA_NOPE = 128
MLA_ROPE = 64
MLA_V = 128
MLA_SCALE = (MLA_NOPE + MLA_ROPE) ** -0.5
MLA_DK_PAD = 256

GQA_HEADS = 4
GQA_KV_HEADS = 2
GQA_HEAD_DIM = 128
GQA_SCALE = GQA_HEAD_DIM ** -0.5

NA_HEADS = 4
NA_HEAD_DIM = 128
NA_WIN_H = 8
NA_WIN_W = 16
NA_SCALE = NA_HEAD_DIM ** -0.5
NA_Q_ROWS = 8
NA_K_ROWS = 16

CONV_CH = 512
CONV_WIDTH = 3
BRANCH_WIDTH = 512
N_BRANCHES = 4

A_COLS = MLA_Q_LORA + MLA_KV_LORA + MLA_ROPE
B_COLS = (GQA_HEADS + 2 * GQA_KV_HEADS) * GQA_HEAD_DIM
C_COLS = 3 * NA_HEADS * NA_HEAD_DIM
D_COLS = 3 * CONV_CH
G_COLS = N_BRANCHES * D_MODEL

P_B = 0
P_C = P_B + B_COLS
P_D = P_C + C_COLS
P_G = P_D + D_COLS
P_CQ = P_G + G_COLS
P_CKV = P_CQ + MLA_Q_LORA
P_COLS = P_CKV + MLA_KV_LORA
IN_TILE = 1024

N_GROUPS = 8
EXPERTS_PER_GROUP = 8
N_EXPERTS = 64
TOP_K = 2
EXPERT_HIDDEN = 512
MOE_BLOCK = 128
MOE_W_SLOTS = 3

LANES = 128
BULK_DMA_PRIORITY = 1
VMEM_LIMIT = 56 * 1024 * 1024


def _params(*sem):
    return pltpu.CompilerParams(dimension_semantics=sem, vmem_limit_bytes=VMEM_LIMIT)


def _pick(n, cands):
    for c in cands:
        if n % c == 0:
            return c
    raise ValueError(f"no tile for {n}")


def _rms(x, g):
    return x * lax.rsqrt(jnp.mean(x * x, axis=-1, keepdims=True) + NORM_EPS) * g


def _mod_row(i, tm, bsz):
    return jnp.minimum(i * tm // SEQ, bsz)


def _mod_kernel(ct_ref, w_ref, b_ref, o_ref, *, n_vec):
    ct = ct_ref[...]
    st = ct * jax.nn.sigmoid(ct)
    w = w_ref[0]
    rows = [jnp.sum(w * st[:, r:r + 1], axis=0, keepdims=True) for r in range(n_vec)]
    rows += [jnp.zeros_like(rows[0])] * (o_ref.shape[1] - n_vec)
    o_ref[0] = jnp.concatenate(rows, axis=0) + b_ref[0]


def _modulation(c_all, n_vec, w_mod, b_mod):
    rows = c_all.shape[0]
    n = MOD_CHUNKS * D_MODEL
    tn = 1024
    return pl.pallas_call(
        functools.partial(_mod_kernel, n_vec=n_vec),
        grid=(DEPTH, n // tn),
        in_specs=[
            pl.BlockSpec((D_MODEL, rows), lambda l, j: (0, 0)),
            pl.BlockSpec((1, D_MODEL, tn), lambda l, j: (l, 0, j)),
            pl.BlockSpec((1, 1, tn), lambda l, j: (l, 0, j)),
        ],
        out_specs=pl.BlockSpec((1, rows, tn), lambda l, j: (l, 0, j)),
        out_shape=jax.ShapeDtypeStruct((DEPTH, rows, n), F32),
        compiler_params=_params("arbitrary", "arbitrary"),
        name="modulation",
    )(c_all.T, w_mod, b_mod.reshape(DEPTH, 1, n))


def _hmod_kernel(x_ref, g_ref, sh_ref, sc_ref, o_ref, *, tm, bsz):
    r = _mod_row(pl.program_id(0), tm, bsz)
    y = _rms(x_ref[...], g_ref[...])
    o_ref[...] = (y * (1.0 + sc_ref[pl.ds(r, 1), :]) + sh_ref[pl.ds(r, 1), :]).astype(o_ref.dtype)


def _hmod(x_all, gain, mod, bsz):
    n = x_all.shape[0]
    tm = _pick(n, (512, 256))
    rows = mod.shape[0]
    return pl.pallas_call(
        functools.partial(_hmod_kernel, tm=tm, bsz=bsz),
        grid=(n // tm,),
        in_specs=[
            pl.BlockSpec((tm, D_MODEL), lambda i: (i, 0)),
            pl.BlockSpec((1, D_MODEL), lambda i: (0, 0)),
            pl.BlockSpec((rows, D_MODEL), lambda i: (0, 0)),
            pl.BlockSpec((rows, D_MODEL), lambda i: (0, 1)),
        ],
        out_specs=pl.BlockSpec((tm, D_MODEL), lambda i: (i, 0)),
        out_shape=jax.ShapeDtypeStruct((n, D_MODEL), BF16),
        compiler_params=_params("arbitrary"),
        name="hmod",
    )(x_all, gain.reshape(1, D_MODEL), mod, mod)


NT_DIMS = (((1,), (1,)), ((), ()))


def _mm_nt_kernel(a_ref, w_ref, o_ref):
    o_ref[...] = lax.dot_general(a_ref[...], w_ref[0], NT_DIMS, preferred_element_type=F32).astype(o_ref.dtype)


def _matmul_nt(a, wt, layer, out_dtype, name):
    m, k = a.shape
    n = wt.shape[1]
    tm = _pick(m, (1024, 512, 256))
    return pl.pallas_call(
        _mm_nt_kernel,
        grid=(m // tm,),
        in_specs=[pl.BlockSpec((tm, k), lambda i: (i, 0)), pl.BlockSpec((1, n, k), lambda i: (layer, 0, 0))],
        out_specs=pl.BlockSpec((tm, n), lambda i: (i, 0)),
        out_shape=jax.ShapeDtypeStruct((m, n), out_dtype),
        compiler_params=_params("arbitrary"),
        name=name,
    )(a, wt)


def _in_proj_kernel(a_ref, wt_hbm, o_ref, wbuf, wbf, sem, *, layer):
    j = pl.program_id(0)
    n_tiles = pl.num_programs(0)
    n_shift = P_CQ // IN_TILE

    def w_copy(jj, slot):
        row = pl.multiple_of(jnp.where(jj < n_shift, A_COLS + jj * IN_TILE, 0), MLA_ROPE)
        return pltpu.make_async_copy(wt_hbm.at[layer, pl.ds(row, IN_TILE), :], wbuf.at[slot], sem.at[slot])

    @pl.when(pl.program_id(1) == 0)
    def _():
        @pl.when(j == 0)
        def _():
            w_copy(0, 0).start(priority=BULK_DMA_PRIORITY)

        @pl.when(j + 1 < n_tiles)
        def _():
            w_copy(j + 1, (j + 1) % 2).start(priority=BULK_DMA_PRIORITY)

        w_copy(j, j % 2).wait()
        wbf[...] = wbuf[j % 2].astype(BF16)

    o_ref[...] = lax.dot_general(a_ref[...], wbf[...], NT_DIMS, preferred_element_type=F32).astype(o_ref.dtype)


def _in_proj(h, wt, layer):
    m, k = h.shape
    tm = _pick(m, (1024, 512, 256))
    return pl.pallas_call(
        functools.partial(_in_proj_kernel, layer=layer),
        grid=(P_COLS // IN_TILE, m // tm),
        in_specs=[pl.BlockSpec((tm, k), lambda j, i: (i, 0)), pl.BlockSpec(memory_space=pl.ANY)],
        out_specs=pl.BlockSpec((tm, IN_TILE), lambda j, i: (i, j)),
        out_shape=jax.ShapeDtypeStruct((m, P_COLS), BF16),
        scratch_shapes=[pltpu.VMEM((2, IN_TILE, k), F32), pltpu.VMEM((IN_TILE, k), BF16),
                        pltpu.SemaphoreType.DMA((2,))],
        compiler_params=_params("arbitrary", "arbitrary"),
        name="in_proj",
    )(h, wt)


def _mm_res_kernel(a_ref, w_ref, x_ref, ga_ref, o_ref, *, tm, bsz):
    r = _mod_row(pl.program_id(1), tm, bsz)
    acc = jnp.dot(a_ref[...], w_ref[0], preferred_element_type=F32)
    o_ref[...] = x_ref[...] + ga_ref[pl.ds(r, 1), :] * acc


def _out_proj_residual(mixed, w_o, layer, x_all, mod, bsz):
    m = mixed.shape[0]
    tm = _pick(m, (1024, 512, 256))
    tn = 1024
    rows = mod.shape[0]
    ga_blk = 2 * D_MODEL // tn
    return pl.pallas_call(
        functools.partial(_mm_res_kernel, tm=tm, bsz=bsz),
        grid=(D_MODEL // tn, m // tm),
        in_specs=[
            pl.BlockSpec((tm, D_MODEL), lambda j, i: (i, 0)),
            pl.BlockSpec((1, D_MODEL, tn), lambda j, i: (layer, 0, j)),
            pl.BlockSpec((tm, tn), lambda j, i: (i, j)),
            pl.BlockSpec((rows, tn), lambda j, i: (0, ga_blk + j)),
        ],
        out_specs=pl.BlockSpec((tm, tn), lambda j, i: (i, j)),
        out_shape=jax.ShapeDtypeStruct((m, D_MODEL), F32),
        compiler_params=_params("arbitrary", "arbitrary"),
        name="out_proj_residual",
    )(mixed, w_o, x_all, mod)


def _prep_kernel(cq_ref, ckv_ref, kpe_ref, bq_ref, bk_ref,
                 cosa_ref, sina_ref, csa_ref, cosb_ref, sinb_ref,
                 wuq_ref, wukv_ref, gqa_ref, gkva_ref, gqn_ref, gkn_ref,
                 qa_ref, ka_ref, va_ref, qb_ref, kb_ref):
    tm = cq_ref.shape[0]
    lane = lax.broadcasted_iota(jnp.int32, (tm, LANES), 1)

    nq = _rms(cq_ref[...].astype(F32), gqa_ref[...]).astype(BF16)
    qf = jnp.dot(nq, wuq_ref[...], preferred_element_type=F32)
    cosa = cosa_ref[...]
    sina = sina_ref[...]
    for h in range(MLA_HEADS):
        base = h * 3 * LANES
        nope = qf[:, base:base + LANES]
        roped = qf[:, base + LANES:base + 2 * LANES] * cosa + qf[:, base + 2 * LANES:base + 3 * LANES] * sina
        qa_ref[:, h * MLA_DK_PAD:h * MLA_DK_PAD + LANES] = (nope * MLA_SCALE).astype(BF16)
        qa_ref[:, h * MLA_DK_PAD + LANES:(h + 1) * MLA_DK_PAD] = (roped * MLA_SCALE).astype(BF16)

    nkv = _rms(ckv_ref[...].astype(F32), gkva_ref[...]).astype(BF16)
    kvf = jnp.dot(nkv, wukv_ref[...], preferred_element_type=F32)
    t = kpe_ref[...].astype(F32) * csa_ref[...]
    kpe = jnp.where(lane < MLA_ROPE, t + pltpu.roll(t, MLA_ROPE, axis=1), 0.0).astype(BF16)
    for h in range(MLA_HEADS):
        ka_ref[:, h * MLA_DK_PAD:h * MLA_DK_PAD + LANES] = kvf[:, h * LANES:(h + 1) * LANES].astype(BF16)
        ka_ref[:, h * MLA_DK_PAD + LANES:(h + 1) * MLA_DK_PAD] = kpe
    va_ref[...] = kvf[:, MLA_HEADS * MLA_NOPE:].astype(BF16)

    cosb = cosb_ref[...]
    sinb = sinb_ref[...]
    first = (lane % (GQA_HEAD_DIM // 2)) < (GQA_HEAD_DIM // 4)

    def rope_b(x):
        rot = jnp.where(first, pltpu.roll(x, LANES - GQA_HEAD_DIM // 4, axis=1),
                        pltpu.roll(x, GQA_HEAD_DIM // 4, axis=1))
        return x * cosb + rot * sinb

    gqn = gqn_ref[...]
    gkn = gkn_ref[...]
    for h in range(GQA_HEADS):
        x = bq_ref[:, h * LANES:(h + 1) * LANES].astype(F32)
        qb_ref[:, h * LANES:(h + 1) * LANES] = (rope_b(_rms(x, gqn)) * GQA_SCALE).astype(BF16)
    for h in range(GQA_KV_HEADS):
        x = bk_ref[:, h * LANES:(h + 1) * LANES].astype(F32)
        kb_ref[:, h * LANES:(h + 1) * LANES] = rope_b(_rms(x, gkn)).astype(BF16)


def _prep(p, pk, tables, w_uq_x, w_ukv_r, g_qa, g_kva, g_qn, g_kn):
    n = p.shape[0]
    tm = _pick(n, (512, 256))
    n_pos = SEQ // tm
    n_lat_tiles = (n // (SEQ + CTX_LEN)) * n_pos

    def tab(i):
        return (jnp.where(i < n_lat_tiles, i % n_pos, n_pos), 0)

    def col(width, off):
        return pl.BlockSpec((tm, width), lambda i: (i, off // width))

    def whole(a):
        return pl.BlockSpec(a.shape, lambda i: (0,) * a.ndim)

    tab_spec = pl.BlockSpec((tm, LANES), tab)
    cosa, sina, csa, cosb, sinb = tables
    outs = [(MLA_HEADS * MLA_DK_PAD, "qa"), (MLA_HEADS * MLA_DK_PAD, "ka"), (MLA_HEADS * MLA_V, "va"),
            (GQA_HEADS * GQA_HEAD_DIM, "qb"), (GQA_KV_HEADS * GQA_HEAD_DIM, "kb")]
    return pl.pallas_call(
        _prep_kernel,
        grid=(n // tm,),
        in_specs=[col(512, P_CQ), col(512, P_CKV), col(LANES, 0), col(512, P_B), col(256, P_B + 512),
                  tab_spec, tab_spec, tab_spec, tab_spec, tab_spec,
                  whole(w_uq_x), whole(w_ukv_r), whole(g_qa), whole(g_kva), whole(g_qn), whole(g_kn)],
        out_specs=[pl.BlockSpec((tm, w), lambda i: (i, 0)) for w, _ in outs],
        out_shape=[jax.ShapeDtypeStruct((n, w), BF16) for w, _ in outs],
        compiler_params=_params("arbitrary"),
        name="attn_prep",
    )(p, p, pk, p, p, cosa, sina, csa, cosb, sinb, w_uq_x, w_ukv_r, g_qa, g_kva, g_qn, g_kn)


def _attn_kernel(*refs, n_seg, n_groups, rep, dk, dv, scale):
    q_ref = refs[0]
    segs = [(refs[1 + 2 * s], refs[2 + 2 * s]) for s in range(n_seg)]
    o_ref = refs[1 + 2 * n_seg]
    nt = (((1,), (1,)), ((), ()))
    for g in range(n_groups):
        ks = [k_ref[:, g * dk:(g + 1) * dk] for k_ref, _ in segs]
        vs = []
        for _, v_ref in segs:
            one = (lax.broadcasted_iota(jnp.int32, (v_ref.shape[0], LANES), 1) == 0).astype(BF16)
            vs.append(jnp.concatenate([v_ref[:, g * dv:(g + 1) * dv], one], axis=1))
        for r in range(rep):
            hq = g * rep + r
            q = q_ref[:, hq * dk:(hq + 1) * dk]
            ss = [lax.dot_general(q, k, nt, preferred_element_type=F32) for k in ks]
            if scale != 1.0:
                ss = [s * scale for s in ss]
            m = functools.reduce(jnp.maximum, [jnp.max(s, axis=-1, keepdims=True) for s in ss])
            acc = None
            for s, v in zip(ss, vs):
                a = jnp.dot(jnp.exp(s - m).astype(BF16), v, preferred_element_type=F32)
                acc = a if acc is None else acc + a
            o_ref[:, hq * dv:(hq + 1) * dv] = (acc[:, :dv] / acc[:, dv:dv + 1]).astype(o_ref.dtype)


def _attention(q, q_off, segs, bsz, n_q_per_batch, q_row0, *, n_groups, rep, dk, dv, scale, name):
    tq = min(512, n_q_per_batch)
    nq = n_q_per_batch // tq
    qw = n_groups * rep * dk
    ow = n_groups * rep * dv

    def q_map(b, i):
        return ((q_row0 + b * n_q_per_batch) // tq + i, q_off // qw)

    in_specs = [pl.BlockSpec((tq, qw), q_map)]
    args = [q]
    for k_arr, k_off, v_arr, v_off, length, row0 in segs:
        kw, vw = n_groups * dk, n_groups * dv
        in_specs.append(pl.BlockSpec((length, kw), functools.partial(
            lambda b, i, length, row0, blk: (row0 // length + b, blk), length=length, row0=row0, blk=k_off // kw)))
        in_specs.append(pl.BlockSpec((length, vw), functools.partial(
            lambda b, i, length, row0, blk: (row0 // length + b, blk), length=length, row0=row0, blk=v_off // vw)))
        args += [k_arr, v_arr]
    return pl.pallas_call(
        functools.partial(_attn_kernel, n_seg=len(segs), n_groups=n_groups, rep=rep, dk=dk, dv=dv, scale=scale),
        grid=(bsz, nq),
        in_specs=in_specs,
        out_specs=pl.BlockSpec((tq, ow), lambda b, i: (b * nq + i, 0)),
        out_shape=jax.ShapeDtypeStruct((bsz * n_q_per_batch, ow), BF16),
        compiler_params=_params("arbitrary", "arbitrary"),
        name=name,
    )(*args)


def _na_kernel(q_ref, k_ref, v_ref, kc_ref, vc_ref, bias_ref, o_ref):
    j = pl.program_id(0)
    k_rows = NA_K_ROWS * GRID_W
    start = jnp.clip(NA_Q_ROWS * j - NA_WIN_H // 2, 0, SEQ // GRID_W - NA_K_ROWS) * GRID_W
    start = pl.multiple_of(start, 256)
    hd = NA_HEAD_DIM

    def with_ones(v):
        one = (lax.broadcasted_iota(jnp.int32, (v.shape[0], LANES), 1) == 0).astype(BF16)
        return jnp.concatenate([v, one], axis=1)

    for h in range(NA_HEADS):
        cols = slice(h * hd, (h + 1) * hd)
        q = q_ref[:, cols]
        s1 = lax.dot_general(q, k_ref[pl.ds(start, k_rows), cols], NT_DIMS, preferred_element_type=F32) * NA_SCALE
        s1 = s1 + bias_ref[0, h]
        s2 = lax.dot_general(q, kc_ref[:, cols], NT_DIMS, preferred_element_type=F32) * NA_SCALE
        m = jnp.maximum(jnp.max(s1, axis=-1, keepdims=True), jnp.max(s2, axis=-1, keepdims=True))
        acc = jnp.dot(jnp.exp(s1 - m).astype(BF16), with_ones(v_ref[pl.ds(start, k_rows), cols]),
                      preferred_element_type=F32)
        acc = acc + jnp.dot(jnp.exp(s2 - m).astype(BF16), with_ones(vc_ref[:, cols]), preferred_element_type=F32)
        o_ref[:, cols] = (acc[:, :hd] / acc[:, hd:hd + 1]).astype(o_ref.dtype)


def _na_bias_kernel(tp_ref, o_ref):
    t = pl.program_id(0)
    rows = SEQ // GRID_W
    k0 = jnp.clip(NA_Q_ROWS * t - NA_WIN_H // 2, 0, rows - NA_K_ROWS)
    left = lax.broadcasted_iota(jnp.int32, (GRID_W, 2 * GRID_W), 1) < GRID_W
    neg = -jnp.inf
    for a in range(NA_Q_ROWS):
        rq = NA_Q_ROWS * t + a
        r0 = jnp.clip(rq - NA_WIN_H // 2, 0, rows - NA_WIN_H)
        for b2 in range(NA_K_ROWS // 2):
            rk = k0 + 2 * b2
            ok_l = (rk >= r0) & (rk < r0 + NA_WIN_H)
            ok_r = (rk + 1 >= r0) & (rk + 1 < r0 + NA_WIN_H)
            pair = tp_ref[0, jnp.clip(rk - rq + NA_WIN_H, 0, 2 * NA_WIN_H - 1)]
            keep = jnp.where(left, ok_l.astype(jnp.int32), ok_r.astype(jnp.int32)) > 0
            o_ref[0, 0, a * GRID_W:(a + 1) * GRID_W, b2 * 2 * GRID_W:(b2 + 1) * 2 * GRID_W] = jnp.where(keep, pair, neg)


def _na_bias(rpb):
    n_dr = 2 * NA_WIN_H - 1
    cols = jnp.arange(GRID_W)
    c0 = jnp.clip(cols - NA_WIN_W // 2, 0, GRID_W - NA_WIN_W)
    col_ok = (cols[None, :] >= c0[:, None]) & (cols[None, :] < c0[:, None] + NA_WIN_W)
    dc = jnp.clip(cols[None, :] - cols[:, None] + NA_WIN_W - 1, 0, 2 * NA_WIN_W - 2)
    oh_c = (dc[:, :, None] == jnp.arange(2 * NA_WIN_W - 1)).astype(F32)
    by_col = jnp.einsum('hdc,qkc->hdqk', rpb.astype(F32), oh_c, precision=lax.Precision.HIGHEST)
    by_col = jnp.where(col_ok[None, None], by_col, -jnp.inf)
    zero = jnp.zeros((NA_HEADS, 1, GRID_W, GRID_W), F32)
    padded = jnp.concatenate([zero, by_col, zero], axis=1)
    pairs = jnp.concatenate([padded[:, :n_dr + 1], padded[:, 1:]], axis=-1)
    n_tiles = SEQ // GRID_W // NA_Q_ROWS
    tq, tk = NA_Q_ROWS * GRID_W, NA_K_ROWS * GRID_W
    return pl.pallas_call(
        _na_bias_kernel,
        grid=(n_tiles, NA_HEADS),
        in_specs=[pl.BlockSpec((1, n_dr + 1, GRID_W, 2 * GRID_W), lambda t, h: (h, 0, 0, 0))],
        out_specs=pl.BlockSpec((1, 1, tq, tk), lambda t, h: (t, h, 0, 0)),
        out_shape=jax.ShapeDtypeStruct((n_tiles, NA_HEADS, tq, tk), F32),
        compiler_params=_params("arbitrary", "arbitrary"),
        name="na_bias",
    )(pairs)


def _na_attention(p, bias, bsz, n_lat):
    tq = NA_Q_ROWS * GRID_W
    n_tiles = SEQ // tq
    w = NA_HEADS * NA_HEAD_DIM
    qb, kb, vb = P_C // w, (P_C + w) // w, (P_C + 2 * w) // w
    return pl.pallas_call(
        _na_kernel,
        grid=(n_tiles, bsz),
        in_specs=[
            pl.BlockSpec((tq, w), lambda j, b: (b * n_tiles + j, qb)),
            pl.BlockSpec((SEQ, w), lambda j, b: (b, kb)),
            pl.BlockSpec((SEQ, w), lambda j, b: (b, vb)),
            pl.BlockSpec((CTX_LEN, w), lambda j, b: (n_lat // CTX_LEN + b, kb)),
            pl.BlockSpec((CTX_LEN, w), lambda j, b: (n_lat // CTX_LEN + b, vb)),
            pl.BlockSpec((1, NA_HEADS, tq, NA_K_ROWS * GRID_W), lambda j, b: (j, 0, 0, 0)),
        ],
        out_specs=pl.BlockSpec((tq, w), lambda j, b: (b * n_tiles + j, 0)),
        out_shape=jax.ShapeDtypeStruct((n_lat, w), BF16),
        compiler_params=_params("arbitrary", "arbitrary"),
        name="na_attention",
    )(p, p, p, p, p, bias)


def _conv_kernel(b_ref, c_ref, u_ref, w_ref, o_ref):
    length = c_ref.shape[0]
    z = c_ref[...].astype(F32) * u_ref[...].astype(F32)
    row = lax.broadcasted_iota(jnp.int32, z.shape, 0)
    z_prev = jnp.where(row == 0, 0.0, pltpu.roll(z, 1, axis=0))
    z_next = jnp.where(row == length - 1, 0.0, pltpu.roll(z, length - 1, axis=0))
    conv = z_prev * w_ref[0:1, :] + z * w_ref[1:2, :] + z_next * w_ref[2:3, :]
    o_ref[...] = (b_ref[...].astype(F32) * conv).astype(o_ref.dtype)


def _short_conv(p, w_conv8, bsz, length, row0):
    blk = P_D // CONV_CH

    def rows(b):
        return row0 // length + b

    return pl.pallas_call(
        _conv_kernel,
        grid=(bsz,),
        in_specs=[
            pl.BlockSpec((length, CONV_CH), lambda b: (rows(b), blk)),
            pl.BlockSpec((length, CONV_CH), lambda b: (rows(b), blk + 1)),
            pl.BlockSpec((length, CONV_CH), lambda b: (rows(b), blk + 2)),
            pl.BlockSpec((8, CONV_CH), lambda b: (0, 0)),
        ],
        out_specs=pl.BlockSpec((length, CONV_CH), lambda b: (b, 0)),
        out_shape=jax.ShapeDtypeStruct((bsz * length, CONV_CH), BF16),
        compiler_params=_params("arbitrary"),
        name="short_conv",
    )(p, p, p, w_conv8)


MERGE_CHUNK = 512


def _merge_kernel(ya_ref, yb_ref, yc_ref, yd_ref, g0_ref, g1_ref, g2_ref, g3_ref, wb_ref, o_ref):
    pairs = ((ya_ref, g0_ref), (yb_ref, g1_ref), (yc_ref, g2_ref), (yd_ref, g3_ref))
    for c in range(D_MODEL // MERGE_CHUNK):
        cols = slice(c * MERGE_CHUNK, (c + 1) * MERGE_CHUNK)
        acc = None
        for k, (y_ref, g_ref) in enumerate(pairs):
            proj = jnp.dot(y_ref[...], wb_ref[0, k, :, cols], preferred_element_type=F32)
            gate = 0.5 * jnp.tanh(0.5 * g_ref[:, cols].astype(F32)) + 0.5
            acc = gate * proj if acc is None else acc + gate * proj
        o_ref[:, cols] = acc.astype(o_ref.dtype)


def _merge(ys, p, w_branch, layer, n_act):
    tm = _pick(n_act, (512, 256))
    g_specs = [pl.BlockSpec((tm, D_MODEL), functools.partial(
        lambda i, k: (i, P_G // D_MODEL + k), k=k)) for k in range(N_BRANCHES)]
    return pl.pallas_call(
        _merge_kernel,
        grid=(n_act // tm,),
        in_specs=[pl.BlockSpec((tm, BRANCH_WIDTH), lambda i: (i, 0))] * N_BRANCHES + g_specs + [
            pl.BlockSpec((1, N_BRANCHES, BRANCH_WIDTH, D_MODEL), lambda i: (layer, 0, 0, 0))],
        out_specs=pl.BlockSpec((tm, D_MODEL), lambda i: (i, 0)),
        out_shape=jax.ShapeDtypeStruct((n_act, D_MODEL), BF16),
        compiler_params=_params("arbitrary"),
        name="branch_merge",
    )(*ys, p, p, p, p, w_branch)


R_E, R_W, R_RANK = 0, 2, 4


def _router_kernel(x_ref, g_ref, sh_ref, sc_ref, wr_ref, br_ref, h_ref, route_ref, counts_ref, run_ref,
                   *, tm, bsz):
    step = pl.program_id(0)
    r = _mod_row(step, tm, bsz)
    h = _rms(x_ref[...], g_ref[...]) * (1.0 + sc_ref[pl.ds(r, 1), :]) + sh_ref[pl.ds(r, 1), :]
    h_ref[...] = h
    h_hi = h.astype(BF16)
    h_lo = (h - h_hi.astype(F32)).astype(BF16)
    logits = jnp.dot(h_hi, wr_ref[0], preferred_element_type=F32) + (
        jnp.dot(h_lo, wr_ref[0], preferred_element_type=F32) + jnp.dot(h_hi, wr_ref[1], preferred_element_type=F32))
    bias = br_ref[...]
    lane = lax.broadcasted_iota(jnp.int32, logits.shape, 1)
    lane_f = lane.astype(F32)
    neg = -jnp.inf

    def rmax(x, mask):
        return jnp.max(jnp.where(mask, x, neg), axis=-1, keepdims=True)

    def first_at(x, val, mask):
        return jnp.min(jnp.where(mask & (x == val), lane_f, 2.0 * LANES), axis=-1, keepdims=True)

    def pick(x, idx):
        return jnp.sum(jnp.where(lane_f == idx, x, 0.0), axis=-1, keepdims=True)

    gmask = lane < N_GROUPS
    ge = jnp.where(gmask, jnp.exp(logits - rmax(logits, gmask)), 0.0)
    g_prob = ge / jnp.sum(ge, axis=-1, keepdims=True)
    g_score = g_prob + bias
    g_sel = first_at(g_score, rmax(g_score, gmask), gmask)
    p_group = pick(g_prob, g_sel)

    group_of_lane = lax.shift_right_arithmetic(lane - N_EXPERTS, 3).astype(F32)
    emask = (lane >= N_EXPERTS) & (group_of_lane == g_sel)
    ee = jnp.where(emask, jnp.exp(logits - rmax(logits, emask)), 0.0)
    e_prob = ee / jnp.sum(ee, axis=-1, keepdims=True)
    e_score = e_prob + bias
    i1 = first_at(e_score, rmax(e_score, emask), emask)
    mask2 = emask & (lane_f != i1)
    i2 = first_at(e_score, rmax(e_score, mask2), mask2)
    p1 = pick(e_prob, i1)
    p2 = pick(e_prob, i2)
    w1 = p_group * p1 / (p1 + p2)
    w2 = p_group * p2 / (p1 + p2)
    e1 = i1 - N_EXPERTS
    e2 = i2 - N_EXPERTS

    @pl.when(step == 0)
    def _():
        run_ref[...] = jnp.zeros_like(run_ref)

    oh1 = lane_f == e1
    oh2 = lane_f == e2
    both = oh1.astype(F32) + oh2.astype(F32)
    tri = (lax.broadcasted_iota(jnp.int32, (tm, tm), 1) < lax.broadcasted_iota(jnp.int32, (tm, tm), 0)).astype(BF16)
    before = jnp.dot(tri, both.astype(BF16), preferred_element_type=F32) + run_ref[0:1, :]
    rank1 = jnp.sum(jnp.where(oh1, before, 0.0), axis=-1, keepdims=True)
    rank2 = jnp.sum(jnp.where(oh2, before + oh1.astype(F32), 0.0), axis=-1, keepdims=True)
    run_ref[...] = run_ref[...] + jnp.sum(both, axis=0, keepdims=True)
    counts_ref[...] = run_ref[...]

    vals = (e1, e2, w1, w2, rank1, rank2)
    out = jnp.zeros_like(logits)
    for k, v in enumerate(vals):
        out = jnp.where(lane == k, v, out)
    route_ref[...] = out


def _ffn_input_and_route(x1, gain, mod, w_route, b_route, bsz):
    n = x1.shape[0]
    tm = _pick(n, (512, 256))
    rows = mod.shape[0]
    return pl.pallas_call(
        functools.partial(_router_kernel, tm=tm, bsz=bsz),
        grid=(n // tm,),
        in_specs=[
            pl.BlockSpec((tm, D_MODEL), lambda i: (i, 0)),
            pl.BlockSpec((1, D_MODEL), lambda i: (0, 0)),
            pl.BlockSpec((rows, D_MODEL), lambda i: (0, 3)),
            pl.BlockSpec((rows, D_MODEL), lambda i: (0, 4)),
            pl.BlockSpec((2, D_MODEL, LANES), lambda i: (0, 0, 0)),
            pl.BlockSpec((1, LANES), lambda i: (0, 0)),
        ],
        out_specs=[pl.BlockSpec((tm, D_MODEL), lambda i: (i, 0)),
                   pl.BlockSpec((tm, LANES), lambda i: (i, 0)),
                   pl.BlockSpec((8, LANES), lambda i: (0, 0))],
        out_shape=[jax.ShapeDtypeStruct((n, D_MODEL), F32),
                   jax.ShapeDtypeStruct((n, LANES), F32),
                   jax.ShapeDtypeStruct((8, LANES), F32)],
        scratch_shapes=[pltpu.VMEM((8, LANES), F32)],
        compiler_params=_params("arbitrary"),
        name="ffn_input_route",
    )(x1, gain.reshape(1, D_MODEL), mod, mod, w_route, b_route)


def _dispatch_indices(route, counts_row, n_tok):
    counts = counts_row[0, :N_EXPERTS].astype(jnp.int32)
    padded = (counts + MOE_BLOCK - 1) // MOE_BLOCK * MOE_BLOCK
    pad_end = jnp.cumsum(padded)
    pad_start = pad_end - padded
    eid = route[:, R_E:R_E + TOP_K].astype(jnp.int32)
    rank = route[:, R_RANK:R_RANK + TOP_K].astype(jnp.int32)
    onehot = eid[:, :, None] == jnp.arange(N_EXPERTS, dtype=jnp.int32)[None, None, :]
    pos = (jnp.sum(jnp.where(onehot, pad_start[None, None, :], 0), axis=-1) + rank).reshape(-1)
    n_blocks = (n_tok * TOP_K + N_EXPERTS * (MOE_BLOCK - 1)) // MOE_BLOCK
    blk_start = jnp.arange(n_blocks, dtype=jnp.int32) * MOE_BLOCK
    blk_e = jnp.minimum(jnp.sum(pad_end[None, :] <= blk_start[:, None], axis=1), N_EXPERTS - 1).astype(jnp.int32)
    n_used = (pad_end[-1:] // MOE_BLOCK).astype(jnp.int32)
    last_blk = jnp.where(counts > 0, pad_end // MOE_BLOCK - 1, -1).astype(jnp.int32)
    blk = jnp.arange(n_blocks, dtype=jnp.int32)
    prev_e = jnp.concatenate([jnp.full((1,), -1, jnp.int32), blk_e[:-1]])
    first = (blk < n_used) & (blk_e != prev_e)
    ordinal = jnp.cumsum(first.astype(jnp.int32)) - 1
    n_first = jnp.sum(first.astype(jnp.int32))
    k = jnp.arange(N_EXPERTS, dtype=jnp.int32)
    expert_at = jnp.sum(jnp.where(first[None, :] & (ordinal[None, :] == k[:, None]), blk_e[None, :], 0), axis=1)

    def expert_after(d):
        target = ordinal + d
        e = jnp.sum(jnp.where(k[None, :] == target[:, None], expert_at[None, :], 0), axis=1)
        return jnp.where(target < n_first, e, -1).astype(jnp.int32)

    sched = (blk_e, n_used, first.astype(jnp.int32), (ordinal % MOE_W_SLOTS).astype(jnp.int32),
             expert_after(1), expert_after(2))
    return sched, last_blk, pos.astype(jnp.int32), n_blocks


def _row_copy(src, src_row, dst, dst_row, sem):
    return pltpu.make_async_copy(src.at[pl.ds(src_row, 1), :], dst.at[pl.ds(dst_row, 1), :], sem)


DISPATCH_CHUNK = 512
DMA_UNROLL = 8


def _dispatch_kernel(pos_ref, last_ref, nused_ref, h_ref, xs_hbm, zbuf, sem, zsem):
    base = pl.program_id(0) * DISPATCH_CHUNK
    n_blocks = xs_hbm.shape[0] // MOE_BLOCK

    @pl.when(pl.program_id(0) == 0)
    def _():
        zbuf[...] = jnp.zeros_like(zbuf)

        def fill(b):
            return pltpu.make_async_copy(zbuf, xs_hbm.at[pl.ds(pl.multiple_of(b * MOE_BLOCK, MOE_BLOCK), MOE_BLOCK), :],
                                         zsem.at[0])

        def each(fn):
            def expert(e, carry):
                @pl.when(last_ref[e] >= 0)
                def _():
                    fn(fill(last_ref[e]))
                return carry

            def unused(b, carry):
                fn(fill(b))
                return carry

            lax.fori_loop(0, N_EXPERTS, expert, 0)
            lax.fori_loop(nused_ref[0], n_blocks, unused, 0)

        each(lambda c: c.start())
        each(lambda c: c.wait())

    def body(r, carry):
        for k in range(TOP_K):
            _row_copy(h_ref, r, xs_hbm, pos_ref[(base + r) * TOP_K + k], sem.at[0]).start(priority=k % 2)
        return carry

    lax.fori_loop(0, DISPATCH_CHUNK, body, 0, unroll=DMA_UNROLL)
    for k in range(TOP_K):
        pltpu.make_async_copy(h_ref, xs_hbm.at[pl.ds(0, DISPATCH_CHUNK), :], sem.at[0]).wait()


def _dispatch(h2, pos, last_blk, n_used, n_rows):
    n, width = h2.shape
    assert n % DISPATCH_CHUNK == 0
    grid_spec = pltpu.PrefetchScalarGridSpec(
        num_scalar_prefetch=3,
        grid=(n // DISPATCH_CHUNK,),
        in_specs=[pl.BlockSpec((DISPATCH_CHUNK, width), lambda i, *_: (i, 0))],
        out_specs=pl.BlockSpec(memory_space=pl.ANY),
        scratch_shapes=[pltpu.VMEM((MOE_BLOCK, width), h2.dtype), pltpu.SemaphoreType.DMA((1,)),
                        pltpu.SemaphoreType.DMA((1,))],
    )
    return pl.pallas_call(
        _dispatch_kernel,
        grid_spec=grid_spec,
        out_shape=jax.ShapeDtypeStruct((n_rows, width), h2.dtype),
        compiler_params=_params("arbitrary"),
        name="moe_dispatch",
    )(pos, last_blk, n_used, h2)


def _moe_kernel(be_ref, nused_ref, first_ref, slot_ref, nxt1_ref, nxt2_ref, x_ref, wg_hbm, wu_hbm, wd_hbm, o_ref,
                wgf, wuf, wdf, wgb, wub, wdb, sem, *, layer):
    i = pl.program_id(0)

    def w_copies(e, s):
        return (pltpu.make_async_copy(wg_hbm.at[layer, e], wgf.at[s], sem.at[s]),
                pltpu.make_async_copy(wu_hbm.at[layer, e], wuf.at[s], sem.at[s]),
                pltpu.make_async_copy(wd_hbm.at[layer, e], wdf.at[s], sem.at[s]))

    def start(e, s):
        for c in w_copies(e, s):
            c.start(priority=BULK_DMA_PRIORITY)

    @pl.when(first_ref[i] == 1)
    def _():
        s = slot_ref[i]

        @pl.when(i == 0)
        def _():
            start(be_ref[i], s)

            @pl.when(nxt1_ref[i] >= 0)
            def _():
                start(nxt1_ref[i], (s + 1) % MOE_W_SLOTS)

        @pl.when(nxt2_ref[i] >= 0)
        def _():
            start(nxt2_ref[i], (s + 2) % MOE_W_SLOTS)

        for c in w_copies(be_ref[i], s):
            c.wait()
        wgb[...] = wgf[s].astype(BF16)
        wub[...] = wuf[s].astype(BF16)
        wdb[...] = wdf[s].astype(BF16)

    @pl.when(i < nused_ref[0])
    def _():
        x = x_ref[...].astype(BF16)
        gate = jnp.dot(x, wgb[...], preferred_element_type=F32)
        up = jnp.dot(x, wub[...], preferred_element_type=F32)
        hid = (gate * jax.nn.sigmoid(gate) * up).astype(BF16)
        o_ref[...] = jnp.dot(hid, wdb[...], preferred_element_type=F32)

    @pl.when(i >= nused_ref[0])
    def _():
        o_ref[...] = jnp.zeros_like(o_ref)


def _moe(xs, sched, w_gate, w_up, w_down, layer):
    n_blocks = sched[0].shape[0]
    grid_spec = pltpu.PrefetchScalarGridSpec(
        num_scalar_prefetch=len(sched),
        grid=(n_blocks,),
        in_specs=[pl.BlockSpec((MOE_BLOCK, D_MODEL), lambda i, *_: (i, 0))] + [pl.BlockSpec(memory_space=pl.ANY)] * 3,
        out_specs=pl.BlockSpec((MOE_BLOCK, D_MODEL), lambda i, *_: (i, 0)),
        scratch_shapes=[
            pltpu.VMEM((MOE_W_SLOTS, D_MODEL, EXPERT_HIDDEN), F32),
            pltpu.VMEM((MOE_W_SLOTS, D_MODEL, EXPERT_HIDDEN), F32),
            pltpu.VMEM((MOE_W_SLOTS, EXPERT_HIDDEN, D_MODEL), F32),
            pltpu.VMEM((D_MODEL, EXPERT_HIDDEN), BF16),
            pltpu.VMEM((D_MODEL, EXPERT_HIDDEN), BF16),
            pltpu.VMEM((EXPERT_HIDDEN, D_MODEL), BF16),
            pltpu.SemaphoreType.DMA((MOE_W_SLOTS,)),
        ],
    )
    return pl.pallas_call(
        functools.partial(_moe_kernel, layer=layer),
        grid_spec=grid_spec,
        out_shape=jax.ShapeDtypeStruct((n_blocks * MOE_BLOCK, D_MODEL), F32),
        compiler_params=_params("arbitrary"),
        name="moe_experts",
    )(*sched, xs, w_gate, w_up, w_down)


def _combine_kernel(pos_ref, ys_hbm, x_ref, route_ref, ga_ref, gf_ref, o_ref, rbuf, sem, *, tm, bsz, final):
    i = pl.program_id(0)
    n = pl.num_programs(0)
    slot = i % 2

    def gather(blk, s):
        def body(r, carry):
            for k in range(TOP_K):
                _row_copy(ys_hbm, pos_ref[(blk * tm + r) * TOP_K + k], rbuf.at[s], k * tm + r,
                          sem.at[s]).start(priority=k % 2)
            return carry
        lax.fori_loop(0, tm, body, 0, unroll=DMA_UNROLL)

    @pl.when(i == 0)
    def _():
        gather(0, 0)

    @pl.when(i + 1 < n)
    def _():
        gather(i + 1, 1 - slot)

    pltpu.make_async_copy(ys_hbm.at[pl.ds(0, TOP_K * tm), :], rbuf.at[slot], sem.at[slot]).wait()
    y = rbuf[slot, 0:tm, :] * route_ref[:, R_W:R_W + 1] + rbuf[slot, tm:2 * tm, :] * route_ref[:, R_W + 1:R_W + 2]
    r = _mod_row(i, tm, bsz)
    x = x_ref[...] + ga_ref[pl.ds(r, 1), :] * y
    if final:
        x = _rms(x, gf_ref[...])
    o_ref[...] = x


def _combine(ys, pos, x1, route, mod, g_final, bsz, final):
    n = x1.shape[0]
    tm = 128
    rows = mod.shape[0]
    grid_spec = pltpu.PrefetchScalarGridSpec(
        num_scalar_prefetch=1,
        grid=(n // tm,),
        in_specs=[
            pl.BlockSpec(memory_space=pl.ANY),
            pl.BlockSpec((tm, D_MODEL), lambda i, pos: (i, 0)),
            pl.BlockSpec((tm, LANES), lambda i, pos: (i, 0)),
            pl.BlockSpec((rows, D_MODEL), lambda i, pos: (0, 5)),
            pl.BlockSpec((1, D_MODEL), lambda i, pos: (0, 0)),
        ],
        out_specs=pl.BlockSpec((tm, D_MODEL), lambda i, pos: (i, 0)),
        scratch_shapes=[pltpu.VMEM((2, TOP_K * tm, D_MODEL), F32), pltpu.SemaphoreType.DMA((2,))],
    )
    return pl.pallas_call(
        functools.partial(_combine_kernel, tm=tm, bsz=bsz, final=final),
        grid_spec=grid_spec,
        out_shape=jax.ShapeDtypeStruct((n, D_MODEL), F32),
        compiler_params=_params("arbitrary"),
        name="moe_combine",
    )(pos, ys, x1, route, mod, g_final.reshape(1, D_MODEL))


def _rotate_half_cols(w, quarter):
    r1, r2, c1, c2 = (w[..., k * quarter:(k + 1) * quarter] for k in range(4))
    return jnp.concatenate([-r2, r1, -c2, c1], axis=-1)


def _kpe_rows(wt):
    kpe = jnp.swapaxes(wt[:, MLA_Q_LORA + MLA_KV_LORA:A_COLS, :], 1, 2)
    both = jnp.concatenate([kpe, _rotate_half_cols(kpe, MLA_ROPE // 4)], axis=-1)
    return jnp.swapaxes(both, 1, 2).astype(BF16)


def _relayout_w_uq(w_uq):
    w = w_uq.reshape(MLA_Q_LORA, MLA_HEADS, MLA_NOPE + MLA_ROPE)
    nope, pe = w[..., :MLA_NOPE], w[..., MLA_NOPE:]
    z = jnp.zeros_like(pe)
    out = jnp.concatenate([nope, pe, z, _rotate_half_cols(pe, MLA_ROPE // 4), z], axis=-1)
    return out.reshape(MLA_Q_LORA, MLA_HEADS * 3 * LANES).astype(BF16)


def _relayout_w_ukv(w_ukv):
    w = w_ukv.reshape(MLA_KV_LORA, MLA_HEADS, MLA_NOPE + MLA_V)
    return jnp.concatenate([w[..., :MLA_NOPE].reshape(MLA_KV_LORA, -1),
                            w[..., MLA_NOPE:].reshape(MLA_KV_LORA, -1)], axis=-1).astype(BF16)


def _rope_tables(tm):
    t = jnp.arange(SEQ)
    rows = (t // GRID_W).astype(F32)
    cols = (t % GRID_W).astype(F32)

    def cs(rot_dim):
        quarter = rot_dim // 4
        inv_freq = ROPE_THETA ** (-jnp.arange(quarter, dtype=F32) / quarter)
        ang_r = rows[:, None] * inv_freq[None, :]
        ang_c = cols[:, None] * inv_freq[None, :]
        cos = jnp.concatenate([jnp.cos(ang_r)] * 2 + [jnp.cos(ang_c)] * 2, axis=-1)
        sin = jnp.concatenate([jnp.sin(ang_r)] * 2 + [jnp.sin(ang_c)] * 2, axis=-1)
        return cos, sin

    def with_identity(tab, ident):
        return jnp.concatenate([tab, jnp.broadcast_to(ident, (tm, LANES))], axis=0)

    cos_a, sin_a = cs(MLA_ROPE)
    zeros = jnp.zeros((SEQ, LANES - MLA_ROPE), F32)
    lane = jnp.arange(LANES)
    one_lo = (lane < MLA_ROPE).astype(F32)[None, :]
    cosa = with_identity(jnp.concatenate([cos_a, zeros], axis=1), one_lo)
    sina = with_identity(jnp.concatenate([sin_a, zeros], axis=1), jnp.zeros((1, LANES), F32))
    csa = with_identity(jnp.concatenate([cos_a, sin_a], axis=1), one_lo)
    cos_b, sin_b = cs(GQA_HEAD_DIM)
    sign = jnp.where((lane % (GQA_HEAD_DIM // 2)) < GQA_HEAD_DIM // 4, -1.0, 1.0).astype(F32)[None, :]
    cosb = with_identity(cos_b, jnp.ones((1, LANES), F32))
    sinb = with_identity(sin_b * sign, jnp.zeros((1, LANES), F32))
    return cosa, sina, csa, cosb, sinb


def kernel(x, c, ctx, c_ctx, w_mod, b_mod, g_mix, g_ffn, w_in, w_uq, g_qa, w_ukv, g_kva, g_qn, g_kn, rpb, w_conv,
           w_branch, w_o, w_group, b_group, w_router, b_router, w_gate_e, w_up_e, w_down_e, g_final):
    bsz = x.shape[0]
    n_lat = bsz * SEQ
    n_ctx = bsz * CTX_LEN
    n_all = n_lat + n_ctx
    mod_rows = -(-(bsz + 1) // 8) * 8

    x_all = jnp.concatenate([x.reshape(n_lat, D_MODEL), ctx.reshape(n_ctx, D_MODEL)], axis=0)
    c_all = jnp.concatenate([c, c_ctx[None, :], jnp.zeros((mod_rows - bsz - 1, D_MODEL), F32)], axis=0)
    mods = _modulation(c_all, bsz + 1, w_mod, b_mod)
    tables = _rope_tables(_pick(n_all, (512, 256)))

    w_in_t = jnp.swapaxes(w_in, 1, 2)
    w_kpe_t = _kpe_rows(w_in_t)
    w_branch_b = w_branch.astype(BF16)
    w_o_b = w_o.astype(BF16)

    out = None
    for l in range(DEPTH):
        last = l == DEPTH - 1
        n_act = n_lat if last else n_all
        mod = mods[l]
        w_uq_x = _relayout_w_uq(w_uq[l])
        w_ukv_r = _relayout_w_ukv(w_ukv[l])
        w_route = jnp.concatenate([w_group[l], jnp.zeros((D_MODEL, N_EXPERTS - N_GROUPS), F32), w_router[l]], axis=1)
        w_route_hi = w_route.astype(BF16)
        w_route = jnp.stack([w_route_hi, (w_route - w_route_hi.astype(F32)).astype(BF16)])
        b_route = jnp.concatenate([b_group[l], jnp.zeros((N_EXPERTS - N_GROUPS,), F32), b_router[l]])[None, :]
        w_conv8 = jnp.concatenate([w_conv[l], jnp.zeros((8 - CONV_WIDTH, CONV_CH), F32)], axis=0)

        h = _hmod(x_all, g_mix[l], mod, bsz)
        p = _in_proj(h, w_in_t, l)
        pk = _matmul_nt(h, w_kpe_t, l, BF16, "kpe_proj")
        qa, ka, va, qb, kb = _prep(p, pk, tables, w_uq_x, w_ukv_r, g_qa[l][None, :], g_kva[l][None, :],
                                   g_qn[l][None, :], g_kn[l][None, :])

        lat_ctx = lambda k_arr, k_off, v_arr, v_off: [(k_arr, k_off, v_arr, v_off, SEQ, 0),
                                                      (k_arr, k_off, v_arr, v_off, CTX_LEN, n_lat)]
        ctx_only = lambda k_arr, k_off, v_arr, v_off: [(k_arr, k_off, v_arr, v_off, CTX_LEN, n_lat)]
        mla = dict(n_groups=MLA_HEADS, rep=1, dk=MLA_DK_PAD, dv=MLA_V, scale=1.0)
        gqa = dict(n_groups=GQA_KV_HEADS, rep=GQA_HEADS // GQA_KV_HEADS, dk=GQA_HEAD_DIM, dv=GQA_HEAD_DIM, scale=1.0)
        nac = dict(n_groups=NA_HEADS, rep=1, dk=NA_HEAD_DIM, dv=NA_HEAD_DIM, scale=NA_SCALE)

        ys_lat = [
            _attention(qa, 0, lat_ctx(ka, 0, va, 0), bsz, SEQ, 0, name="mla_attention", **mla),
            _attention(qb, 0, lat_ctx(kb, 0, p, P_B + 768), bsz, SEQ, 0, name="gqa_attention", **gqa),
            _na_attention(p, _na_bias(rpb[l]), bsz, n_lat),
            _short_conv(p, w_conv8, bsz, SEQ, 0),
        ]
        if last:
            ys_all = ys_lat
        else:
            ys_ctx = [
                _attention(qa, 0, ctx_only(ka, 0, va, 0), bsz, CTX_LEN, n_lat, name="mla_attention_ctx", **mla),
                _attention(qb, 0, ctx_only(kb, 0, p, P_B + 768), bsz, CTX_LEN, n_lat,
                           name="gqa_attention_ctx", **gqa),
                _attention(p, P_C, ctx_only(p, P_C + 512, p, P_C + 1024), bsz, CTX_LEN, n_lat,
                           name="na_attention_ctx", **nac),
                _short_conv(p, w_conv8, bsz, CTX_LEN, n_lat),
            ]
            ys_all = [jnp.concatenate([a, b], axis=0) for a, b in zip(ys_lat, ys_ctx)]

        mixed = _merge(ys_all, p, w_branch_b, l, n_act)
        x1 = _out_proj_residual(mixed, w_o_b, l, x_all, mod, bsz)
        h2, route, counts = _ffn_input_and_route(x1, g_ffn[l], mod, w_route, b_route, bsz)
        sched, last_blk, pos, n_blocks = _dispatch_indices(route, counts, n_act)
        xs = _dispatch(h2, pos, last_blk, sched[1], n_blocks * MOE_BLOCK)
        ys = _moe(xs, sched, w_gate_e, w_up_e, w_down_e, l)
        x_all = _combine(ys, pos, x1, route, mod, g_final, bsz, last)
        out = x_all
    return out.reshape(bsz, SEQ, D_MODEL)
```

```python
import functools

import jax
import jax.numpy as jnp
from jax import lax
from jax.experimental import pallas as pl
from jax.experimental.pallas import tpu as pltpu

F32 = jnp.float32
BF16 = jnp.bfloat16

D_MODEL = 2048
SEQ = 2048
DEPTH = 2
CTX_LEN = 256
GRID_W = 64
ROPE_THETA = 10000.0
NORM_EPS = 1e-6
MOD_CHUNKS = 6

MLA_HEADS = 4
MLA_Q_LORA = 512
MLA_KV_LORA = 512
MLA_NOPE = 128
MLA_ROPE = 64
MLA_V = 128
MLA_SCALE = (MLA_NOPE + MLA_ROPE) ** -0.5
MLA_DK_PAD = 256

GQA_HEADS = 4
GQA_KV_HEADS = 2
GQA_HEAD_DIM = 128
GQA_SCALE = GQA_HEAD_DIM ** -0.5

NA_HEADS = 4
NA_HEAD_DIM = 128
NA_WIN_H = 8
NA_WIN_W = 16
NA_SCALE = NA_HEAD_DIM ** -0.5
NA_Q_ROWS = 8
NA_K_ROWS = 16

CONV_CH = 512
CONV_WIDTH = 3
BRANCH_WIDTH = 512
N_BRANCHES = 4

A_COLS = MLA_Q_LORA + MLA_KV_LORA + MLA_ROPE
B_COLS = (GQA_HEADS + 2 * GQA_KV_HEADS) * GQA_HEAD_DIM
C_COLS = 3 * NA_HEADS * NA_HEAD_DIM
D_COLS = 3 * CONV_CH
G_COLS = N_BRANCHES * D_MODEL

P_B = 0
P_C = P_B + B_COLS
P_D = P_C + C_COLS
P_G = P_D + D_COLS
P_CQ = P_G + G_COLS
P_CKV = P_CQ + MLA_Q_LORA
P_COLS = P_CKV + MLA_KV_LORA
IN_TILE = 1024

N_GROUPS = 8
EXPERTS_PER_GROUP = 8
N_EXPERTS = 64
TOP_K = 2
EXPERT_HIDDEN = 512
MOE_BLOCK = 128
MOE_W_SLOTS = 3

LANES = 128
BULK_DMA_PRIORITY = 1
VMEM_LIMIT = 56 * 1024 * 1024


def _params(*sem):
    return pltpu.CompilerParams(dimension_semantics=sem, vmem_limit_bytes=VMEM_LIMIT)


def _pick(n, cands):
    for c in cands:
        if n % c == 0:
            return c
    raise ValueError(f"no tile for {n}")


def _rms(x, g):
    return x * lax.rsqrt(jnp.mean(x * x, axis=-1, keepdims=True) + NORM_EPS) * g


def _mod_row(i, tm, bsz):
    return jnp.minimum(i * tm // SEQ, bsz)


def _mod_kernel(ct_ref, w_ref, b_ref, o_ref, *, n_vec):
    ct = ct_ref[...]
    st = ct * jax.nn.sigmoid(ct)
    w = w_ref[0]
    rows = [jnp.sum(w * st[:, r:r + 1], axis=0, keepdims=True) for r in range(n_vec)]
    rows += [jnp.zeros_like(rows[0])] * (o_ref.shape[1] - n_vec)
    o_ref[0] = jnp.concatenate(rows, axis=0) + b_ref[0]


def _modulation(c_all, n_vec, w_mod, b_mod):
    rows = c_all.shape[0]
    n = MOD_CHUNKS * D_MODEL
    tn = 1024
    return pl.pallas_call(
        functools.partial(_mod_kernel, n_vec=n_vec),
        grid=(DEPTH, n // tn),
        in_specs=[
            pl.BlockSpec((D_MODEL, rows), lambda l, j: (0, 0)),
            pl.BlockSpec((1, D_MODEL, tn), lambda l, j: (l, 0, j)),
            pl.BlockSpec((1, 1, tn), lambda l, j: (l, 0, j)),
        ],
        out_specs=pl.BlockSpec((1, rows, tn), lambda l, j: (l, 0, j)),
        out_shape=jax.ShapeDtypeStruct((DEPTH, rows, n), F32),
        compiler_params=_params("arbitrary", "arbitrary"),
        name="modulation",
    )(c_all.T, w_mod, b_mod.reshape(DEPTH, 1, n))


def _modulated(x, gain, sh_ref, sc_ref, r):
    return _rms(x, gain) * (1.0 + sc_ref[pl.ds(r, 1), :]) + sh_ref[pl.ds(r, 1), :]


def _hmod_kernel(x_ref, c_ref, g_ref, sh_ref, sc_ref, h_ref, xall_ref, *, tm, bsz, n_lat_tiles):
    i = pl.program_id(0)
    r = _mod_row(i, tm, bsz)

    def emit(src_ref):
        x = src_ref[...]
        xall_ref[...] = x
        h_ref[...] = _modulated(x, g_ref[...], sh_ref, sc_ref, r).astype(h_ref.dtype)

    @pl.when(i < n_lat_tiles)
    def _():
        emit(x_ref)

    @pl.when(i >= n_lat_tiles)
    def _():
        emit(c_ref)


def _hmod(x_lat, x_ctx, gain, mod, bsz):
    n_lat, n_ctx = x_lat.shape[0], x_ctx.shape[0]
    n = n_lat + n_ctx
    tm = _pick(n_ctx, (512, 256))
    n_lat_tiles = n_lat // tm
    rows = mod.shape[0]
    return pl.pallas_call(
        functools.partial(_hmod_kernel, tm=tm, bsz=bsz, n_lat_tiles=n_lat_tiles),
        grid=(n // tm,),
        in_specs=[
            pl.BlockSpec((tm, D_MODEL), lambda i: (jnp.minimum(i, n_lat_tiles - 1), 0)),
            pl.BlockSpec((tm, D_MODEL), lambda i: (jnp.maximum(i - n_lat_tiles, 0), 0)),
            pl.BlockSpec((1, D_MODEL), lambda i: (0, 0)),
            pl.BlockSpec((rows, D_MODEL), lambda i: (0, 0)),
            pl.BlockSpec((rows, D_MODEL), lambda i: (0, 1)),
        ],
        out_specs=[pl.BlockSpec((tm, D_MODEL), lambda i: (i, 0)), pl.BlockSpec((tm, D_MODEL), lambda i: (i, 0))],
        out_shape=[jax.ShapeDtypeStruct((n, D_MODEL), BF16), jax.ShapeDtypeStruct((n, D_MODEL), F32)],
        compiler_params=_params("arbitrary"),
        name="hmod",
    )(x_lat, x_ctx, gain.reshape(1, D_MODEL), mod, mod)


NT_DIMS = (((1,), (1,)), ((), ()))


def _mm_nt_kernel(a_ref, w_ref, o_ref):
    o_ref[...] = lax.dot_general(a_ref[...], w_ref[0], NT_DIMS, preferred_element_type=F32).astype(o_ref.dtype)


def _matmul_nt(a, wt, layer, out_dtype, name):
    m, k = a.shape
    n = wt.shape[1]
    tm = _pick(m, (1024, 512, 256))
    return pl.pallas_call(
        _mm_nt_kernel,
        grid=(m // tm,),
        in_specs=[pl.BlockSpec((tm, k), lambda i: (i, 0)), pl.BlockSpec((1, n, k), lambda i: (layer, 0, 0))],
        out_specs=pl.BlockSpec((tm, n), lambda i: (i, 0)),
        out_shape=jax.ShapeDtypeStruct((m, n), out_dtype),
        compiler_params=_params("arbitrary"),
        name=name,
    )(a, wt)


def _in_proj_kernel(a_ref, wt_hbm, o_ref, wbuf, wbf, sem, *, layer):
    j = pl.program_id(0)
    n_tiles = pl.num_programs(0)
    n_shift = P_CQ // IN_TILE

    def w_copy(jj, slot):
        row = pl.multiple_of(jnp.where(jj < n_shift, A_COLS + jj * IN_TILE, 0), MLA_ROPE)
        return pltpu.make_async_copy(wt_hbm.at[layer, pl.ds(row, IN_TILE), :], wbuf.at[slot], sem.at[slot])

    @pl.when(pl.program_id(1) == 0)
    def _():
        @pl.when(j == 0)
        def _():
            w_copy(0, 0).start(priority=BULK_DMA_PRIORITY)

        @pl.when(j + 1 < n_tiles)
        def _():
            w_copy(j + 1, (j + 1) % 2).start(priority=BULK_DMA_PRIORITY)

        w_copy(j, j % 2).wait()
        wbf[...] = wbuf[j % 2].astype(BF16)

    o_ref[...] = lax.dot_general(a_ref[...], wbf[...], NT_DIMS, preferred_element_type=F32).astype(o_ref.dtype)


def _in_proj(h, wt, layer):
    m, k = h.shape
    tm = _pick(m, (1024, 512, 256))
    return pl.pallas_call(
        functools.partial(_in_proj_kernel, layer=layer),
        grid=(P_COLS // IN_TILE, m // tm),
        in_specs=[pl.BlockSpec((tm, k), lambda j, i: (i, 0)), pl.BlockSpec(memory_space=pl.ANY)],
        out_specs=pl.BlockSpec((tm, IN_TILE), lambda j, i: (i, j)),
        out_shape=jax.ShapeDtypeStruct((m, P_COLS), BF16),
        scratch_shapes=[pltpu.VMEM((2, IN_TILE, k), F32), pltpu.VMEM((IN_TILE, k), BF16),
                        pltpu.SemaphoreType.DMA((2,))],
        compiler_params=_params("arbitrary", "arbitrary"),
        name="in_proj",
    )(h, wt)


def _mm_res_kernel(a_ref, w_ref, x_ref, ga_ref, o_ref, *, tm, bsz):
    r = _mod_row(pl.program_id(1), tm, bsz)
    acc = jnp.dot(a_ref[...], w_ref[0], preferred_element_type=F32)
    o_ref[...] = x_ref[...] + ga_ref[pl.ds(r, 1), :] * acc


def _out_proj_residual(mixed, w_o, layer, x_all, mod, bsz):
    m = mixed.shape[0]
    tm = _pick(m, (1024, 512, 256))
    tn = 1024
    rows = mod.shape[0]
    ga_blk = 2 * D_MODEL // tn
    return pl.pallas_call(
        functools.partial(_mm_res_kernel, tm=tm, bsz=bsz),
        grid=(D_MODEL // tn, m // tm),
        in_specs=[
            pl.BlockSpec((tm, D_MODEL), lambda j, i: (i, 0)),
            pl.BlockSpec((1, D_MODEL, tn), lambda j, i: (layer, 0, j)),
            pl.BlockSpec((tm, tn), lambda j, i: (i, j)),
            pl.BlockSpec((rows, tn), lambda j, i: (0, ga_blk + j)),
        ],
        out_specs=pl.BlockSpec((tm, tn), lambda j, i: (i, j)),
        out_shape=jax.ShapeDtypeStruct((m, D_MODEL), F32),
        compiler_params=_params("arbitrary", "arbitrary"),
        name="out_proj_residual",
    )(mixed, w_o, x_all, mod)


def _prep_kernel(cq_ref, ckv_ref, kpe_ref, bq_ref, bk_ref,
                 cosa_ref, sina_ref, csa_ref, cosb_ref, sinb_ref,
                 wuq_ref, wukv_ref, gqa_ref, gkva_ref, gqn_ref, gkn_ref,
                 qa_ref, ka_ref, va_ref, qb_ref, kb_ref):
    tm = cq_ref.shape[0]
    lane = lax.broadcasted_iota(jnp.int32, (tm, LANES), 1)

    nq = _rms(cq_ref[...].astype(F32), gqa_ref[...]).astype(BF16)
    qf = jnp.dot(nq, wuq_ref[...], preferred_element_type=F32)
    cosa = cosa_ref[...]
    sina = sina_ref[...]
    for h in range(MLA_HEADS):
        base = h * 3 * LANES
        nope = qf[:, base:base + LANES]
        roped = qf[:, base + LANES:base + 2 * LANES] * cosa + qf[:, base + 2 * LANES:base + 3 * LANES] * sina
        qa_ref[:, h * MLA_DK_PAD:h * MLA_DK_PAD + LANES] = (nope * MLA_SCALE).astype(BF16)
        qa_ref[:, h * MLA_DK_PAD + LANES:(h + 1) * MLA_DK_PAD] = (roped * MLA_SCALE).astype(BF16)

    nkv = _rms(ckv_ref[...].astype(F32), gkva_ref[...]).astype(BF16)
    kvf = jnp.dot(nkv, wukv_ref[...], preferred_element_type=F32)
    t = kpe_ref[...].astype(F32) * csa_ref[...]
    kpe = jnp.where(lane < MLA_ROPE, t + pltpu.roll(t, MLA_ROPE, axis=1), 0.0).astype(BF16)
    for h in range(MLA_HEADS):
        ka_ref[:, h * MLA_DK_PAD:h * MLA_DK_PAD + LANES] = kvf[:, h * LANES:(h + 1) * LANES].astype(BF16)
        ka_ref[:, h * MLA_DK_PAD + LANES:(h + 1) * MLA_DK_PAD] = kpe
    va_ref[...] = kvf[:, MLA_HEADS * MLA_NOPE:].astype(BF16)

    cosb = cosb_ref[...]
    sinb = sinb_ref[...]
    first = (lane % (GQA_HEAD_DIM // 2)) < (GQA_HEAD_DIM // 4)

    def rope_b(x):
        rot = jnp.where(first, pltpu.roll(x, LANES - GQA_HEAD_DIM // 4, axis=1),
                        pltpu.roll(x, GQA_HEAD_DIM // 4, axis=1))
        return x * cosb + rot * sinb

    gqn = gqn_ref[...]
    gkn = gkn_ref[...]
    for h in range(GQA_HEADS):
        x = bq_ref[:, h * LANES:(h + 1) * LANES].astype(F32)
        qb_ref[:, h * LANES:(h + 1) * LANES] = (rope_b(_rms(x, gqn)) * GQA_SCALE).astype(BF16)
    for h in range(GQA_KV_HEADS):
        x = bk_ref[:, h * LANES:(h + 1) * LANES].astype(F32)
        kb_ref[:, h * LANES:(h + 1) * LANES] = rope_b(_rms(x, gkn)).astype(BF16)


def _prep(p, pk, tables, w_uq_x, w_ukv_r, g_qa, g_kva, g_qn, g_kn):
    n = p.shape[0]
    tm = _pick(n, (512, 256))
    n_pos = SEQ // tm
    n_lat_tiles = (n // (SEQ + CTX_LEN)) * n_pos

    def tab(i):
        return (jnp.where(i < n_lat_tiles, i % n_pos, n_pos), 0)

    def col(width, off):
        return pl.BlockSpec((tm, width), lambda i: (i, off // width))

    def whole(a):
        return pl.BlockSpec(a.shape, lambda i: (0,) * a.ndim)

    tab_spec = pl.BlockSpec((tm, LANES), tab)
    cosa, sina, csa, cosb, sinb = tables
    outs = [(MLA_HEADS * MLA_DK_PAD, "qa"), (MLA_HEADS * MLA_DK_PAD, "ka"), (MLA_HEADS * MLA_V, "va"),
            (GQA_HEADS * GQA_HEAD_DIM, "qb"), (GQA_KV_HEADS * GQA_HEAD_DIM, "kb")]
    return pl.pallas_call(
        _prep_kernel,
        grid=(n // tm,),
        in_specs=[col(512, P_CQ), col(512, P_CKV), col(LANES, 0), col(512, P_B), col(256, P_B + 512),
                  tab_spec, tab_spec, tab_spec, tab_spec, tab_spec,
                  whole(w_uq_x), whole(w_ukv_r), whole(g_qa), whole(g_kva), whole(g_qn), whole(g_kn)],
        out_specs=[pl.BlockSpec((tm, w), lambda i: (i, 0)) for w, _ in outs],
        out_shape=[jax.ShapeDtypeStruct((n, w), BF16) for w, _ in outs],
        compiler_params=_params("arbitrary"),
        name="attn_prep",
    )(p, p, pk, p, p, cosa, sina, csa, cosb, sinb, w_uq_x, w_ukv_r, g_qa, g_kva, g_qn, g_kn)


def _attn_kernel(*refs, n_seg, n_groups, rep, dk, dv, scale):
    q_ref = refs[0]
    segs = [(refs[1 + 2 * s], refs[2 + 2 * s]) for s in range(n_seg)]
    o_ref = refs[1 + 2 * n_seg]
    nt = (((1,), (1,)), ((), ()))
    for g in range(n_groups):
        ks = [k_ref[:, g * dk:(g + 1) * dk] for k_ref, _ in segs]
        vs = []
        for _, v_ref in segs:
            one = (lax.broadcasted_iota(jnp.int32, (v_ref.shape[0], LANES), 1) == 0).astype(BF16)
            vs.append(jnp.concatenate([v_ref[:, g * dv:(g + 1) * dv], one], axis=1))
        for r in range(rep):
            hq = g * rep + r
            q = q_ref[:, hq * dk:(hq + 1) * dk]
            ss = [lax.dot_general(q, k, nt, preferred_element_type=F32) for k in ks]
            if scale != 1.0:
                ss = [s * scale for s in ss]
            m = functools.reduce(jnp.maximum, [jnp.max(s, axis=-1, keepdims=True) for s in ss])
            acc = None
            for s, v in zip(ss, vs):
                a = jnp.dot(jnp.exp(s - m).astype(BF16), v, preferred_element_type=F32)
                acc = a if acc is None else acc + a
            o_ref[:, hq * dv:(hq + 1) * dv] = (acc[:, :dv] / acc[:, dv:dv + 1]).astype(o_ref.dtype)


def _attention(q, q_off, segs, bsz, n_q_per_batch, q_row0, *, n_groups, rep, dk, dv, scale, name):
    tq = min(512, n_q_per_batch)
    nq = n_q_per_batch // tq
    qw = n_groups * rep * dk
    ow = n_groups * rep * dv

    def q_map(b, i):
        return ((q_row0 + b * n_q_per_batch) // tq + i, q_off // qw)

    in_specs = [pl.BlockSpec((tq, qw), q_map)]
    args = [q]
    for k_arr, k_off, v_arr, v_off, length, row0 in segs:
        kw, vw = n_groups * dk, n_groups * dv
        in_specs.append(pl.BlockSpec((length, kw), functools.partial(
            lambda b, i, length, row0, blk: (row0 // length + b, blk), length=length, row0=row0, blk=k_off // kw)))
        in_specs.append(pl.BlockSpec((length, vw), functools.partial(
            lambda b, i, length, row0, blk: (row0 // length + b, blk), length=length, row0=row0, blk=v_off // vw)))
        args += [k_arr, v_arr]
    return pl.pallas_call(
        functools.partial(_attn_kernel, n_seg=len(segs), n_groups=n_groups, rep=rep, dk=dk, dv=dv, scale=scale),
        grid=(bsz, nq),
        in_specs=in_specs,
        out_specs=pl.BlockSpec((tq, ow), lambda b, i: (b * nq + i, 0)),
        out_shape=jax.ShapeDtypeStruct((bsz * n_q_per_batch, ow), BF16),
        compiler_params=_params("arbitrary", "arbitrary"),
        name=name,
    )(*args)


def _na_kernel(q_ref, k_ref, v_ref, kc_ref, vc_ref, bias_ref, o_ref):
    j = pl.program_id(0)
    k_rows = NA_K_ROWS * GRID_W
    start = jnp.clip(NA_Q_ROWS * j - NA_WIN_H // 2, 0, SEQ // GRID_W - NA_K_ROWS) * GRID_W
    start = pl.multiple_of(start, 256)
    hd = NA_HEAD_DIM

    def with_ones(v):
        one = (lax.broadcasted_iota(jnp.int32, (v.shape[0], LANES), 1) == 0).astype(BF16)
        return jnp.concatenate([v, one], axis=1)

    for h in range(NA_HEADS):
        cols = slice(h * hd, (h + 1) * hd)
        q = q_ref[:, cols]
        s1 = lax.dot_general(q, k_ref[pl.ds(start, k_rows), cols], NT_DIMS, preferred_element_type=F32) * NA_SCALE
        s1 = s1 + bias_ref[0, h]
        s2 = lax.dot_general(q, kc_ref[:, cols], NT_DIMS, preferred_element_type=F32) * NA_SCALE
        m = jnp.maximum(jnp.max(s1, axis=-1, keepdims=True), jnp.max(s2, axis=-1, keepdims=True))
        acc = jnp.dot(jnp.exp(s1 - m).astype(BF16), with_ones(v_ref[pl.ds(start, k_rows), cols]),
                      preferred_element_type=F32)
        acc = acc + jnp.dot(jnp.exp(s2 - m).astype(BF16), with_ones(vc_ref[:, cols]), preferred_element_type=F32)
        o_ref[:, cols] = (acc[:, :hd] / acc[:, hd:hd + 1]).astype(o_ref.dtype)


def _na_bias_kernel(tp_ref, o_ref):
    t = pl.program_id(0)
    rows = SEQ // GRID_W
    k0 = jnp.clip(NA_Q_ROWS * t - NA_WIN_H // 2, 0, rows - NA_K_ROWS)
    left = lax.broadcasted_iota(jnp.int32, (GRID_W, 2 * GRID_W), 1) < GRID_W
    neg = -jnp.inf
    for a in range(NA_Q_ROWS):
        rq = NA_Q_ROWS * t + a
        r0 = jnp.clip(rq - NA_WIN_H // 2, 0, rows - NA_WIN_H)
        for b2 in range(NA_K_ROWS // 2):
            rk = k0 + 2 * b2
            ok_l = (rk >= r0) & (rk < r0 + NA_WIN_H)
            ok_r = (rk + 1 >= r0) & (rk + 1 < r0 + NA_WIN_H)
            pair = tp_ref[0, jnp.clip(rk - rq + NA_WIN_H, 0, 2 * NA_WIN_H - 1)]
            keep = jnp.where(left, ok_l.astype(jnp.int32), ok_r.astype(jnp.int32)) > 0
            o_ref[0, 0, a * GRID_W:(a + 1) * GRID_W, b2 * 2 * GRID_W:(b2 + 1) * 2 * GRID_W] = jnp.where(keep, pair, neg)


def _na_bias(rpb):
    n_dr = 2 * NA_WIN_H - 1
    cols = jnp.arange(GRID_W)
    c0 = jnp.clip(cols - NA_WIN_W // 2, 0, GRID_W - NA_WIN_W)
    col_ok = (cols[None, :] >= c0[:, None]) & (cols[None, :] < c0[:, None] + NA_WIN_W)
    dc = jnp.clip(cols[None, :] - cols[:, None] + NA_WIN_W - 1, 0, 2 * NA_WIN_W - 2)
    oh_c = (dc[:, :, None] == jnp.arange(2 * NA_WIN_W - 1)).astype(F32)
    by_col = jnp.einsum('hdc,qkc->hdqk', rpb.astype(F32), oh_c, precision=lax.Precision.HIGHEST)
    by_col = jnp.where(col_ok[None, None], by_col, -jnp.inf)
    zero = jnp.zeros((NA_HEADS, 1, GRID_W, GRID_W), F32)
    padded = jnp.concatenate([zero, by_col, zero], axis=1)
    pairs = jnp.concatenate([padded[:, :n_dr + 1], padded[:, 1:]], axis=-1)
    n_tiles = SEQ // GRID_W // NA_Q_ROWS
    tq, tk = NA_Q_ROWS * GRID_W, NA_K_ROWS * GRID_W
    return pl.pallas_call(
        _na_bias_kernel,
        grid=(n_tiles, NA_HEADS),
        in_specs=[pl.BlockSpec((1, n_dr + 1, GRID_W, 2 * GRID_W), lambda t, h: (h, 0, 0, 0))],
        out_specs=pl.BlockSpec((1, 1, tq, tk), lambda t, h: (t, h, 0, 0)),
        out_shape=jax.ShapeDtypeStruct((n_tiles, NA_HEADS, tq, tk), F32),
        compiler_params=_params("arbitrary", "arbitrary"),
        name="na_bias",
    )(pairs)


def _na_attention(p, bias, bsz, n_lat):
    tq = NA_Q_ROWS * GRID_W
    n_tiles = SEQ // tq
    w = NA_HEADS * NA_HEAD_DIM
    qb, kb, vb = P_C // w, (P_C + w) // w, (P_C + 2 * w) // w
    return pl.pallas_call(
        _na_kernel,
        grid=(n_tiles, bsz),
        in_specs=[
            pl.BlockSpec((tq, w), lambda j, b: (b * n_tiles + j, qb)),
            pl.BlockSpec((SEQ, w), lambda j, b: (b, kb)),
            pl.BlockSpec((SEQ, w), lambda j, b: (b, vb)),
            pl.BlockSpec((CTX_LEN, w), lambda j, b: (n_lat // CTX_LEN + b, kb)),
            pl.BlockSpec((CTX_LEN, w), lambda j, b: (n_lat // CTX_LEN + b, vb)),
            pl.BlockSpec((1, NA_HEADS, tq, NA_K_ROWS * GRID_W), lambda j, b: (j, 0, 0, 0)),
        ],
        out_specs=pl.BlockSpec((tq, w), lambda j, b: (b * n_tiles + j, 0)),
        out_shape=jax.ShapeDtypeStruct((n_lat, w), BF16),
        compiler_params=_params("arbitrary", "arbitrary"),
        name="na_attention",
    )(p, p, p, p, p, bias)


def _conv_kernel(b_ref, c_ref, u_ref, w_ref, o_ref):
    length = c_ref.shape[0]
    z = c_ref[...].astype(F32) * u_ref[...].astype(F32)
    row = lax.broadcasted_iota(jnp.int32, z.shape, 0)
    z_prev = jnp.where(row == 0, 0.0, pltpu.roll(z, 1, axis=0))
    z_next = jnp.where(row == length - 1, 0.0, pltpu.roll(z, length - 1, axis=0))
    conv = z_prev * w_ref[0:1, :] + z * w_ref[1:2, :] + z_next * w_ref[2:3, :]
    o_ref[...] = (b_ref[...].astype(F32) * conv).astype(o_ref.dtype)


def _short_conv(p, w_conv8, bsz, length, row0):
    blk = P_D // CONV_CH

    def rows(b):
        return row0 // length + b

    return pl.pallas_call(
        _conv_kernel,
        grid=(bsz,),
        in_specs=[
            pl.BlockSpec((length, CONV_CH), lambda b: (rows(b), blk)),
            pl.BlockSpec((length, CONV_CH), lambda b: (rows(b), blk + 1)),
            pl.BlockSpec((length, CONV_CH), lambda b: (rows(b), blk + 2)),
            pl.BlockSpec((8, CONV_CH), lambda b: (0, 0)),
        ],
        out_specs=pl.BlockSpec((length, CONV_CH), lambda b: (b, 0)),
        out_shape=jax.ShapeDtypeStruct((bsz * length, CONV_CH), BF16),
        compiler_params=_params("arbitrary"),
        name="short_conv",
    )(p, p, p, w_conv8)


MERGE_CHUNK = 512


def _merge_kernel(*refs, n_lat_tiles):
    o_ref, wb_ref = refs[-1], refs[-2]
    g_refs = refs[-2 - N_BRANCHES:-2]
    y_sets = [refs[s:s + N_BRANCHES] for s in range(0, len(refs) - 2 - N_BRANCHES, N_BRANCHES)]

    def emit(y_refs):
        for c in range(D_MODEL // MERGE_CHUNK):
            cols = slice(c * MERGE_CHUNK, (c + 1) * MERGE_CHUNK)
            acc = None
            for k, (y_ref, g_ref) in enumerate(zip(y_refs, g_refs)):
                proj = jnp.dot(y_ref[...], wb_ref[0, k, :, cols], preferred_element_type=F32)
                gate = 0.5 * jnp.tanh(0.5 * g_ref[:, cols].astype(F32)) + 0.5
                acc = gate * proj if acc is None else acc + gate * proj
            o_ref[:, cols] = acc.astype(o_ref.dtype)

    if len(y_sets) == 1:
        emit(y_sets[0])
    else:
        i = pl.program_id(0)

        @pl.when(i < n_lat_tiles)
        def _():
            emit(y_sets[0])

        @pl.when(i >= n_lat_tiles)
        def _():
            emit(y_sets[1])


def _merge(ys_lat, ys_ctx, p, w_branch, layer):
    n_lat = ys_lat[0].shape[0]
    n_act = n_lat + (ys_ctx[0].shape[0] if ys_ctx else 0)
    tm = _pick(ys_ctx[0].shape[0] if ys_ctx else n_lat, (512, 256))
    n_lat_tiles = n_lat // tm
    y_specs = [pl.BlockSpec((tm, BRANCH_WIDTH), lambda i: (jnp.minimum(i, n_lat_tiles - 1), 0))] * N_BRANCHES
    if ys_ctx:
        y_specs += [pl.BlockSpec((tm, BRANCH_WIDTH), lambda i: (jnp.maximum(i - n_lat_tiles, 0), 0))] * N_BRANCHES
    g_specs = [pl.BlockSpec((tm, D_MODEL), functools.partial(
        lambda i, k: (i, P_G // D_MODEL + k), k=k)) for k in range(N_BRANCHES)]
    return pl.pallas_call(
        functools.partial(_merge_kernel, n_lat_tiles=n_lat_tiles),
        grid=(n_act // tm,),
        in_specs=y_specs + g_specs + [
            pl.BlockSpec((1, N_BRANCHES, BRANCH_WIDTH, D_MODEL), lambda i: (layer, 0, 0, 0))],
        out_specs=pl.BlockSpec((tm, D_MODEL), lambda i: (i, 0)),
        out_shape=jax.ShapeDtypeStruct((n_act, D_MODEL), BF16),
        compiler_params=_params("arbitrary"),
        name="branch_merge",
    )(*ys_lat, *ys_ctx, p, p, p, p, w_branch)


R_E, R_W, R_RANK = 0, 2, 4


def _router_kernel(x_ref, g_ref, sh_ref, sc_ref, wr_ref, br_ref, h_ref, route_ref, counts_ref, run_ref,
                   *, tm, bsz):
    step = pl.program_id(0)
    r = _mod_row(step, tm, bsz)
    h = _modulated(x_ref[...], g_ref[...], sh_ref, sc_ref, r)
    h_ref[...] = h
    h_hi = h.astype(BF16)
    h_lo = (h - h_hi.astype(F32)).astype(BF16)
    logits = jnp.dot(h_hi, wr_ref[0], preferred_element_type=F32) + (
        jnp.dot(h_lo, wr_ref[0], preferred_element_type=F32) + jnp.dot(h_hi, wr_ref[1], preferred_element_type=F32))
    bias = br_ref[...]
    lane = lax.broadcasted_iota(jnp.int32, logits.shape, 1)
    lane_f = lane.astype(F32)
    neg = -jnp.inf

    def rmax(x, mask):
        return jnp.max(jnp.where(mask, x, neg), axis=-1, keepdims=True)

    def first_at(x, val, mask):
        return jnp.min(jnp.where(mask & (x == val), lane_f, 2.0 * LANES), axis=-1, keepdims=True)

    def pick(x, idx):
        return jnp.sum(jnp.where(lane_f == idx, x, 0.0), axis=-1, keepdims=True)

    gmask = lane < N_GROUPS
    ge = jnp.where(gmask, jnp.exp(logits - rmax(logits, gmask)), 0.0)
    g_prob = ge / jnp.sum(ge, axis=-1, keepdims=True)
    g_score = g_prob + bias
    g_sel = first_at(g_score, rmax(g_score, gmask), gmask)
    p_group = pick(g_prob, g_sel)

    group_of_lane = lax.shift_right_arithmetic(lane - N_EXPERTS, 3).astype(F32)
    emask = (lane >= N_EXPERTS) & (group_of_lane == g_sel)
    ee = jnp.where(emask, jnp.exp(logits - rmax(logits, emask)), 0.0)
    e_prob = ee / jnp.sum(ee, axis=-1, keepdims=True)
    e_score = e_prob + bias
    i1 = first_at(e_score, rmax(e_score, emask), emask)
    mask2 = emask & (lane_f != i1)
    i2 = first_at(e_score, rmax(e_score, mask2), mask2)
    p1 = pick(e_prob, i1)
    p2 = pick(e_prob, i2)
    w1 = p_group * p1 / (p1 + p2)
    w2 = p_group * p2 / (p1 + p2)
    e1 = i1 - N_EXPERTS
    e2 = i2 - N_EXPERTS

    @pl.when(step == 0)
    def _():
        run_ref[...] = jnp.zeros_like(run_ref)

    oh1 = lane_f == e1
    oh2 = lane_f == e2
    both = oh1.astype(F32) + oh2.astype(F32)
    tri = (lax.broadcasted_iota(jnp.int32, (tm, tm), 1) < lax.broadcasted_iota(jnp.int32, (tm, tm), 0)).astype(BF16)
    before = jnp.dot(tri, both.astype(BF16), preferred_element_type=F32) + run_ref[0:1, :]
    rank1 = jnp.sum(jnp.where(oh1, before, 0.0), axis=-1, keepdims=True)
    rank2 = jnp.sum(jnp.where(oh2, before + oh1.astype(F32), 0.0), axis=-1, keepdims=True)
    run_ref[...] = run_ref[...] + jnp.sum(both, axis=0, keepdims=True)
    counts_ref[...] = run_ref[...]

    vals = (e1, e2, w1, w2, rank1, rank2)
    out = jnp.zeros_like(logits)
    for k, v in enumerate(vals):
        out = jnp.where(lane == k, v, out)
    route_ref[...] = out


def _ffn_input_and_route(x1, gain, mod, w_route, b_route, bsz):
    n = x1.shape[0]
    tm = _pick(n, (512, 256))
    rows = mod.shape[0]
    return pl.pallas_call(
        functools.partial(_router_kernel, tm=tm, bsz=bsz),
        grid=(n // tm,),
        in_specs=[
            pl.BlockSpec((tm, D_MODEL), lambda i: (i, 0)),
            pl.BlockSpec((1, D_MODEL), lambda i: (0, 0)),
            pl.BlockSpec((rows, D_MODEL), lambda i: (0, 3)),
            pl.BlockSpec((rows, D_MODEL), lambda i: (0, 4)),
            pl.BlockSpec((2, D_MODEL, LANES), lambda i: (0, 0, 0)),
            pl.BlockSpec((1, LANES), lambda i: (0, 0)),
        ],
        out_specs=[pl.BlockSpec((tm, D_MODEL), lambda i: (i, 0)),
                   pl.BlockSpec((tm, LANES), lambda i: (i, 0)),
                   pl.BlockSpec((8, LANES), lambda i: (0, 0))],
        out_shape=[jax.ShapeDtypeStruct((n, D_MODEL), F32),
                   jax.ShapeDtypeStruct((n, LANES), F32),
                   jax.ShapeDtypeStruct((8, LANES), F32)],
        scratch_shapes=[pltpu.VMEM((8, LANES), F32)],
        compiler_params=_params("arbitrary"),
        name="ffn_input_route",
    )(x1, gain.reshape(1, D_MODEL), mod, mod, w_route, b_route)


def _dispatch_indices(route, counts_row, n_tok):
    counts = counts_row[0, :N_EXPERTS].astype(jnp.int32)
    padded = (counts + MOE_BLOCK - 1) // MOE_BLOCK * MOE_BLOCK
    pad_end = jnp.cumsum(padded)
    pad_start = pad_end - padded
    eid = route[:, R_E:R_E + TOP_K].astype(jnp.int32)
    rank = route[:, R_RANK:R_RANK + TOP_K].astype(jnp.int32)
    onehot = eid[:, :, None] == jnp.arange(N_EXPERTS, dtype=jnp.int32)[None, None, :]
    pos = (jnp.sum(jnp.where(onehot, pad_start[None, None, :], 0), axis=-1) + rank).reshape(-1)
    n_blocks = (n_tok * TOP_K + N_EXPERTS * (MOE_BLOCK - 1)) // MOE_BLOCK
    blk_start = jnp.arange(n_blocks, dtype=jnp.int32) * MOE_BLOCK
    blk_e = jnp.minimum(jnp.sum(pad_end[None, :] <= blk_start[:, None], axis=1), N_EXPERTS - 1).astype(jnp.int32)
    n_used = (pad_end[-1:] // MOE_BLOCK).astype(jnp.int32)
    last_blk = jnp.where(counts > 0, pad_end // MOE_BLOCK - 1, -1).astype(jnp.int32)
    blk = jnp.arange(n_blocks, dtype=jnp.int32)
    prev_e = jnp.concatenate([jnp.full((1,), -1, jnp.int32), blk_e[:-1]])
    first = (blk < n_used) & (blk_e != prev_e)
    ordinal = jnp.cumsum(first.astype(jnp.int32)) - 1
    n_first = jnp.sum(first.astype(jnp.int32))
    k = jnp.arange(N_EXPERTS, dtype=jnp.int32)
    expert_at = jnp.sum(jnp.where(first[None, :] & (ordinal[None, :] == k[:, None]), blk_e[None, :], 0), axis=1)

    def expert_after(d):
        target = ordinal + d
        e = jnp.sum(jnp.where(k[None, :] == target[:, None], expert_at[None, :], 0), axis=1)
        return jnp.where(target < n_first, e, -1).astype(jnp.int32)

    sched = (blk_e, n_used, first.astype(jnp.int32), (ordinal % MOE_W_SLOTS).astype(jnp.int32),
             expert_after(1), expert_after(2))
    return sched, last_blk, pos.astype(jnp.int32), n_blocks


def _row_copy(src, src_row, dst, dst_row, sem):
    return pltpu.make_async_copy(src.at[pl.ds(src_row, 1), :], dst.at[pl.ds(dst_row, 1), :], sem)


DISPATCH_CHUNK = 512
DMA_UNROLL = 8


def _dispatch_kernel(pos_ref, last_ref, nused_ref, h_ref, xs_hbm, zbuf, sem, zsem):
    base = pl.program_id(0) * DISPATCH_CHUNK
    n_blocks = xs_hbm.shape[0] // MOE_BLOCK

    @pl.when(pl.program_id(0) == 0)
    def _():
        zbuf[...] = jnp.zeros_like(zbuf)

        def fill(b):
            return pltpu.make_async_copy(zbuf, xs_hbm.at[pl.ds(pl.multiple_of(b * MOE_BLOCK, MOE_BLOCK), MOE_BLOCK), :],
                                         zsem.at[0])

        def each(fn):
            def expert(e, carry):
                @pl.when(last_ref[e] >= 0)
                def _():
                    fn(fill(last_ref[e]))
                return carry

            def unused(b, carry):
                fn(fill(b))
                return carry

            lax.fori_loop(0, N_EXPERTS, expert, 0)
            lax.fori_loop(nused_ref[0], n_blocks, unused, 0)

        each(lambda c: c.start())
        each(lambda c: c.wait())

    def body(r, carry):
        for k in range(TOP_K):
            _row_copy(h_ref, r, xs_hbm, pos_ref[(base + r) * TOP_K + k], sem.at[0]).start(priority=k % 2)
        return carry

    lax.fori_loop(0, DISPATCH_CHUNK, body, 0, unroll=DMA_UNROLL)
    for k in range(TOP_K):
        pltpu.make_async_copy(h_ref, xs_hbm.at[pl.ds(0, DISPATCH_CHUNK), :], sem.at[0]).wait()


def _dispatch(h2, pos, last_blk, n_used, n_rows):
    n, width = h2.shape
    assert n % DISPATCH_CHUNK == 0
    grid_spec = pltpu.PrefetchScalarGridSpec(
        num_scalar_prefetch=3,
        grid=(n // DISPATCH_CHUNK,),
        in_specs=[pl.BlockSpec((DISPATCH_CHUNK, width), lambda i, *_: (i, 0))],
        out_specs=pl.BlockSpec(memory_space=pl.ANY),
        scratch_shapes=[pltpu.VMEM((MOE_BLOCK, width), h2.dtype), pltpu.SemaphoreType.DMA((1,)),
                        pltpu.SemaphoreType.DMA((1,))],
    )
    return pl.pallas_call(
        _dispatch_kernel,
        grid_spec=grid_spec,
        out_shape=jax.ShapeDtypeStruct((n_rows, width), h2.dtype),
        compiler_params=_params("arbitrary"),
        name="moe_dispatch",
    )(pos, last_blk, n_used, h2)


def _moe_kernel(be_ref, nused_ref, first_ref, slot_ref, nxt1_ref, nxt2_ref, x_ref, wg_hbm, wu_hbm, wd_hbm, o_ref,
                wgf, wuf, wdf, wgb, wub, wdb, sem, *, layer):
    i = pl.program_id(0)

    def w_copies(e, s):
        return (pltpu.make_async_copy(wg_hbm.at[layer, e], wgf.at[s], sem.at[s]),
                pltpu.make_async_copy(wu_hbm.at[layer, e], wuf.at[s], sem.at[s]),
                pltpu.make_async_copy(wd_hbm.at[layer, e], wdf.at[s], sem.at[s]))

    def start(e, s):
        for c in w_copies(e, s):
            c.start(priority=BULK_DMA_PRIORITY)

    @pl.when(first_ref[i] == 1)
    def _():
        s = slot_ref[i]

        @pl.when(i == 0)
        def _():
            start(be_ref[i], s)

            @pl.when(nxt1_ref[i] >= 0)
            def _():
                start(nxt1_ref[i], (s + 1) % MOE_W_SLOTS)

        @pl.when(nxt2_ref[i] >= 0)
        def _():
            start(nxt2_ref[i], (s + 2) % MOE_W_SLOTS)

        for c in w_copies(be_ref[i], s):
            c.wait()
        wgb[...] = wgf[s].astype(BF16)
        wub[...] = wuf[s].astype(BF16)
        wdb[...] = wdf[s].astype(BF16)

    @pl.when(i < nused_ref[0])
    def _():
        x = x_ref[...].astype(BF16)
        gate = jnp.dot(x, wgb[...], preferred_element_type=F32)
        up = jnp.dot(x, wub[...], preferred_element_type=F32)
        hid = (gate * jax.nn.sigmoid(gate) * up).astype(BF16)
        o_ref[...] = jnp.dot(hid, wdb[...], preferred_element_type=F32)

    @pl.when(i >= nused_ref[0])
    def _():
        o_ref[...] = jnp.zeros_like(o_ref)


def _moe(xs, sched, w_gate, w_up, w_down, layer):
    n_blocks = sched[0].shape[0]
    grid_spec = pltpu.PrefetchScalarGridSpec(
        num_scalar_prefetch=len(sched),
        grid=(n_blocks,),
        in_specs=[pl.BlockSpec((MOE_BLOCK, D_MODEL), lambda i, *_: (i, 0))] + [pl.BlockSpec(memory_space=pl.ANY)] * 3,
        out_specs=pl.BlockSpec((MOE_BLOCK, D_MODEL), lambda i, *_: (i, 0)),
        scratch_shapes=[
            pltpu.VMEM((MOE_W_SLOTS, D_MODEL, EXPERT_HIDDEN), F32),
            pltpu.VMEM((MOE_W_SLOTS, D_MODEL, EXPERT_HIDDEN), F32),
            pltpu.VMEM((MOE_W_SLOTS, EXPERT_HIDDEN, D_MODEL), F32),
            pltpu.VMEM((D_MODEL, EXPERT_HIDDEN), BF16),
            pltpu.VMEM((D_MODEL, EXPERT_HIDDEN), BF16),
            pltpu.VMEM((EXPERT_HIDDEN, D_MODEL), BF16),
            pltpu.SemaphoreType.DMA((MOE_W_SLOTS,)),
        ],
    )
    return pl.pallas_call(
        functools.partial(_moe_kernel, layer=layer),
        grid_spec=grid_spec,
        out_shape=jax.ShapeDtypeStruct((n_blocks * MOE_BLOCK, D_MODEL), F32),
        compiler_params=_params("arbitrary"),
        name="moe_experts",
    )(*sched, xs, w_gate, w_up, w_down)


def _combine_kernel(pos_ref, ys_hbm, x_ref, route_ref, ga_ref, gn_ref, *rest, tm, bsz, final):
    if final:
        o_ref, rbuf, sem = rest
    else:
        nsh_ref, nsc_ref, o_ref, h_ref, rbuf, sem = rest
    i = pl.program_id(0)
    n = pl.num_programs(0)
    slot = i % 2

    def gather(blk, s):
        def body(r, carry):
            for k in range(TOP_K):
                _row_copy(ys_hbm, pos_ref[(blk * tm + r) * TOP_K + k], rbuf.at[s], k * tm + r,
                          sem.at[s]).start(priority=k % 2)
            return carry
        lax.fori_loop(0, tm, body, 0, unroll=DMA_UNROLL)

    @pl.when(i == 0)
    def _():
        gather(0, 0)

    @pl.when(i + 1 < n)
    def _():
        gather(i + 1, 1 - slot)

    pltpu.make_async_copy(ys_hbm.at[pl.ds(0, TOP_K * tm), :], rbuf.at[slot], sem.at[slot]).wait()
    y = rbuf[slot, 0:tm, :] * route_ref[:, R_W:R_W + 1] + rbuf[slot, tm:2 * tm, :] * route_ref[:, R_W + 1:R_W + 2]
    r = _mod_row(i, tm, bsz)
    x = x_ref[...] + ga_ref[pl.ds(r, 1), :] * y
    if final:
        o_ref[...] = _rms(x, gn_ref[...])
    else:
        o_ref[...] = x
        h_ref[...] = _modulated(x, gn_ref[...], nsh_ref, nsc_ref, r).astype(h_ref.dtype)


def _combine(ys, pos, x1, route, mod, gain, next_mod, bsz):
    n = x1.shape[0]
    tm = 128
    rows = mod.shape[0]
    final = next_mod is None
    tile = pl.BlockSpec((tm, D_MODEL), lambda i, pos: (i, 0))
    in_specs = [
        pl.BlockSpec(memory_space=pl.ANY),
        tile,
        pl.BlockSpec((tm, LANES), lambda i, pos: (i, 0)),
        pl.BlockSpec((rows, D_MODEL), lambda i, pos: (0, 5)),
        pl.BlockSpec((1, D_MODEL), lambda i, pos: (0, 0)),
    ]
    args = [pos, ys, x1, route, mod, gain.reshape(1, D_MODEL)]
    out_specs, out_shape = tile, jax.ShapeDtypeStruct((n, D_MODEL), F32)
    if not final:
        in_specs += [pl.BlockSpec((rows, D_MODEL), lambda i, pos: (0, 0)),
                     pl.BlockSpec((rows, D_MODEL), lambda i, pos: (0, 1))]
        args += [next_mod, next_mod]
        out_specs, out_shape = [tile, tile], [out_shape, jax.ShapeDtypeStruct((n, D_MODEL), BF16)]
    grid_spec = pltpu.PrefetchScalarGridSpec(
        num_scalar_prefetch=1,
        grid=(n // tm,),
        in_specs=in_specs,
        out_specs=out_specs,
        scratch_shapes=[pltpu.VMEM((2, TOP_K * tm, D_MODEL), F32), pltpu.SemaphoreType.DMA((2,))],
    )
    return pl.pallas_call(
        functools.partial(_combine_kernel, tm=tm, bsz=bsz, final=final),
        grid_spec=grid_spec,
        out_shape=out_shape,
        compiler_params=_params("arbitrary"),
        name="moe_combine",
    )(*args)


def _rotate_half_cols(w, quarter):
    r1, r2, c1, c2 = (w[..., k * quarter:(k + 1) * quarter] for k in range(4))
    return jnp.concatenate([-r2, r1, -c2, c1], axis=-1)


def _kpe_rows(wt):
    kpe = jnp.swapaxes(wt[:, MLA_Q_LORA + MLA_KV_LORA:A_COLS, :], 1, 2)
    both = jnp.concatenate([kpe, _rotate_half_cols(kpe, MLA_ROPE // 4)], axis=-1)
    return jnp.swapaxes(both, 1, 2).astype(BF16)


def _relayout_w_uq(w_uq):
    w = w_uq.reshape(MLA_Q_LORA, MLA_HEADS, MLA_NOPE + MLA_ROPE)
    nope, pe = w[..., :MLA_NOPE], w[..., MLA_NOPE:]
    z = jnp.zeros_like(pe)
    out = jnp.concatenate([nope, pe, z, _rotate_half_cols(pe, MLA_ROPE // 4), z], axis=-1)
    return out.reshape(MLA_Q_LORA, MLA_HEADS * 3 * LANES).astype(BF16)


def _relayout_w_ukv(w_ukv):
    w = w_ukv.reshape(MLA_KV_LORA, MLA_HEADS, MLA_NOPE + MLA_V)
    return jnp.concatenate([w[..., :MLA_NOPE].reshape(MLA_KV_LORA, -1),
                            w[..., MLA_NOPE:].reshape(MLA_KV_LORA, -1)], axis=-1).astype(BF16)


def _rope_tables(tm):
    t = jnp.arange(SEQ)
    rows = (t // GRID_W).astype(F32)
    cols = (t % GRID_W).astype(F32)

    def cs(rot_dim):
        quarter = rot_dim // 4
        inv_freq = ROPE_THETA ** (-jnp.arange(quarter, dtype=F32) / quarter)
        ang_r = rows[:, None] * inv_freq[None, :]
        ang_c = cols[:, None] * inv_freq[None, :]
        cos = jnp.concatenate([jnp.cos(ang_r)] * 2 + [jnp.cos(ang_c)] * 2, axis=-1)
        sin = jnp.concatenate([jnp.sin(ang_r)] * 2 + [jnp.sin(ang_c)] * 2, axis=-1)
        return cos, sin

    def with_identity(tab, ident):
        return jnp.concatenate([tab, jnp.broadcast_to(ident, (tm, LANES))], axis=0)

    cos_a, sin_a = cs(MLA_ROPE)
    zeros = jnp.zeros((SEQ, LANES - MLA_ROPE), F32)
    lane = jnp.arange(LANES)
    one_lo = (lane < MLA_ROPE).astype(F32)[None, :]
    cosa = with_identity(jnp.concatenate([cos_a, zeros], axis=1), one_lo)
    sina = with_identity(jnp.concatenate([sin_a, zeros], axis=1), jnp.zeros((1, LANES), F32))
    csa = with_identity(jnp.concatenate([cos_a, sin_a], axis=1), one_lo)
    cos_b, sin_b = cs(GQA_HEAD_DIM)
    sign = jnp.where((lane % (GQA_HEAD_DIM // 2)) < GQA_HEAD_DIM // 4, -1.0, 1.0).astype(F32)[None, :]
    cosb = with_identity(cos_b, jnp.ones((1, LANES), F32))
    sinb = with_identity(sin_b * sign, jnp.zeros((1, LANES), F32))
    return cosa, sina, csa, cosb, sinb


def kernel(x, c, ctx, c_ctx, w_mod, b_mod, g_mix, g_ffn, w_in, w_uq, g_qa, w_ukv, g_kva, g_qn, g_kn, rpb, w_conv,
           w_branch, w_o, w_group, b_group, w_router, b_router, w_gate_e, w_up_e, w_down_e, g_final):
    bsz = x.shape[0]
    n_lat = bsz * SEQ
    n_ctx = bsz * CTX_LEN
    n_all = n_lat + n_ctx
    mod_rows = -(-(bsz + 1) // 8) * 8

    c_all = jnp.concatenate([c, c_ctx[None, :], jnp.zeros((mod_rows - bsz - 1, D_MODEL), F32)], axis=0)
    mods = _modulation(c_all, bsz + 1, w_mod, b_mod)
    tables = _rope_tables(_pick(n_all, (512, 256)))

    w_in_t = jnp.swapaxes(w_in, 1, 2)
    w_kpe_t = _kpe_rows(w_in_t)
    w_branch_b = w_branch.astype(BF16)
    w_o_b = w_o.astype(BF16)

    for l in range(DEPTH):
        last = l == DEPTH - 1
        n_act = n_lat if last else n_all
        mod = mods[l]
        w_uq_x = _relayout_w_uq(w_uq[l])
        w_ukv_r = _relayout_w_ukv(w_ukv[l])
        w_route = jnp.concatenate([w_group[l], jnp.zeros((D_MODEL, N_EXPERTS - N_GROUPS), F32), w_router[l]], axis=1)
        w_route_hi = w_route.astype(BF16)
        w_route = jnp.stack([w_route_hi, (w_route - w_route_hi.astype(F32)).astype(BF16)])
        b_route = jnp.concatenate([b_group[l], jnp.zeros((N_EXPERTS - N_GROUPS,), F32), b_router[l]])[None, :]
        w_conv8 = jnp.concatenate([w_conv[l], jnp.zeros((8 - CONV_WIDTH, CONV_CH), F32)], axis=0)

        if l == 0:
            h, x_all = _hmod(x.reshape(n_lat, D_MODEL), ctx.reshape(n_ctx, D_MODEL), g_mix[l], mod, bsz)
        p = _in_proj(h, w_in_t, l)
        pk = _matmul_nt(h, w_kpe_t, l, BF16, "kpe_proj")
        qa, ka, va, qb, kb = _prep(p, pk, tables, w_uq_x, w_ukv_r, g_qa[l][None, :], g_kva[l][None, :],
                                   g_qn[l][None, :], g_kn[l][None, :])

        lat_ctx = lambda k_arr, k_off, v_arr, v_off: [(k_arr, k_off, v_arr, v_off, SEQ, 0),
                                                      (k_arr, k_off, v_arr, v_off, CTX_LEN, n_lat)]
        ctx_only = lambda k_arr, k_off, v_arr, v_off: [(k_arr, k_off, v_arr, v_off, CTX_LEN, n_lat)]
        mla = dict(n_groups=MLA_HEADS, rep=1, dk=MLA_DK_PAD, dv=MLA_V, scale=1.0)
        gqa = dict(n_groups=GQA_KV_HEADS, rep=GQA_HEADS // GQA_KV_HEADS, dk=GQA_HEAD_DIM, dv=GQA_HEAD_DIM, scale=1.0)
        nac = dict(n_groups=NA_HEADS, rep=1, dk=NA_HEAD_DIM, dv=NA_HEAD_DIM, scale=NA_SCALE)

        ys_lat = [
            _attention(qa, 0, lat_ctx(ka, 0, va, 0), bsz, SEQ, 0, name="mla_attention", **mla),
            _attention(qb, 0, lat_ctx(kb, 0, p, P_B + 768), bsz, SEQ, 0, name="gqa_attention", **gqa),
            _na_attention(p, _na_bias(rpb[l]), bsz, n_lat),
            _short_conv(p, w_conv8, bsz, SEQ, 0),
        ]
        ys_ctx = []
        if not last:
            ys_ctx = [
                _attention(qa, 0, ctx_only(ka, 0, va, 0), bsz, CTX_LEN, n_lat, name="mla_attention_ctx", **mla),
                _attention(qb, 0, ctx_only(kb, 0, p, P_B + 768), bsz, CTX_LEN, n_lat,
                           name="gqa_attention_ctx", **gqa),
                _attention(p, P_C, ctx_only(p, P_C + 512, p, P_C + 1024), bsz, CTX_LEN, n_lat,
                           name="na_attention_ctx", **nac),
                _short_conv(p, w_conv8, bsz, CTX_LEN, n_lat),
            ]

        mixed = _merge(ys_lat, ys_ctx, p, w_branch_b, l)
        x1 = _out_proj_residual(mixed, w_o_b, l, x_all, mod, bsz)
        h2, route, counts = _ffn_input_and_route(x1, g_ffn[l], mod, w_route, b_route, bsz)
        sched, last_blk, pos, n_blocks = _dispatch_indices(route, counts, n_act)
        xs = _dispatch(h2, pos, last_blk, sched[1], n_blocks * MOE_BLOCK)
        ys = _moe(xs, sched, w_gate_e, w_up_e, w_down_e, l)
        if last:
            x_all = _combine(ys, pos, x1, route, mod, g_final, None, bsz)
        else:
            x_all, h = _combine(ys, pos, x1, route, mod, g_mix[l + 1], mods[l + 1], bsz)
    return x_all.reshape(bsz, SEQ, D_MODEL)
```

```python
import functools

import jax
import jax.numpy as jnp
from jax import lax
from jax.experimental import pallas as pl
from jax.experimental.pallas import tpu as pltpu

F32 = jnp.float32
BF16 = jnp.bfloat16

D_MODEL = 2048
SEQ = 2048
DEPTH = 2
CTX_LEN = 256
GRID_W = 64
ROPE_THETA = 10000.0
NORM_EPS = 1e-6
MOD_CHUNKS = 6

MLA_HEADS = 4
MLA_Q_LORA = 512
MLA_KV_LORA = 512
MLA_NOPE = 128
MLA_ROPE = 64
MLA_V = 128
MLA_SCALE = (MLA_NOPE + MLA_ROPE) ** -0.5
MLA_DK_PAD = 256

GQA_HEADS = 4
GQA_KV_HEADS = 2
GQA_HEAD_DIM = 128
GQA_SCALE = GQA_HEAD_DIM ** -0.5

NA_HEADS = 4
NA_HEAD_DIM = 128
NA_WIN_H = 8
NA_WIN_W = 16
NA_SCALE = NA_HEAD_DIM ** -0.5
NA_Q_ROWS = 8
NA_K_ROWS = 16

CONV_CH = 512
CONV_WIDTH = 3
BRANCH_WIDTH = 512
N_BRANCHES = 4

A_COLS = MLA_Q_LORA + MLA_KV_LORA + MLA_ROPE
B_COLS = (GQA_HEADS + 2 * GQA_KV_HEADS) * GQA_HEAD_DIM
C_COLS = 3 * NA_HEADS * NA_HEAD_DIM
D_COLS = 3 * CONV_CH
G_COLS = N_BRANCHES * D_MODEL

P_B = 0
P_C = P_B + B_COLS
P_D = P_C + C_COLS
P_G = P_D + D_COLS
P_CQ = P_G + G_COLS
P_CKV = P_CQ + MLA_Q_LORA
P_COLS = P_CKV + MLA_KV_LORA
IN_TILE = 1024

N_GROUPS = 8
EXPERTS_PER_GROUP = 8
N_EXPERTS = 64
TOP_K = 2
EXPERT_HIDDEN = 512
MOE_BLOCK = 128
MOE_W_SLOTS = 3

LANES = 128
BULK_DMA_PRIORITY = 1
VMEM_LIMIT = 56 * 1024 * 1024


def _params(*sem):
    return pltpu.CompilerParams(dimension_semantics=sem, vmem_limit_bytes=VMEM_LIMIT)


def _pick(n, cands):
    for c in cands:
        if n % c == 0:
            return c
    raise ValueError(f"no tile for {n}")


def _rms(x, g):
    return x * lax.rsqrt(jnp.mean(x * x, axis=-1, keepdims=True) + NORM_EPS) * g


def _mod_row(i, tm, bsz):
    return jnp.minimum(i * tm // SEQ, bsz)


def _mod_kernel(ct_ref, w_ref, b_ref, o_ref, *, n_vec):
    ct = ct_ref[...]
    st = ct * jax.nn.sigmoid(ct)
    w = w_ref[0]
    rows = [jnp.sum(w * st[:, r:r + 1], axis=0, keepdims=True) for r in range(n_vec)]
    rows += [jnp.zeros_like(rows[0])] * (o_ref.shape[1] - n_vec)
    o_ref[0] = jnp.concatenate(rows, axis=0) + b_ref[0]


def _modulation(c_all, n_vec, w_mod, b_mod):
    rows = c_all.shape[0]
    n = MOD_CHUNKS * D_MODEL
    tn = 1024
    return pl.pallas_call(
        functools.partial(_mod_kernel, n_vec=n_vec),
        grid=(DEPTH, n // tn),
        in_specs=[
            pl.BlockSpec((D_MODEL, rows), lambda l, j: (0, 0)),
            pl.BlockSpec((1, D_MODEL, tn), lambda l, j: (l, 0, j)),
            pl.BlockSpec((1, 1, tn), lambda l, j: (l, 0, j)),
        ],
        out_specs=pl.BlockSpec((1, rows, tn), lambda l, j: (l, 0, j)),
        out_shape=jax.ShapeDtypeStruct((DEPTH, rows, n), F32),
        compiler_params=_params("arbitrary", "arbitrary"),
        name="modulation",
    )(c_all.T, w_mod, b_mod.reshape(DEPTH, 1, n))


def _modulated(x, gain, sh_ref, sc_ref, r):
    return _rms(x, gain) * (1.0 + sc_ref[pl.ds(r, 1), :]) + sh_ref[pl.ds(r, 1), :]


def _hmod_kernel(x_ref, c_ref, g_ref, sh_ref, sc_ref, h_ref, xall_ref, *, tm, bsz, n_lat_tiles):
    i = pl.program_id(0)
    r = _mod_row(i, tm, bsz)

    def emit(src_ref):
        x = src_ref[...]
        xall_ref[...] = x
        h_ref[...] = _modulated(x, g_ref[...], sh_ref, sc_ref, r).astype(h_ref.dtype)

    @pl.when(i < n_lat_tiles)
    def _():
        emit(x_ref)

    @pl.when(i >= n_lat_tiles)
    def _():
        emit(c_ref)


def _hmod(x_lat, x_ctx, gain, mod, bsz):
    n_lat, n_ctx = x_lat.shape[0], x_ctx.shape[0]
    n = n_lat + n_ctx
    tm = _pick(n_ctx, (512, 256))
    n_lat_tiles = n_lat // tm
    rows = mod.shape[0]
    return pl.pallas_call(
        functools.partial(_hmod_kernel, tm=tm, bsz=bsz, n_lat_tiles=n_lat_tiles),
        grid=(n // tm,),
        in_specs=[
            pl.BlockSpec((tm, D_MODEL), lambda i: (jnp.minimum(i, n_lat_tiles - 1), 0)),
            pl.BlockSpec((tm, D_MODEL), lambda i: (jnp.maximum(i - n_lat_tiles, 0), 0)),
            pl.BlockSpec((1, D_MODEL), lambda i: (0, 0)),
            pl.BlockSpec((rows, D_MODEL), lambda i: (0, 0)),
            pl.BlockSpec((rows, D_MODEL), lambda i: (0, 1)),
        ],
        out_specs=[pl.BlockSpec((tm, D_MODEL), lambda i: (i, 0)), pl.BlockSpec((tm, D_MODEL), lambda i: (i, 0))],
        out_shape=[jax.ShapeDtypeStruct((n, D_MODEL), BF16), jax.ShapeDtypeStruct((n, D_MODEL), F32)],
        compiler_params=_params("arbitrary"),
        name="hmod",
    )(x_lat, x_ctx, gain.reshape(1, D_MODEL), mod, mod)


NT_DIMS = (((1,), (1,)), ((), ()))


def _mm_nt_kernel(a_ref, w_ref, o_ref):
    o_ref[...] = lax.dot_general(a_ref[...], w_ref[0], NT_DIMS, preferred_element_type=F32).astype(o_ref.dtype)


def _matmul_nt(a, wt, layer, out_dtype, name):
    m, k = a.shape
    n = wt.shape[1]
    tm = _pick(m, (1024, 512, 256))
    return pl.pallas_call(
        _mm_nt_kernel,
        grid=(m // tm,),
        in_specs=[pl.BlockSpec((tm, k), lambda i: (i, 0)), pl.BlockSpec((1, n, k), lambda i: (layer, 0, 0))],
        out_specs=pl.BlockSpec((tm, n), lambda i: (i, 0)),
        out_shape=jax.ShapeDtypeStruct((m, n), out_dtype),
        compiler_params=_params("arbitrary"),
        name=name,
    )(a, wt)


def _in_proj_kernel(a_ref, wt_hbm, o_ref, wbuf, wbf, sem, *, layer):
    j = pl.program_id(0)
    n_tiles = pl.num_programs(0)
    n_shift = P_CQ // IN_TILE

    def w_copy(jj, slot):
        row = pl.multiple_of(jnp.where(jj < n_shift, A_COLS + jj * IN_TILE, 0), MLA_ROPE)
        return pltpu.make_async_copy(wt_hbm.at[layer, pl.ds(row, IN_TILE), :], wbuf.at[slot], sem.at[slot])

    @pl.when(pl.program_id(1) == 0)
    def _():
        @pl.when(j == 0)
        def _():
            w_copy(0, 0).start(priority=BULK_DMA_PRIORITY)

        @pl.when(j + 1 < n_tiles)
        def _():
            w_copy(j + 1, (j + 1) % 2).start(priority=BULK_DMA_PRIORITY)

        w_copy(j, j % 2).wait()
        wbf[...] = wbuf[j % 2].astype(BF16)

    o_ref[...] = lax.dot_general(a_ref[...], wbf[...], NT_DIMS, preferred_element_type=F32).astype(o_ref.dtype)


def _in_proj(h, wt, layer):
    m, k = h.shape
    tm = _pick(m, (1024, 512, 256))
    return pl.pallas_call(
        functools.partial(_in_proj_kernel, layer=layer),
        grid=(P_COLS // IN_TILE, m // tm),
        in_specs=[pl.BlockSpec((tm, k), lambda j, i: (i, 0)), pl.BlockSpec(memory_space=pl.ANY)],
        out_specs=pl.BlockSpec((tm, IN_TILE), lambda j, i: (i, j)),
        out_shape=jax.ShapeDtypeStruct((m, P_COLS), BF16),
        scratch_shapes=[pltpu.VMEM((2, IN_TILE, k), F32), pltpu.VMEM((IN_TILE, k), BF16),
                        pltpu.SemaphoreType.DMA((2,))],
        compiler_params=_params("arbitrary", "arbitrary"),
        name="in_proj",
    )(h, wt)


def _mm_res_kernel(a_ref, w_ref, x_ref, ga_ref, o_ref, *, tm, bsz):
    r = _mod_row(pl.program_id(1), tm, bsz)
    acc = jnp.dot(a_ref[...], w_ref[0], preferred_element_type=F32)
    o_ref[...] = x_ref[...] + ga_ref[pl.ds(r, 1), :] * acc


def _out_proj_residual(mixed, w_o, layer, x_all, mod, bsz):
    m = mixed.shape[0]
    tm = _pick(m, (1024, 512, 256))
    tn = 1024
    rows = mod.shape[0]
    ga_blk = 2 * D_MODEL // tn
    return pl.pallas_call(
        functools.partial(_mm_res_kernel, tm=tm, bsz=bsz),
        grid=(D_MODEL // tn, m // tm),
        in_specs=[
            pl.BlockSpec((tm, D_MODEL), lambda j, i: (i, 0)),
            pl.BlockSpec((1, D_MODEL, tn), lambda j, i: (layer, 0, j)),
            pl.BlockSpec((tm, tn), lambda j, i: (i, j)),
            pl.BlockSpec((rows, tn), lambda j, i: (0, ga_blk + j)),
        ],
        out_specs=pl.BlockSpec((tm, tn), lambda j, i: (i, j)),
        out_shape=jax.ShapeDtypeStruct((m, D_MODEL), F32),
        compiler_params=_params("arbitrary", "arbitrary"),
        name="out_proj_residual",
    )(mixed, w_o, x_all, mod)


def _prep_kernel(cq_ref, ckv_ref, kpe_ref, bq_ref, bk_ref,
                 cosa_ref, sina_ref, csa_ref, cosb_ref, sinb_ref,
                 wuq_ref, wukv_ref, gqa_ref, gkva_ref, gqn_ref, gkn_ref,
                 qa_ref, ka_ref, va_ref, qb_ref, kb_ref):
    tm = cq_ref.shape[0]
    lane = lax.broadcasted_iota(jnp.int32, (tm, LANES), 1)

    nq = _rms(cq_ref[...].astype(F32), gqa_ref[...]).astype(BF16)
    qf = jnp.dot(nq, wuq_ref[...], preferred_element_type=F32)
    cosa = cosa_ref[...]
    sina = sina_ref[...]
    for h in range(MLA_HEADS):
        base = h * 3 * LANES
        nope = qf[:, base:base + LANES]
        roped = qf[:, base + LANES:base + 2 * LANES] * cosa + qf[:, base + 2 * LANES:base + 3 * LANES] * sina
        qa_ref[:, h * MLA_DK_PAD:h * MLA_DK_PAD + LANES] = (nope * MLA_SCALE).astype(BF16)
        qa_ref[:, h * MLA_DK_PAD + LANES:(h + 1) * MLA_DK_PAD] = (roped * MLA_SCALE).astype(BF16)

    nkv = _rms(ckv_ref[...].astype(F32), gkva_ref[...]).astype(BF16)
    kvf = jnp.dot(nkv, wukv_ref[...], preferred_element_type=F32)
    t = kpe_ref[...].astype(F32) * csa_ref[...]
    kpe = jnp.where(lane < MLA_ROPE, t + pltpu.roll(t, MLA_ROPE, axis=1), 0.0).astype(BF16)
    for h in range(MLA_HEADS):
        ka_ref[:, h * MLA_DK_PAD:h * MLA_DK_PAD + LANES] = kvf[:, h * LANES:(h + 1) * LANES].astype(BF16)
        ka_ref[:, h * MLA_DK_PAD + LANES:(h + 1) * MLA_DK_PAD] = kpe
    va_ref[...] = kvf[:, MLA_HEADS * MLA_NOPE:].astype(BF16)

    cosb = cosb_ref[...]
    sinb = sinb_ref[...]
    first = (lane % (GQA_HEAD_DIM // 2)) < (GQA_HEAD_DIM // 4)

    def rope_b(x):
        rot = jnp.where(first, pltpu.roll(x, LANES - GQA_HEAD_DIM // 4, axis=1),
                        pltpu.roll(x, GQA_HEAD_DIM // 4, axis=1))
        return x * cosb + rot * sinb

    gqn = gqn_ref[...]
    gkn = gkn_ref[...]
    for h in range(GQA_HEADS):
        x = bq_ref[:, h * LANES:(h + 1) * LANES].astype(F32)
        qb_ref[:, h * LANES:(h + 1) * LANES] = (rope_b(_rms(x, gqn)) * GQA_SCALE).astype(BF16)
    for h in range(GQA_KV_HEADS):
        x = bk_ref[:, h * LANES:(h + 1) * LANES].astype(F32)
        kb_ref[:, h * LANES:(h + 1) * LANES] = rope_b(_rms(x, gkn)).astype(BF16)


def _prep(p, pk, tables, w_uq_x, w_ukv_r, g_qa, g_kva, g_qn, g_kn):
    n = p.shape[0]
    tm = _pick(n, (512, 256))
    n_pos = SEQ // tm
    n_lat_tiles = (n // (SEQ + CTX_LEN)) * n_pos

    def tab(i):
        return (jnp.where(i < n_lat_tiles, i % n_pos, n_pos), 0)

    def col(width, off):
        return pl.BlockSpec((tm, width), lambda i: (i, off // width))

    def whole(a):
        return pl.BlockSpec(a.shape, lambda i: (0,) * a.ndim)

    tab_spec = pl.BlockSpec((tm, LANES), tab)
    cosa, sina, csa, cosb, sinb = tables
    outs = [(MLA_HEADS * MLA_DK_PAD, "qa"), (MLA_HEADS * MLA_DK_PAD, "ka"), (MLA_HEADS * MLA_V, "va"),
            (GQA_HEADS * GQA_HEAD_DIM, "qb"), (GQA_KV_HEADS * GQA_HEAD_DIM, "kb")]
    return pl.pallas_call(
        _prep_kernel,
        grid=(n // tm,),
        in_specs=[col(512, P_CQ), col(512, P_CKV), col(LANES, 0), col(512, P_B), col(256, P_B + 512),
                  tab_spec, tab_spec, tab_spec, tab_spec, tab_spec,
                  whole(w_uq_x), whole(w_ukv_r), whole(g_qa), whole(g_kva), whole(g_qn), whole(g_kn)],
        out_specs=[pl.BlockSpec((tm, w), lambda i: (i, 0)) for w, _ in outs],
        out_shape=[jax.ShapeDtypeStruct((n, w), BF16) for w, _ in outs],
        compiler_params=_params("arbitrary"),
        name="attn_prep",
    )(p, p, pk, p, p, cosa, sina, csa, cosb, sinb, w_uq_x, w_ukv_r, g_qa, g_kva, g_qn, g_kn)


def _attn_kernel(*refs, n_seg, n_groups, rep, dk, dv, scale):
    q_ref = refs[0]
    segs = [(refs[1 + 2 * s], refs[2 + 2 * s]) for s in range(n_seg)]
    o_ref = refs[1 + 2 * n_seg]
    nt = (((1,), (1,)), ((), ()))
    for g in range(n_groups):
        ks = [k_ref[:, g * dk:(g + 1) * dk] for k_ref, _ in segs]
        vs = []
        for _, v_ref in segs:
            one = (lax.broadcasted_iota(jnp.int32, (v_ref.shape[0], LANES), 1) == 0).astype(BF16)
            vs.append(jnp.concatenate([v_ref[:, g * dv:(g + 1) * dv], one], axis=1))
        for r in range(rep):
            hq = g * rep + r
            q = q_ref[:, hq * dk:(hq + 1) * dk]
            ss = [lax.dot_general(q, k, nt, preferred_element_type=F32) for k in ks]
            if scale != 1.0:
                ss = [s * scale for s in ss]
            m = functools.reduce(jnp.maximum, [jnp.max(s, axis=-1, keepdims=True) for s in ss])
            acc = None
            for s, v in zip(ss, vs):
                a = jnp.dot(jnp.exp(s - m).astype(BF16), v, preferred_element_type=F32)
                acc = a if acc is None else acc + a
            o_ref[:, hq * dv:(hq + 1) * dv] = (acc[:, :dv] / acc[:, dv:dv + 1]).astype(o_ref.dtype)


def _attention(q, q_off, segs, bsz, n_q_per_batch, q_row0, *, n_groups, rep, dk, dv, scale, name):
    tq = min(512, n_q_per_batch)
    nq = n_q_per_batch // tq
    qw = n_groups * rep * dk
    ow = n_groups * rep * dv

    def q_map(b, i):
        return ((q_row0 + b * n_q_per_batch) // tq + i, q_off // qw)

    in_specs = [pl.BlockSpec((tq, qw), q_map)]
    args = [q]
    for k_arr, k_off, v_arr, v_off, length, row0 in segs:
        kw, vw = n_groups * dk, n_groups * dv
        in_specs.append(pl.BlockSpec((length, kw), functools.partial(
            lambda b, i, length, row0, blk: (row0 // length + b, blk), length=length, row0=row0, blk=k_off // kw)))
        in_specs.append(pl.BlockSpec((length, vw), functools.partial(
            lambda b, i, length, row0, blk: (row0 // length + b, blk), length=length, row0=row0, blk=v_off // vw)))
        args += [k_arr, v_arr]
    return pl.pallas_call(
        functools.partial(_attn_kernel, n_seg=len(segs), n_groups=n_groups, rep=rep, dk=dk, dv=dv, scale=scale),
        grid=(bsz, nq),
        in_specs=in_specs,
        out_specs=pl.BlockSpec((tq, ow), lambda b, i: (b * nq + i, 0)),
        out_shape=jax.ShapeDtypeStruct((bsz * n_q_per_batch, ow), BF16),
        compiler_params=_params("arbitrary", "arbitrary"),
        name=name,
    )(*args)


def _na_kernel(q_ref, k_ref, v_ref, kc_ref, vc_ref, bias_ref, o_ref):
    j = pl.program_id(0)
    k_rows = NA_K_ROWS * GRID_W
    start = jnp.clip(NA_Q_ROWS * j - NA_WIN_H // 2, 0, SEQ // GRID_W - NA_K_ROWS) * GRID_W
    start = pl.multiple_of(start, 256)
    hd = NA_HEAD_DIM

    def with_ones(v):
        one = (lax.broadcasted_iota(jnp.int32, (v.shape[0], LANES), 1) == 0).astype(BF16)
        return jnp.concatenate([v, one], axis=1)

    for h in range(NA_HEADS):
        cols = slice(h * hd, (h + 1) * hd)
        q = q_ref[:, cols]
        s1 = lax.dot_general(q, k_ref[pl.ds(start, k_rows), cols], NT_DIMS, preferred_element_type=F32) * NA_SCALE
        s1 = s1 + bias_ref[0, h]
        s2 = lax.dot_general(q, kc_ref[:, cols], NT_DIMS, preferred_element_type=F32) * NA_SCALE
        m = jnp.maximum(jnp.max(s1, axis=-1, keepdims=True), jnp.max(s2, axis=-1, keepdims=True))
        acc = jnp.dot(jnp.exp(s1 - m).astype(BF16), with_ones(v_ref[pl.ds(start, k_rows), cols]),
                      preferred_element_type=F32)
        acc = acc + jnp.dot(jnp.exp(s2 - m).astype(BF16), with_ones(vc_ref[:, cols]), preferred_element_type=F32)
        o_ref[:, cols] = (acc[:, :hd] / acc[:, hd:hd + 1]).astype(o_ref.dtype)


def _na_bias_kernel(tp_ref, o_ref):
    t = pl.program_id(0)
    rows = SEQ // GRID_W
    k0 = jnp.clip(NA_Q_ROWS * t - NA_WIN_H // 2, 0, rows - NA_K_ROWS)
    left = lax.broadcasted_iota(jnp.int32, (GRID_W, 2 * GRID_W), 1) < GRID_W
    neg = -jnp.inf
    for a in range(NA_Q_ROWS):
        rq = NA_Q_ROWS * t + a
        r0 = jnp.clip(rq - NA_WIN_H // 2, 0, rows - NA_WIN_H)
        for b2 in range(NA_K_ROWS // 2):
            rk = k0 + 2 * b2
            ok_l = (rk >= r0) & (rk < r0 + NA_WIN_H)
            ok_r = (rk + 1 >= r0) & (rk + 1 < r0 + NA_WIN_H)
            pair = tp_ref[0, jnp.clip(rk - rq + NA_WIN_H, 0, 2 * NA_WIN_H - 1)]
            keep = jnp.where(left, ok_l.astype(jnp.int32), ok_r.astype(jnp.int32)) > 0
            o_ref[0, 0, a * GRID_W:(a + 1) * GRID_W, b2 * 2 * GRID_W:(b2 + 1) * 2 * GRID_W] = jnp.where(keep, pair, neg)


def _na_bias(rpb):
    n_dr = 2 * NA_WIN_H - 1
    cols = jnp.arange(GRID_W)
    c0 = jnp.clip(cols - NA_WIN_W // 2, 0, GRID_W - NA_WIN_W)
    col_ok = (cols[None, :] >= c0[:, None]) & (cols[None, :] < c0[:, None] + NA_WIN_W)
    dc = jnp.clip(cols[None, :] - cols[:, None] + NA_WIN_W - 1, 0, 2 * NA_WIN_W - 2)
    oh_c = (dc[:, :, None] == jnp.arange(2 * NA_WIN_W - 1)).astype(F32)
    by_col = jnp.einsum('hdc,qkc->hdqk', rpb.astype(F32), oh_c, precision=lax.Precision.HIGHEST)
    by_col = jnp.where(col_ok[None, None], by_col, -jnp.inf)
    zero = jnp.zeros((NA_HEADS, 1, GRID_W, GRID_W), F32)
    padded = jnp.concatenate([zero, by_col, zero], axis=1)
    pairs = jnp.concatenate([padded[:, :n_dr + 1], padded[:, 1:]], axis=-1)
    n_tiles = SEQ // GRID_W // NA_Q_ROWS
    tq, tk = NA_Q_ROWS * GRID_W, NA_K_ROWS * GRID_W
    return pl.pallas_call(
        _na_bias_kernel,
        grid=(n_tiles, NA_HEADS),
        in_specs=[pl.BlockSpec((1, n_dr + 1, GRID_W, 2 * GRID_W), lambda t, h: (h, 0, 0, 0))],
        out_specs=pl.BlockSpec((1, 1, tq, tk), lambda t, h: (t, h, 0, 0)),
        out_shape=jax.ShapeDtypeStruct((n_tiles, NA_HEADS, tq, tk), F32),
        compiler_params=_params("arbitrary", "arbitrary"),
        name="na_bias",
    )(pairs)


def _na_attention(p, bias, bsz, n_lat):
    tq = NA_Q_ROWS * GRID_W
    n_tiles = SEQ // tq
    w = NA_HEADS * NA_HEAD_DIM
    qb, kb, vb = P_C // w, (P_C + w) // w, (P_C + 2 * w) // w
    return pl.pallas_call(
        _na_kernel,
        grid=(n_tiles, bsz),
        in_specs=[
            pl.BlockSpec((tq, w), lambda j, b: (b * n_tiles + j, qb)),
            pl.BlockSpec((SEQ, w), lambda j, b: (b, kb)),
            pl.BlockSpec((SEQ, w), lambda j, b: (b, vb)),
            pl.BlockSpec((CTX_LEN, w), lambda j, b: (n_lat // CTX_LEN + b, kb)),
            pl.BlockSpec((CTX_LEN, w), lambda j, b: (n_lat // CTX_LEN + b, vb)),
            pl.BlockSpec((1, NA_HEADS, tq, NA_K_ROWS * GRID_W), lambda j, b: (j, 0, 0, 0)),
        ],
        out_specs=pl.BlockSpec((tq, w), lambda j, b: (b * n_tiles + j, 0)),
        out_shape=jax.ShapeDtypeStruct((n_lat, w), BF16),
        compiler_params=_params("arbitrary", "arbitrary"),
        name="na_attention",
    )(p, p, p, p, p, bias)


def _conv_kernel(b_ref, c_ref, u_ref, w_ref, o_ref):
    length = c_ref.shape[0]
    z = c_ref[...].astype(F32) * u_ref[...].astype(F32)
    row = lax.broadcasted_iota(jnp.int32, z.shape, 0)
    z_prev = jnp.where(row == 0, 0.0, pltpu.roll(z, 1, axis=0))
    z_next = jnp.where(row == length - 1, 0.0, pltpu.roll(z, length - 1, axis=0))
    conv = z_prev * w_ref[0:1, :] + z * w_ref[1:2, :] + z_next * w_ref[2:3, :]
    o_ref[...] = (b_ref[...].astype(F32) * conv).astype(o_ref.dtype)


def _short_conv(p, w_conv8, bsz, length, row0):
    blk = P_D // CONV_CH

    def rows(b):
        return row0 // length + b

    return pl.pallas_call(
        _conv_kernel,
        grid=(bsz,),
        in_specs=[
            pl.BlockSpec((length, CONV_CH), lambda b: (rows(b), blk)),
            pl.BlockSpec((length, CONV_CH), lambda b: (rows(b), blk + 1)),
            pl.BlockSpec((length, CONV_CH), lambda b: (rows(b), blk + 2)),
            pl.BlockSpec((8, CONV_CH), lambda b: (0, 0)),
        ],
        out_specs=pl.BlockSpec((length, CONV_CH), lambda b: (b, 0)),
        out_shape=jax.ShapeDtypeStruct((bsz * length, CONV_CH), BF16),
        compiler_params=_params("arbitrary"),
        name="short_conv",
    )(p, p, p, w_conv8)


MERGE_CHUNK = 512


def _merge_kernel(*refs, n_lat_tiles):
    o_ref, wb_ref = refs[-1], refs[-2]
    g_refs = refs[-2 - N_BRANCHES:-2]
    y_sets = [refs[s:s + N_BRANCHES] for s in range(0, len(refs) - 2 - N_BRANCHES, N_BRANCHES)]

    def emit(y_refs):
        for c in range(D_MODEL // MERGE_CHUNK):
            cols = slice(c * MERGE_CHUNK, (c + 1) * MERGE_CHUNK)
            acc = None
            for k, (y_ref, g_ref) in enumerate(zip(y_refs, g_refs)):
                proj = jnp.dot(y_ref[...], wb_ref[0, k, :, cols], preferred_element_type=F32)
                gate = 0.5 * jnp.tanh(0.5 * g_ref[:, cols].astype(F32)) + 0.5
                acc = gate * proj if acc is None else acc + gate * proj
            o_ref[:, cols] = acc.astype(o_ref.dtype)

    if len(y_sets) == 1:
        emit(y_sets[0])
    else:
        i = pl.program_id(0)

        @pl.when(i < n_lat_tiles)
        def _():
            emit(y_sets[0])

        @pl.when(i >= n_lat_tiles)
        def _():
            emit(y_sets[1])


def _merge(ys_lat, ys_ctx, p, w_branch, layer):
    n_lat = ys_lat[0].shape[0]
    n_act = n_lat + (ys_ctx[0].shape[0] if ys_ctx else 0)
    tm = _pick(ys_ctx[0].shape[0] if ys_ctx else n_lat, (512, 256))
    n_lat_tiles = n_lat // tm
    y_specs = [pl.BlockSpec((tm, BRANCH_WIDTH), lambda i: (jnp.minimum(i, n_lat_tiles - 1), 0))] * N_BRANCHES
    if ys_ctx:
        y_specs += [pl.BlockSpec((tm, BRANCH_WIDTH), lambda i: (jnp.maximum(i - n_lat_tiles, 0), 0))] * N_BRANCHES
    g_specs = [pl.BlockSpec((tm, D_MODEL), functools.partial(
        lambda i, k: (i, P_G // D_MODEL + k), k=k)) for k in range(N_BRANCHES)]
    return pl.pallas_call(
        functools.partial(_merge_kernel, n_lat_tiles=n_lat_tiles),
        grid=(n_act // tm,),
        in_specs=y_specs + g_specs + [
            pl.BlockSpec((1, N_BRANCHES, BRANCH_WIDTH, D_MODEL), lambda i: (layer, 0, 0, 0))],
        out_specs=pl.BlockSpec((tm, D_MODEL), lambda i: (i, 0)),
        out_shape=jax.ShapeDtypeStruct((n_act, D_MODEL), BF16),
        compiler_params=_params("arbitrary"),
        name="branch_merge",
    )(*ys_lat, *ys_ctx, p, p, p, p, w_branch)


R_E, R_W, R_RANK = 0, 2, 4


def _router_kernel(x_ref, g_ref, sh_ref, sc_ref, wr_ref, br_ref, h_ref, route_ref, counts_ref, run_ref,
                   *, tm, bsz):
    step = pl.program_id(0)
    r = _mod_row(step, tm, bsz)
    h = _modulated(x_ref[...], g_ref[...], sh_ref, sc_ref, r)
    h_ref[...] = h
    h_hi = h.astype(BF16)
    h_lo = (h - h_hi.astype(F32)).astype(BF16)
    logits = jnp.dot(h_hi, wr_ref[0], preferred_element_type=F32) + (
        jnp.dot(h_lo, wr_ref[0], preferred_element_type=F32) + jnp.dot(h_hi, wr_ref[1], preferred_element_type=F32))
    bias = br_ref[...]
    lane = lax.broadcasted_iota(jnp.int32, logits.shape, 1)
    lane_f = lane.astype(F32)
    neg = -jnp.inf

    def rmax(x, mask):
        return jnp.max(jnp.where(mask, x, neg), axis=-1, keepdims=True)

    def first_at(x, val, mask):
        return jnp.min(jnp.where(mask & (x == val), lane_f, 2.0 * LANES), axis=-1, keepdims=True)

    def pick(x, idx):
        return jnp.sum(jnp.where(lane_f == idx, x, 0.0), axis=-1, keepdims=True)

    gmask = lane < N_GROUPS
    ge = jnp.where(gmask, jnp.exp(logits - rmax(logits, gmask)), 0.0)
    g_prob = ge / jnp.sum(ge, axis=-1, keepdims=True)
    g_score = g_prob + bias
    g_sel = first_at(g_score, rmax(g_score, gmask), gmask)
    p_group = pick(g_prob, g_sel)

    group_of_lane = lax.shift_right_arithmetic(lane - N_EXPERTS, 3).astype(F32)
    emask = (lane >= N_EXPERTS) & (group_of_lane == g_sel)
    ee = jnp.where(emask, jnp.exp(logits - rmax(logits, emask)), 0.0)
    e_prob = ee / jnp.sum(ee, axis=-1, keepdims=True)
    e_score = e_prob + bias
    i1 = first_at(e_score, rmax(e_score, emask), emask)
    mask2 = emask & (lane_f != i1)
    i2 = first_at(e_score, rmax(e_score, mask2), mask2)
    p1 = pick(e_prob, i1)
    p2 = pick(e_prob, i2)
    w1 = p_group * p1 / (p1 + p2)
    w2 = p_group * p2 / (p1 + p2)
    e1 = i1 - N_EXPERTS
    e2 = i2 - N_EXPERTS

    @pl.when(step == 0)
    def _():
        run_ref[...] = jnp.zeros_like(run_ref)

    oh1 = lane_f == e1
    oh2 = lane_f == e2
    both = oh1.astype(F32) + oh2.astype(F32)
    tri = (lax.broadcasted_iota(jnp.int32, (tm, tm), 1) < lax.broadcasted_iota(jnp.int32, (tm, tm), 0)).astype(BF16)
    before = jnp.dot(tri, both.astype(BF16), preferred_element_type=F32) + run_ref[0:1, :]
    rank1 = jnp.sum(jnp.where(oh1, before, 0.0), axis=-1, keepdims=True)
    rank2 = jnp.sum(jnp.where(oh2, before + oh1.astype(F32), 0.0), axis=-1, keepdims=True)
    run_ref[...] = run_ref[...] + jnp.sum(both, axis=0, keepdims=True)
    counts_ref[...] = run_ref[...]

    vals = (e1, e2, w1, w2, rank1, rank2)
    out = jnp.zeros_like(logits)
    for k, v in enumerate(vals):
        out = jnp.where(lane == k, v, out)
    route_ref[...] = out


def _ffn_input_and_route(x1, gain, mod, w_route, b_route, bsz):
    n = x1.shape[0]
    tm = _pick(n, (512, 256))
    rows = mod.shape[0]
    return pl.pallas_call(
        functools.partial(_router_kernel, tm=tm, bsz=bsz),
        grid=(n // tm,),
        in_specs=[
            pl.BlockSpec((tm, D_MODEL), lambda i: (i, 0)),
            pl.BlockSpec((1, D_MODEL), lambda i: (0, 0)),
            pl.BlockSpec((rows, D_MODEL), lambda i: (0, 3)),
            pl.BlockSpec((rows, D_MODEL), lambda i: (0, 4)),
            pl.BlockSpec((2, D_MODEL, LANES), lambda i: (0, 0, 0)),
            pl.BlockSpec((1, LANES), lambda i: (0, 0)),
        ],
        out_specs=[pl.BlockSpec((tm, D_MODEL), lambda i: (i, 0)),
                   pl.BlockSpec((tm, LANES), lambda i: (i, 0)),
                   pl.BlockSpec((8, LANES), lambda i: (0, 0))],
        out_shape=[jax.ShapeDtypeStruct((n, D_MODEL), F32),
                   jax.ShapeDtypeStruct((n, LANES), F32),
                   jax.ShapeDtypeStruct((8, LANES), F32)],
        scratch_shapes=[pltpu.VMEM((8, LANES), F32)],
        compiler_params=_params("arbitrary"),
        name="ffn_input_route",
    )(x1, gain.reshape(1, D_MODEL), mod, mod, w_route, b_route)


def _dispatch_indices(route, counts_row, n_tok):
    counts = counts_row[0, :N_EXPERTS].astype(jnp.int32)
    padded = (counts + MOE_BLOCK - 1) // MOE_BLOCK * MOE_BLOCK
    pad_end = jnp.cumsum(padded)
    pad_start = pad_end - padded
    eid = route[:, R_E:R_E + TOP_K].astype(jnp.int32)
    rank = route[:, R_RANK:R_RANK + TOP_K].astype(jnp.int32)
    onehot = eid[:, :, None] == jnp.arange(N_EXPERTS, dtype=jnp.int32)[None, None, :]
    pos = (jnp.sum(jnp.where(onehot, pad_start[None, None, :], 0), axis=-1) + rank).reshape(-1)
    n_blocks = (n_tok * TOP_K + N_EXPERTS * (MOE_BLOCK - 1)) // MOE_BLOCK
    blk_start = jnp.arange(n_blocks, dtype=jnp.int32) * MOE_BLOCK
    blk_e = jnp.minimum(jnp.sum(pad_end[None, :] <= blk_start[:, None], axis=1), N_EXPERTS - 1).astype(jnp.int32)
    n_used = (pad_end[-1:] // MOE_BLOCK).astype(jnp.int32)
    last_blk = jnp.where(counts > 0, pad_end // MOE_BLOCK - 1, -1).astype(jnp.int32)
    blk = jnp.arange(n_blocks, dtype=jnp.int32)
    prev_e = jnp.concatenate([jnp.full((1,), -1, jnp.int32), blk_e[:-1]])
    first = (blk < n_used) & (blk_e != prev_e)
    ordinal = jnp.cumsum(first.astype(jnp.int32)) - 1
    n_first = jnp.sum(first.astype(jnp.int32))
    k = jnp.arange(N_EXPERTS, dtype=jnp.int32)
    expert_at = jnp.sum(jnp.where(first[None, :] & (ordinal[None, :] == k[:, None]), blk_e[None, :], 0), axis=1)

    def expert_after(d):
        target = ordinal + d
        e = jnp.sum(jnp.where(k[None, :] == target[:, None], expert_at[None, :], 0), axis=1)
        return jnp.where(target < n_first, e, -1).astype(jnp.int32)

    sched = (blk_e, n_used, first.astype(jnp.int32), (ordinal % MOE_W_SLOTS).astype(jnp.int32),
             expert_after(1), expert_after(2))
    return sched, last_blk, pos.astype(jnp.int32), n_blocks


def _row_copy(src, src_row, dst, dst_row, sem):
    return pltpu.make_async_copy(src.at[pl.ds(src_row, 1), :], dst.at[pl.ds(dst_row, 1), :], sem)


DISPATCH_CHUNK = 512
DMA_UNROLL = 8


def _dispatch_kernel(pos_ref, last_ref, nused_ref, h_ref, xs_hbm, zbuf, sem, zsem):
    base = pl.program_id(0) * DISPATCH_CHUNK
    n_blocks = xs_hbm.shape[0] // MOE_BLOCK

    @pl.when(pl.program_id(0) == 0)
    def _():
        zbuf[...] = jnp.zeros_like(zbuf)

        def fill(b):
            return pltpu.make_async_copy(zbuf, xs_hbm.at[pl.ds(pl.multiple_of(b * MOE_BLOCK, MOE_BLOCK), MOE_BLOCK), :],
                                         zsem.at[0])

        def each(fn):
            def expert(e, carry):
                @pl.when(last_ref[e] >= 0)
                def _():
                    fn(fill(last_ref[e]))
                return carry

            def unused(b, carry):
                fn(fill(b))
                return carry

            lax.fori_loop(0, N_EXPERTS, expert, 0)
            lax.fori_loop(nused_ref[0], n_blocks, unused, 0)

        each(lambda c: c.start())
        each(lambda c: c.wait())

    def group(g, carry):
        row0 = pl.multiple_of(g * DMA_UNROLL, DMA_UNROLL)
        for j in range(DMA_UNROLL):
            for k in range(TOP_K):
                _row_copy(h_ref, row0 + j, xs_hbm, pos_ref[(base + row0 + j) * TOP_K + k], sem.at[0]).start()
        return carry

    lax.fori_loop(0, DISPATCH_CHUNK // DMA_UNROLL, group, 0)
    for k in range(TOP_K):
        pltpu.make_async_copy(h_ref, xs_hbm.at[pl.ds(0, DISPATCH_CHUNK), :], sem.at[0]).wait()


def _dispatch(h2, pos, last_blk, n_used, n_rows):
    n, width = h2.shape
    assert n % DISPATCH_CHUNK == 0
    grid_spec = pltpu.PrefetchScalarGridSpec(
        num_scalar_prefetch=3,
        grid=(n // DISPATCH_CHUNK,),
        in_specs=[pl.BlockSpec((DISPATCH_CHUNK, width), lambda i, *_: (i, 0))],
        out_specs=pl.BlockSpec(memory_space=pl.ANY),
        scratch_shapes=[pltpu.VMEM((MOE_BLOCK, width), h2.dtype), pltpu.SemaphoreType.DMA((1,)),
                        pltpu.SemaphoreType.DMA((1,))],
    )
    return pl.pallas_call(
        _dispatch_kernel,
        grid_spec=grid_spec,
        out_shape=jax.ShapeDtypeStruct((n_rows, width), h2.dtype),
        compiler_params=_params("arbitrary"),
        name="moe_dispatch",
    )(pos, last_blk, n_used, h2)


def _moe_kernel(be_ref, nused_ref, first_ref, slot_ref, nxt1_ref, nxt2_ref, x_ref, wg_hbm, wu_hbm, wd_hbm, o_ref,
                wgf, wuf, wdf, wgb, wub, wdb, sem, *, layer):
    i = pl.program_id(0)

    def w_copies(e, s):
        return (pltpu.make_async_copy(wg_hbm.at[layer, e], wgf.at[s], sem.at[s]),
                pltpu.make_async_copy(wu_hbm.at[layer, e], wuf.at[s], sem.at[s]),
                pltpu.make_async_copy(wd_hbm.at[layer, e], wdf.at[s], sem.at[s]))

    def start(e, s):
        for c in w_copies(e, s):
            c.start(priority=BULK_DMA_PRIORITY)

    @pl.when(first_ref[i] == 1)
    def _():
        s = slot_ref[i]

        @pl.when(i == 0)
        def _():
            start(be_ref[i], s)

            @pl.when(nxt1_ref[i] >= 0)
            def _():
                start(nxt1_ref[i], (s + 1) % MOE_W_SLOTS)

        @pl.when(nxt2_ref[i] >= 0)
        def _():
            start(nxt2_ref[i], (s + 2) % MOE_W_SLOTS)

        for c in w_copies(be_ref[i], s):
            c.wait()
        wgb[...] = wgf[s].astype(BF16)
        wub[...] = wuf[s].astype(BF16)
        wdb[...] = wdf[s].astype(BF16)

    @pl.when(i < nused_ref[0])
    def _():
        x = x_ref[...].astype(BF16)
        gate = jnp.dot(x, wgb[...], preferred_element_type=F32)
        up = jnp.dot(x, wub[...], preferred_element_type=F32)
        hid = (gate * jax.nn.sigmoid(gate) * up).astype(BF16)
        o_ref[...] = jnp.dot(hid, wdb[...], preferred_element_type=F32)

    @pl.when(i >= nused_ref[0])
    def _():
        o_ref[...] = jnp.zeros_like(o_ref)


def _moe(xs, sched, w_gate, w_up, w_down, layer):
    n_blocks = sched[0].shape[0]
    grid_spec = pltpu.PrefetchScalarGridSpec(
        num_scalar_prefetch=len(sched),
        grid=(n_blocks,),
        in_specs=[pl.BlockSpec((MOE_BLOCK, D_MODEL), lambda i, *_: (i, 0))] + [pl.BlockSpec(memory_space=pl.ANY)] * 3,
        out_specs=pl.BlockSpec((MOE_BLOCK, D_MODEL), lambda i, *_: (i, 0)),
        scratch_shapes=[
            pltpu.VMEM((MOE_W_SLOTS, D_MODEL, EXPERT_HIDDEN), F32),
            pltpu.VMEM((MOE_W_SLOTS, D_MODEL, EXPERT_HIDDEN), F32),
            pltpu.VMEM((MOE_W_SLOTS, EXPERT_HIDDEN, D_MODEL), F32),
            pltpu.VMEM((D_MODEL, EXPERT_HIDDEN), BF16),
            pltpu.VMEM((D_MODEL, EXPERT_HIDDEN), BF16),
            pltpu.VMEM((EXPERT_HIDDEN, D_MODEL), BF16),
            pltpu.SemaphoreType.DMA((MOE_W_SLOTS,)),
        ],
    )
    return pl.pallas_call(
        functools.partial(_moe_kernel, layer=layer),
        grid_spec=grid_spec,
        out_shape=jax.ShapeDtypeStruct((n_blocks * MOE_BLOCK, D_MODEL), F32),
        compiler_params=_params("arbitrary"),
        name="moe_experts",
    )(*sched, xs, w_gate, w_up, w_down)


def _combine_kernel(pos_ref, ys_hbm, x_ref, route_ref, ga_ref, gn_ref, *rest, tm, bsz, final):
    if final:
        o_ref, rbuf, sem = rest
    else:
        nsh_ref, nsc_ref, o_ref, h_ref, rbuf, sem = rest
    i = pl.program_id(0)
    n = pl.num_programs(0)
    slot = i % 2

    def gather(blk, s):
        def group(g, carry):
            row0 = pl.multiple_of(g * DMA_UNROLL, DMA_UNROLL)
            for j in range(DMA_UNROLL):
                for k in range(TOP_K):
                    _row_copy(ys_hbm, pos_ref[(blk * tm + row0 + j) * TOP_K + k], rbuf.at[s], k * tm + row0 + j,
                              sem.at[s]).start()
            return carry
        lax.fori_loop(0, tm // DMA_UNROLL, group, 0)

    @pl.when(i == 0)
    def _():
        gather(0, 0)

    @pl.when(i + 1 < n)
    def _():
        gather(i + 1, 1 - slot)

    pltpu.make_async_copy(ys_hbm.at[pl.ds(0, TOP_K * tm), :], rbuf.at[slot], sem.at[slot]).wait()
    y = rbuf[slot, 0:tm, :] * route_ref[:, R_W:R_W + 1] + rbuf[slot, tm:2 * tm, :] * route_ref[:, R_W + 1:R_W + 2]
    r = _mod_row(i, tm, bsz)
    x = x_ref[...] + ga_ref[pl.ds(r, 1), :] * y
    if final:
        o_ref[...] = _rms(x, gn_ref[...])
    else:
        o_ref[...] = x
        h_ref[...] = _modulated(x, gn_ref[...], nsh_ref, nsc_ref, r).astype(h_ref.dtype)


def _combine(ys, pos, x1, route, mod, gain, next_mod, bsz):
    n = x1.shape[0]
    tm = 128
    rows = mod.shape[0]
    final = next_mod is None
    tile = pl.BlockSpec((tm, D_MODEL), lambda i, pos: (i, 0))
    in_specs = [
        pl.BlockSpec(memory_space=pl.ANY),
        tile,
        pl.BlockSpec((tm, LANES), lambda i, pos: (i, 0)),
        pl.BlockSpec((rows, D_MODEL), lambda i, pos: (0, 5)),
        pl.BlockSpec((1, D_MODEL), lambda i, pos: (0, 0)),
    ]
    args = [pos, ys, x1, route, mod, gain.reshape(1, D_MODEL)]
    out_specs, out_shape = tile, jax.ShapeDtypeStruct((n, D_MODEL), F32)
    if not final:
        in_specs += [pl.BlockSpec((rows, D_MODEL), lambda i, pos: (0, 0)),
                     pl.BlockSpec((rows, D_MODEL), lambda i, pos: (0, 1))]
        args += [next_mod, next_mod]
        out_specs, out_shape = [tile, tile], [out_shape, jax.ShapeDtypeStruct((n, D_MODEL), BF16)]
    grid_spec = pltpu.PrefetchScalarGridSpec(
        num_scalar_prefetch=1,
        grid=(n // tm,),
        in_specs=in_specs,
        out_specs=out_specs,
        scratch_shapes=[pltpu.VMEM((2, TOP_K * tm, D_MODEL), F32), pltpu.SemaphoreType.DMA((2,))],
    )
    return pl.pallas_call(
        functools.partial(_combine_kernel, tm=tm, bsz=bsz, final=final),
        grid_spec=grid_spec,
        out_shape=out_shape,
        compiler_params=_params("arbitrary"),
        name="moe_combine",
    )(*args)


def _rotate_half_cols(w, quarter):
    r1, r2, c1, c2 = (w[..., k * quarter:(k + 1) * quarter] for k in range(4))
    return jnp.concatenate([-r2, r1, -c2, c1], axis=-1)


def _kpe_rows(wt):
    kpe = jnp.swapaxes(wt[:, MLA_Q_LORA + MLA_KV_LORA:A_COLS, :], 1, 2)
    both = jnp.concatenate([kpe, _rotate_half_cols(kpe, MLA_ROPE // 4)], axis=-1)
    return jnp.swapaxes(both, 1, 2).astype(BF16)


def _relayout_w_uq(w_uq):
    w = w_uq.reshape(MLA_Q_LORA, MLA_HEADS, MLA_NOPE + MLA_ROPE)
    nope, pe = w[..., :MLA_NOPE], w[..., MLA_NOPE:]
    z = jnp.zeros_like(pe)
    out = jnp.concatenate([nope, pe, z, _rotate_half_cols(pe, MLA_ROPE // 4), z], axis=-1)
    return out.reshape(MLA_Q_LORA, MLA_HEADS * 3 * LANES).astype(BF16)


def _relayout_w_ukv(w_ukv):
    w = w_ukv.reshape(MLA_KV_LORA, MLA_HEADS, MLA_NOPE + MLA_V)
    return jnp.concatenate([w[..., :MLA_NOPE].reshape(MLA_KV_LORA, -1),
                            w[..., MLA_NOPE:].reshape(MLA_KV_LORA, -1)], axis=-1).astype(BF16)


def _rope_tables(tm):
    t = jnp.arange(SEQ)
    rows = (t // GRID_W).astype(F32)
    cols = (t % GRID_W).astype(F32)

    def cs(rot_dim):
        quarter = rot_dim // 4
        inv_freq = ROPE_THETA ** (-jnp.arange(quarter, dtype=F32) / quarter)
        ang_r = rows[:, None] * inv_freq[None, :]
        ang_c = cols[:, None] * inv_freq[None, :]
        cos = jnp.concatenate([jnp.cos(ang_r)] * 2 + [jnp.cos(ang_c)] * 2, axis=-1)
        sin = jnp.concatenate([jnp.sin(ang_r)] * 2 + [jnp.sin(ang_c)] * 2, axis=-1)
        return cos, sin

    def with_identity(tab, ident):
        return jnp.concatenate([tab, jnp.broadcast_to(ident, (tm, LANES))], axis=0)

    cos_a, sin_a = cs(MLA_ROPE)
    zeros = jnp.zeros((SEQ, LANES - MLA_ROPE), F32)
    lane = jnp.arange(LANES)
    one_lo = (lane < MLA_ROPE).astype(F32)[None, :]
    cosa = with_identity(jnp.concatenate([cos_a, zeros], axis=1), one_lo)
    sina = with_identity(jnp.concatenate([sin_a, zeros], axis=1), jnp.zeros((1, LANES), F32))
    csa = with_identity(jnp.concatenate([cos_a, sin_a], axis=1), one_lo)
    cos_b, sin_b = cs(GQA_HEAD_DIM)
    sign = jnp.where((lane % (GQA_HEAD_DIM // 2)) < GQA_HEAD_DIM // 4, -1.0, 1.0).astype(F32)[None, :]
    cosb = with_identity(cos_b, jnp.ones((1, LANES), F32))
    sinb = with_identity(sin_b * sign, jnp.zeros((1, LANES), F32))
    return cosa, sina, csa, cosb, sinb


def kernel(x, c, ctx, c_ctx, w_mod, b_mod, g_mix, g_ffn, w_in, w_uq, g_qa, w_ukv, g_kva, g_qn, g_kn, rpb, w_conv,
           w_branch, w_o, w_group, b_group, w_router, b_router, w_gate_e, w_up_e, w_down_e, g_final):
    bsz = x.shape[0]
    n_lat = bsz * SEQ
    n_ctx = bsz * CTX_LEN
    n_all = n_lat + n_ctx
    mod_rows = -(-(bsz + 1) // 8) * 8

    c_all = jnp.concatenate([c, c_ctx[None, :], jnp.zeros((mod_rows - bsz - 1, D_MODEL), F32)], axis=0)
    mods = _modulation(c_all, bsz + 1, w_mod, b_mod)
    tables = _rope_tables(_pick(n_all, (512, 256)))

    w_in_t = jnp.swapaxes(w_in, 1, 2)
    w_kpe_t = _kpe_rows(w_in_t)
    w_branch_b = w_branch.astype(BF16)
    w_o_b = w_o.astype(BF16)

    for l in range(DEPTH):
        last = l == DEPTH - 1
        n_act = n_lat if last else n_all
        mod = mods[l]
        w_uq_x = _relayout_w_uq(w_uq[l])
        w_ukv_r = _relayout_w_ukv(w_ukv[l])
        w_route = jnp.concatenate([w_group[l], jnp.zeros((D_MODEL, N_EXPERTS - N_GROUPS), F32), w_router[l]], axis=1)
        w_route_hi = w_route.astype(BF16)
        w_route = jnp.stack([w_route_hi, (w_route - w_route_hi.astype(F32)).astype(BF16)])
        b_route = jnp.concatenate([b_group[l], jnp.zeros((N_EXPERTS - N_GROUPS,), F32), b_router[l]])[None, :]
        w_conv8 = jnp.concatenate([w_conv[l], jnp.zeros((8 - CONV_WIDTH, CONV_CH), F32)], axis=0)

        if l == 0:
            h, x_all = _hmod(x.reshape(n_lat, D_MODEL), ctx.reshape(n_ctx, D_MODEL), g_mix[l], mod, bsz)
        p = _in_proj(h, w_in_t, l)
        pk = _matmul_nt(h, w_kpe_t, l, BF16, "kpe_proj")
        qa, ka, va, qb, kb = _prep(p, pk, tables, w_uq_x, w_ukv_r, g_qa[l][None, :], g_kva[l][None, :],
                                   g_qn[l][None, :], g_kn[l][None, :])

        lat_ctx = lambda k_arr, k_off, v_arr, v_off: [(k_arr, k_off, v_arr, v_off, SEQ, 0),
                                                      (k_arr, k_off, v_arr, v_off, CTX_LEN, n_lat)]
        ctx_only = lambda k_arr, k_off, v_arr, v_off: [(k_arr, k_off, v_arr, v_off, CTX_LEN, n_lat)]
        mla = dict(n_groups=MLA_HEADS, rep=1, dk=MLA_DK_PAD, dv=MLA_V, scale=1.0)
        gqa = dict(n_groups=GQA_KV_HEADS, rep=GQA_HEADS // GQA_KV_HEADS, dk=GQA_HEAD_DIM, dv=GQA_HEAD_DIM, scale=1.0)
        nac = dict(n_groups=NA_HEADS, rep=1, dk=NA_HEAD_DIM, dv=NA_HEAD_DIM, scale=NA_SCALE)

        ys_lat = [
            _attention(qa, 0, lat_ctx(ka, 0, va, 0), bsz, SEQ, 0, name="mla_attention", **mla),
            _attention(qb, 0, lat_ctx(kb, 0, p, P_B + 768), bsz, SEQ, 0, name="gqa_attention", **gqa),
            _na_attention(p, _na_bias(rpb[l]), bsz, n_lat),
            _short_conv(p, w_conv8, bsz, SEQ, 0),
        ]
        ys_ctx = []
        if not last:
            ys_ctx = [
                _attention(qa, 0, ctx_only(ka, 0, va, 0), bsz, CTX_LEN, n_lat, name="mla_attention_ctx", **mla),
                _attention(qb, 0, ctx_only(kb, 0, p, P_B + 768), bsz, CTX_LEN, n_lat,
                           name="gqa_attention_ctx", **gqa),
                _attention(p, P_C, ctx_only(p, P_C + 512, p, P_C + 1024), bsz, CTX_LEN, n_lat,
                           name="na_attention_ctx", **nac),
                _short_conv(p, w_conv8, bsz, CTX_LEN, n_lat),
            ]

        mixed = _merge(ys_lat, ys_ctx, p, w_branch_b, l)
        x1 = _out_proj_residual(mixed, w_o_b, l, x_all, mod, bsz)
        h2, route, counts = _ffn_input_and_route(x1, g_ffn[l], mod, w_route, b_route, bsz)
        sched, last_blk, pos, n_blocks = _dispatch_indices(route, counts, n_act)
        xs = _dispatch(h2, pos, last_blk, sched[1], n_blocks * MOE_BLOCK)
        ys = _moe(xs, sched, w_gate_e, w_up_e, w_down_e, l)
        if last:
            x_all = _combine(ys, pos, x1, route, mod, g_final, None, bsz)
        else:
            x_all, h = _combine(ys, pos, x1, route, mod, g_mix[l + 1], mods[l + 1], bsz)
    return x_all.reshape(bsz, SEQ, D_MODEL)
```

```python
import functools

import jax
import jax.numpy as jnp
from jax import lax
from jax.experimental import pallas as pl
from jax.experimental.pallas import tpu as pltpu

F32 = jnp.float32
BF16 = jnp.bfloat16

D_MODEL = 2048
SEQ = 2048
DEPTH = 2
CTX_LEN = 256
GRID_W = 64
ROPE_THETA = 10000.0
NORM_EPS = 1e-6
MOD_CHUNKS = 6

MLA_HEADS = 4
MLA_Q_LORA = 512
MLA_KV_LORA = 512
MLA_NOPE = 128
MLA_ROPE = 64
MLA_V = 128
MLA_SCALE = (MLA_NOPE + MLA_ROPE) ** -0.5
MLA_DK_PAD = 256

GQA_HEADS = 4
GQA_KV_HEADS = 2
GQA_HEAD_DIM = 128
GQA_SCALE = GQA_HEAD_DIM ** -0.5

NA_HEADS = 4
NA_HEAD_DIM = 128
NA_WIN_H = 8
NA_WIN_W = 16
NA_SCALE = NA_HEAD_DIM ** -0.5
NA_Q_ROWS = 8
NA_K_ROWS = 16

CONV_CH = 512
CONV_WIDTH = 3
BRANCH_WIDTH = 512
N_BRANCHES = 4

A_COLS = MLA_Q_LORA + MLA_KV_LORA + MLA_ROPE
B_COLS = (GQA_HEADS + 2 * GQA_KV_HEADS) * GQA_HEAD_DIM
C_COLS = 3 * NA_HEADS * NA_HEAD_DIM
D_COLS = 3 * CONV_CH
G_COLS = N_BRANCHES * D_MODEL

P_B = 0
P_C = P_B + B_COLS
P_D = P_C + C_COLS
P_G = P_D + D_COLS
P_CQ = P_G + G_COLS
P_CKV = P_CQ + MLA_Q_LORA
P_COLS = P_CKV + MLA_KV_LORA
IN_TILE = 1024

N_GROUPS = 8
EXPERTS_PER_GROUP = 8
N_EXPERTS = 64
TOP_K = 2
EXPERT_HIDDEN = 512
MOE_BLOCK = 128
MOE_W_SLOTS = 3

LANES = 128
BULK_DMA_PRIORITY = 1
VMEM_LIMIT = 56 * 1024 * 1024


def _params(*sem):
    return pltpu.CompilerParams(dimension_semantics=sem, vmem_limit_bytes=VMEM_LIMIT)


def _pick(n, cands):
    for c in cands:
        if n % c == 0:
            return c
    raise ValueError(f"no tile for {n}")


def _rms(x, g):
    return x * lax.rsqrt(jnp.mean(x * x, axis=-1, keepdims=True) + NORM_EPS) * g


def _mod_row(i, tm, bsz):
    return jnp.minimum(i * tm // SEQ, bsz)


def _mod_kernel(ct_ref, w_ref, b_ref, o_ref, *, n_vec):
    ct = ct_ref[...]
    st = ct * jax.nn.sigmoid(ct)
    w = w_ref[0]
    rows = [jnp.sum(w * st[:, r:r + 1], axis=0, keepdims=True) for r in range(n_vec)]
    rows += [jnp.zeros_like(rows[0])] * (o_ref.shape[1] - n_vec)
    o_ref[0] = jnp.concatenate(rows, axis=0) + b_ref[0]


def _modulation(c_all, n_vec, w_mod, b_mod):
    rows = c_all.shape[0]
    n = MOD_CHUNKS * D_MODEL
    tn = 1024
    return pl.pallas_call(
        functools.partial(_mod_kernel, n_vec=n_vec),
        grid=(DEPTH, n // tn),
        in_specs=[
            pl.BlockSpec((D_MODEL, rows), lambda l, j: (0, 0)),
            pl.BlockSpec((1, D_MODEL, tn), lambda l, j: (l, 0, j)),
            pl.BlockSpec((1, 1, tn), lambda l, j: (l, 0, j)),
        ],
        out_specs=pl.BlockSpec((1, rows, tn), lambda l, j: (l, 0, j)),
        out_shape=jax.ShapeDtypeStruct((DEPTH, rows, n), F32),
        compiler_params=_params("arbitrary", "arbitrary"),
        name="modulation",
    )(c_all.T, w_mod, b_mod.reshape(DEPTH, 1, n))


def _modulated(x, gain, sh_ref, sc_ref, r):
    return _rms(x, gain) * (1.0 + sc_ref[pl.ds(r, 1), :]) + sh_ref[pl.ds(r, 1), :]


def _hmod_kernel(x_ref, c_ref, g_ref, sh_ref, sc_ref, h_ref, xall_ref, *, tm, bsz, n_lat_tiles):
    i = pl.program_id(0)
    r = _mod_row(i, tm, bsz)

    def emit(src_ref):
        x = src_ref[...]
        xall_ref[...] = x
        h_ref[...] = _modulated(x, g_ref[...], sh_ref, sc_ref, r).astype(h_ref.dtype)

    @pl.when(i < n_lat_tiles)
    def _():
        emit(x_ref)

    @pl.when(i >= n_lat_tiles)
    def _():
        emit(c_ref)


def _hmod(x_lat, x_ctx, gain, mod, bsz):
    n_lat, n_ctx = x_lat.shape[0], x_ctx.shape[0]
    n = n_lat + n_ctx
    tm = _pick(n_ctx, (512, 256))
    n_lat_tiles = n_lat // tm
    rows = mod.shape[0]
    return pl.pallas_call(
        functools.partial(_hmod_kernel, tm=tm, bsz=bsz, n_lat_tiles=n_lat_tiles),
        grid=(n // tm,),
        in_specs=[
            pl.BlockSpec((tm, D_MODEL), lambda i: (jnp.minimum(i, n_lat_tiles - 1), 0)),
            pl.BlockSpec((tm, D_MODEL), lambda i: (jnp.maximum(i - n_lat_tiles, 0), 0)),
            pl.BlockSpec((1, D_MODEL), lambda i: (0, 0)),
            pl.BlockSpec((rows, D_MODEL), lambda i: (0, 0)),
            pl.BlockSpec((rows, D_MODEL), lambda i: (0, 1)),
        ],
        out_specs=[pl.BlockSpec((tm, D_MODEL), lambda i: (i, 0)), pl.BlockSpec((tm, D_MODEL), lambda i: (i, 0))],
        out_shape=[jax.ShapeDtypeStruct((n, D_MODEL), BF16), jax.ShapeDtypeStruct((n, D_MODEL), F32)],
        compiler_params=_params("arbitrary"),
        name="hmod",
    )(x_lat, x_ctx, gain.reshape(1, D_MODEL), mod, mod)


NT_DIMS = (((1,), (1,)), ((), ()))


def _mm_nt_kernel(a_ref, w_ref, o_ref):
    o_ref[...] = lax.dot_general(a_ref[...], w_ref[0], NT_DIMS, preferred_element_type=F32).astype(o_ref.dtype)


def _matmul_nt(a, wt, layer, out_dtype, name):
    m, k = a.shape
    n = wt.shape[1]
    tm = _pick(m, (1024, 512, 256))
    return pl.pallas_call(
        _mm_nt_kernel,
        grid=(m // tm,),
        in_specs=[pl.BlockSpec((tm, k), lambda i: (i, 0)), pl.BlockSpec((1, n, k), lambda i: (layer, 0, 0))],
        out_specs=pl.BlockSpec((tm, n), lambda i: (i, 0)),
        out_shape=jax.ShapeDtypeStruct((m, n), out_dtype),
        compiler_params=_params("arbitrary"),
        name=name,
    )(a, wt)


def _in_proj_kernel(a_ref, wt_hbm, o_ref, wbuf, wbf, sem, *, layer):
    j = pl.program_id(0)
    n_tiles = pl.num_programs(0)
    n_shift = P_CQ // IN_TILE

    def w_copy(jj, slot):
        row = pl.multiple_of(jnp.where(jj < n_shift, A_COLS + jj * IN_TILE, 0), MLA_ROPE)
        return pltpu.make_async_copy(wt_hbm.at[layer, pl.ds(row, IN_TILE), :], wbuf.at[slot], sem.at[slot])

    @pl.when(pl.program_id(1) == 0)
    def _():
        @pl.when(j == 0)
        def _():
            w_copy(0, 0).start(priority=BULK_DMA_PRIORITY)

        @pl.when(j + 1 < n_tiles)
        def _():
            w_copy(j + 1, (j + 1) % 2).start(priority=BULK_DMA_PRIORITY)

        w_copy(j, j % 2).wait()
        wbf[...] = wbuf[j % 2].astype(BF16)

    o_ref[...] = lax.dot_general(a_ref[...], wbf[...], NT_DIMS, preferred_element_type=F32).astype(o_ref.dtype)


def _in_proj(h, wt, layer):
    m, k = h.shape
    tm = _pick(m, (1024, 512, 256))
    return pl.pallas_call(
        functools.partial(_in_proj_kernel, layer=layer),
        grid=(P_COLS // IN_TILE, m // tm),
        in_specs=[pl.BlockSpec((tm, k), lambda j, i: (i, 0)), pl.BlockSpec(memory_space=pl.ANY)],
        out_specs=pl.BlockSpec((tm, IN_TILE), lambda j, i: (i, j)),
        out_shape=jax.ShapeDtypeStruct((m, P_COLS), BF16),
        scratch_shapes=[pltpu.VMEM((2, IN_TILE, k), F32), pltpu.VMEM((IN_TILE, k), BF16),
                        pltpu.SemaphoreType.DMA((2,))],
        compiler_params=_params("arbitrary", "arbitrary"),
        name="in_proj",
    )(h, wt)


def _mm_res_kernel(a_ref, w_ref, x_ref, ga_ref, o_ref, *, tm, bsz):
    r = _mod_row(pl.program_id(1), tm, bsz)
    acc = jnp.dot(a_ref[...], w_ref[0], preferred_element_type=F32)
    o_ref[...] = x_ref[...] + ga_ref[pl.ds(r, 1), :] * acc


def _out_proj_residual(mixed, w_o, layer, x_all, mod, bsz):
    m = mixed.shape[0]
    tm = _pick(m, (1024, 512, 256))
    tn = 1024
    rows = mod.shape[0]
    ga_blk = 2 * D_MODEL // tn
    return pl.pallas_call(
        functools.partial(_mm_res_kernel, tm=tm, bsz=bsz),
        grid=(D_MODEL // tn, m // tm),
        in_specs=[
            pl.BlockSpec((tm, D_MODEL), lambda j, i: (i, 0)),
            pl.BlockSpec((1, D_MODEL, tn), lambda j, i: (layer, 0, j)),
            pl.BlockSpec((tm, tn), lambda j, i: (i, j)),
            pl.BlockSpec((rows, tn), lambda j, i: (0, ga_blk + j)),
        ],
        out_specs=pl.BlockSpec((tm, tn), lambda j, i: (i, j)),
        out_shape=jax.ShapeDtypeStruct((m, D_MODEL), F32),
        compiler_params=_params("arbitrary", "arbitrary"),
        name="out_proj_residual",
    )(mixed, w_o, x_all, mod)


def _prep_kernel(cq_ref, ckv_ref, kpe_ref, bq_ref, bk_ref,
                 cosa_ref, sina_ref, csa_ref, cosb_ref, sinb_ref,
                 wuq_ref, wukv_ref, gqa_ref, gkva_ref, gqn_ref, gkn_ref,
                 qa_ref, ka_ref, va_ref, qb_ref, kb_ref):
    tm = cq_ref.shape[0]
    lane = lax.broadcasted_iota(jnp.int32, (tm, LANES), 1)

    nq = _rms(cq_ref[...].astype(F32), gqa_ref[...]).astype(BF16)
    qf = jnp.dot(nq, wuq_ref[...], preferred_element_type=F32)
    cosa = cosa_ref[...]
    sina = sina_ref[...]
    for h in range(MLA_HEADS):
        base = h * 3 * LANES
        nope = qf[:, base:base + LANES]
        roped = qf[:, base + LANES:base + 2 * LANES] * cosa + qf[:, base + 2 * LANES:base + 3 * LANES] * sina
        qa_ref[:, h * MLA_DK_PAD:h * MLA_DK_PAD + LANES] = (nope * MLA_SCALE).astype(BF16)
        qa_ref[:, h * MLA_DK_PAD + LANES:(h + 1) * MLA_DK_PAD] = (roped * MLA_SCALE).astype(BF16)

    nkv = _rms(ckv_ref[...].astype(F32), gkva_ref[...]).astype(BF16)
    kvf = jnp.dot(nkv, wukv_ref[...], preferred_element_type=F32)
    t = kpe_ref[...].astype(F32) * csa_ref[...]
    kpe = jnp.where(lane < MLA_ROPE, t + pltpu.roll(t, MLA_ROPE, axis=1), 0.0).astype(BF16)
    for h in range(MLA_HEADS):
        ka_ref[:, h * MLA_DK_PAD:h * MLA_DK_PAD + LANES] = kvf[:, h * LANES:(h + 1) * LANES].astype(BF16)
        ka_ref[:, h * MLA_DK_PAD + LANES:(h + 1) * MLA_DK_PAD] = kpe
    va_ref[...] = kvf[:, MLA_HEADS * MLA_NOPE:].astype(BF16)

    cosb = cosb_ref[...]
    sinb = sinb_ref[...]
    first = (lane % (GQA_HEAD_DIM // 2)) < (GQA_HEAD_DIM // 4)

    def rope_b(x):
        rot = jnp.where(first, pltpu.roll(x, LANES - GQA_HEAD_DIM // 4, axis=1),
                        pltpu.roll(x, GQA_HEAD_DIM // 4, axis=1))
        return x * cosb + rot * sinb

    gqn = gqn_ref[...]
    gkn = gkn_ref[...]
    for h in range(GQA_HEADS):
        x = bq_ref[:, h * LANES:(h + 1) * LANES].astype(F32)
        qb_ref[:, h * LANES:(h + 1) * LANES] = (rope_b(_rms(x, gqn)) * GQA_SCALE).astype(BF16)
    for h in range(GQA_KV_HEADS):
        x = bk_ref[:, h * LANES:(h + 1) * LANES].astype(F32)
        kb_ref[:, h * LANES:(h + 1) * LANES] = rope_b(_rms(x, gkn)).astype(BF16)


def _prep(p, pk, tables, w_uq_x, w_ukv_r, g_qa, g_kva, g_qn, g_kn):
    n = p.shape[0]
    tm = _pick(n, (512, 256))
    n_pos = SEQ // tm
    n_lat_tiles = (n // (SEQ + CTX_LEN)) * n_pos

    def tab(i):
        return (jnp.where(i < n_lat_tiles, i % n_pos, n_pos), 0)

    def col(width, off):
        return pl.BlockSpec((tm, width), lambda i: (i, off // width))

    def whole(a):
        return pl.BlockSpec(a.shape, lambda i: (0,) * a.ndim)

    tab_spec = pl.BlockSpec((tm, LANES), tab)
    cosa, sina, csa, cosb, sinb = tables
    outs = [(MLA_HEADS * MLA_DK_PAD, "qa"), (MLA_HEADS * MLA_DK_PAD, "ka"), (MLA_HEADS * MLA_V, "va"),
            (GQA_HEADS * GQA_HEAD_DIM, "qb"), (GQA_KV_HEADS * GQA_HEAD_DIM, "kb")]
    return pl.pallas_call(
        _prep_kernel,
        grid=(n // tm,),
        in_specs=[col(512, P_CQ), col(512, P_CKV), col(LANES, 0), col(512, P_B), col(256, P_B + 512),
                  tab_spec, tab_spec, tab_spec, tab_spec, tab_spec,
                  whole(w_uq_x), whole(w_ukv_r), whole(g_qa), whole(g_kva), whole(g_qn), whole(g_kn)],
        out_specs=[pl.BlockSpec((tm, w), lambda i: (i, 0)) for w, _ in outs],
        out_shape=[jax.ShapeDtypeStruct((n, w), BF16) for w, _ in outs],
        compiler_params=_params("arbitrary"),
        name="attn_prep",
    )(p, p, pk, p, p, cosa, sina, csa, cosb, sinb, w_uq_x, w_ukv_r, g_qa, g_kva, g_qn, g_kn)


def _attn_kernel(*refs, n_seg, n_groups, rep, dk, dv, scale):
    q_ref = refs[0]
    segs = [(refs[1 + 2 * s], refs[2 + 2 * s]) for s in range(n_seg)]
    o_ref = refs[1 + 2 * n_seg]
    nt = (((1,), (1,)), ((), ()))
    for g in range(n_groups):
        ks = [k_ref[:, g * dk:(g + 1) * dk] for k_ref, _ in segs]
        vs = []
        for _, v_ref in segs:
            one = (lax.broadcasted_iota(jnp.int32, (v_ref.shape[0], LANES), 1) == 0).astype(BF16)
            vs.append(jnp.concatenate([v_ref[:, g * dv:(g + 1) * dv], one], axis=1))
        for r in range(rep):
            hq = g * rep + r
            q = q_ref[:, hq * dk:(hq + 1) * dk]
            ss = [lax.dot_general(q, k, nt, preferred_element_type=F32) for k in ks]
            if scale != 1.0:
                ss = [s * scale for s in ss]
            m = functools.reduce(jnp.maximum, [jnp.max(s, axis=-1, keepdims=True) for s in ss])
            acc = None
            for s, v in zip(ss, vs):
                a = jnp.dot(jnp.exp(s - m).astype(BF16), v, preferred_element_type=F32)
                acc = a if acc is None else acc + a
            o_ref[:, hq * dv:(hq + 1) * dv] = (acc[:, :dv] / acc[:, dv:dv + 1]).astype(o_ref.dtype)


def _attention(q, q_off, segs, bsz, n_q_per_batch, q_row0, *, n_groups, rep, dk, dv, scale, name):
    tq = min(512, n_q_per_batch)
    nq = n_q_per_batch // tq
    qw = n_groups * rep * dk
    ow = n_groups * rep * dv

    def q_map(b, i):
        return ((q_row0 + b * n_q_per_batch) // tq + i, q_off // qw)

    in_specs = [pl.BlockSpec((tq, qw), q_map)]
    args = [q]
    for k_arr, k_off, v_arr, v_off, length, row0 in segs:
        kw, vw = n_groups * dk, n_groups * dv
        in_specs.append(pl.BlockSpec((length, kw), functools.partial(
            lambda b, i, length, row0, blk: (row0 // length + b, blk), length=length, row0=row0, blk=k_off // kw)))
        in_specs.append(pl.BlockSpec((length, vw), functools.partial(
            lambda b, i, length, row0, blk: (row0 // length + b, blk), length=length, row0=row0, blk=v_off // vw)))
        args += [k_arr, v_arr]
    return pl.pallas_call(
        functools.partial(_attn_kernel, n_seg=len(segs), n_groups=n_groups, rep=rep, dk=dk, dv=dv, scale=scale),
        grid=(bsz, nq),
        in_specs=in_specs,
        out_specs=pl.BlockSpec((tq, ow), lambda b, i: (b * nq + i, 0)),
        out_shape=jax.ShapeDtypeStruct((bsz * n_q_per_batch, ow), BF16),
        compiler_params=_params("arbitrary", "arbitrary"),
        name=name,
    )(*args)


def _na_kernel(q_ref, k_ref, v_ref, kc_ref, vc_ref, bias_ref, o_ref):
    j = pl.program_id(0)
    k_rows = NA_K_ROWS * GRID_W
    start = jnp.clip(NA_Q_ROWS * j - NA_WIN_H // 2, 0, SEQ // GRID_W - NA_K_ROWS) * GRID_W
    start = pl.multiple_of(start, 256)
    hd = NA_HEAD_DIM

    def with_ones(v):
        one = (lax.broadcasted_iota(jnp.int32, (v.shape[0], LANES), 1) == 0).astype(BF16)
        return jnp.concatenate([v, one], axis=1)

    for h in range(NA_HEADS):
        cols = slice(h * hd, (h + 1) * hd)
        q = q_ref[:, cols]
        s1 = lax.dot_general(q, k_ref[pl.ds(start, k_rows), cols], NT_DIMS, preferred_element_type=F32) * NA_SCALE
        s1 = s1 + bias_ref[0, h]
        s2 = lax.dot_general(q, kc_ref[:, cols], NT_DIMS, preferred_element_type=F32) * NA_SCALE
        m = jnp.maximum(jnp.max(s1, axis=-1, keepdims=True), jnp.max(s2, axis=-1, keepdims=True))
        acc = jnp.dot(jnp.exp(s1 - m).astype(BF16), with_ones(v_ref[pl.ds(start, k_rows), cols]),
                      preferred_element_type=F32)
        acc = acc + jnp.dot(jnp.exp(s2 - m).astype(BF16), with_ones(vc_ref[:, cols]), preferred_element_type=F32)
        o_ref[:, cols] = (acc[:, :hd] / acc[:, hd:hd + 1]).astype(o_ref.dtype)


def _na_bias_kernel(tp_ref, o_ref):
    t = pl.program_id(0)
    rows = SEQ // GRID_W
    k0 = jnp.clip(NA_Q_ROWS * t - NA_WIN_H // 2, 0, rows - NA_K_ROWS)
    left = lax.broadcasted_iota(jnp.int32, (GRID_W, 2 * GRID_W), 1) < GRID_W
    neg = -jnp.inf
    for a in range(NA_Q_ROWS):
        rq = NA_Q_ROWS * t + a
        r0 = jnp.clip(rq - NA_WIN_H // 2, 0, rows - NA_WIN_H)
        for b2 in range(NA_K_ROWS // 2):
            rk = k0 + 2 * b2
            ok_l = (rk >= r0) & (rk < r0 + NA_WIN_H)
            ok_r = (rk + 1 >= r0) & (rk + 1 < r0 + NA_WIN_H)
            pair = tp_ref[0, jnp.clip(rk - rq + NA_WIN_H, 0, 2 * NA_WIN_H - 1)]
            keep = jnp.where(left, ok_l.astype(jnp.int32), ok_r.astype(jnp.int32)) > 0
            o_ref[0, 0, a * GRID_W:(a + 1) * GRID_W, b2 * 2 * GRID_W:(b2 + 1) * 2 * GRID_W] = jnp.where(keep, pair, neg)


def _na_bias(rpb):
    n_dr = 2 * NA_WIN_H - 1
    cols = jnp.arange(GRID_W)
    c0 = jnp.clip(cols - NA_WIN_W // 2, 0, GRID_W - NA_WIN_W)
    col_ok = (cols[None, :] >= c0[:, None]) & (cols[None, :] < c0[:, None] + NA_WIN_W)
    dc = jnp.clip(cols[None, :] - cols[:, None] + NA_WIN_W - 1, 0, 2 * NA_WIN_W - 2)
    oh_c = (dc[:, :, None] == jnp.arange(2 * NA_WIN_W - 1)).astype(F32)
    by_col = jnp.einsum('hdc,qkc->hdqk', rpb.astype(F32), oh_c, precision=lax.Precision.HIGHEST)
    by_col = jnp.where(col_ok[None, None], by_col, -jnp.inf)
    zero = jnp.zeros((NA_HEADS, 1, GRID_W, GRID_W), F32)
    padded = jnp.concatenate([zero, by_col, zero], axis=1)
    pairs = jnp.concatenate([padded[:, :n_dr + 1], padded[:, 1:]], axis=-1)
    n_tiles = SEQ // GRID_W // NA_Q_ROWS
    tq, tk = NA_Q_ROWS * GRID_W, NA_K_ROWS * GRID_W
    return pl.pallas_call(
        _na_bias_kernel,
        grid=(n_tiles, NA_HEADS),
        in_specs=[pl.BlockSpec((1, n_dr + 1, GRID_W, 2 * GRID_W), lambda t, h: (h, 0, 0, 0))],
        out_specs=pl.BlockSpec((1, 1, tq, tk), lambda t, h: (t, h, 0, 0)),
        out_shape=jax.ShapeDtypeStruct((n_tiles, NA_HEADS, tq, tk), F32),
        compiler_params=_params("arbitrary", "arbitrary"),
        name="na_bias",
    )(pairs)


def _na_attention(p, bias, bsz, n_lat):
    tq = NA_Q_ROWS * GRID_W
    n_tiles = SEQ // tq
    w = NA_HEADS * NA_HEAD_DIM
    qb, kb, vb = P_C // w, (P_C + w) // w, (P_C + 2 * w) // w
    return pl.pallas_call(
        _na_kernel,
        grid=(n_tiles, bsz),
        in_specs=[
            pl.BlockSpec((tq, w), lambda j, b: (b * n_tiles + j, qb)),
            pl.BlockSpec((SEQ, w), lambda j, b: (b, kb)),
            pl.BlockSpec((SEQ, w), lambda j, b: (b, vb)),
            pl.BlockSpec((CTX_LEN, w), lambda j, b: (n_lat // CTX_LEN + b, kb)),
            pl.BlockSpec((CTX_LEN, w), lambda j, b: (n_lat // CTX_LEN + b, vb)),
            pl.BlockSpec((1, NA_HEADS, tq, NA_K_ROWS * GRID_W), lambda j, b: (j, 0, 0, 0)),
        ],
        out_specs=pl.BlockSpec((tq, w), lambda j, b: (b * n_tiles + j, 0)),
        out_shape=jax.ShapeDtypeStruct((n_lat, w), BF16),
        compiler_params=_params("arbitrary", "arbitrary"),
        name="na_attention",
    )(p, p, p, p, p, bias)


def _conv_kernel(b_ref, c_ref, u_ref, w_ref, o_ref):
    length = c_ref.shape[0]
    z = c_ref[...].astype(F32) * u_ref[...].astype(F32)
    row = lax.broadcasted_iota(jnp.int32, z.shape, 0)
    z_prev = jnp.where(row == 0, 0.0, pltpu.roll(z, 1, axis=0))
    z_next = jnp.where(row == length - 1, 0.0, pltpu.roll(z, length - 1, axis=0))
    conv = z_prev * w_ref[0:1, :] + z * w_ref[1:2, :] + z_next * w_ref[2:3, :]
    o_ref[...] = (b_ref[...].astype(F32) * conv).astype(o_ref.dtype)


def _short_conv(p, w_conv8, bsz, length, row0):
    blk = P_D // CONV_CH

    def rows(b):
        return row0 // length + b

    return pl.pallas_call(
        _conv_kernel,
        grid=(bsz,),
        in_specs=[
            pl.BlockSpec((length, CONV_CH), lambda b: (rows(b), blk)),
            pl.BlockSpec((length, CONV_CH), lambda b: (rows(b), blk + 1)),
            pl.BlockSpec((length, CONV_CH), lambda b: (rows(b), blk + 2)),
            pl.BlockSpec((8, CONV_CH), lambda b: (0, 0)),
        ],
        out_specs=pl.BlockSpec((length, CONV_CH), lambda b: (b, 0)),
        out_shape=jax.ShapeDtypeStruct((bsz * length, CONV_CH), BF16),
        compiler_params=_params("arbitrary"),
        name="short_conv",
    )(p, p, p, w_conv8)


MERGE_CHUNK = 512


def _merge_kernel(*refs, n_lat_tiles):
    o_ref, wb_ref = refs[-1], refs[-2]
    g_refs = refs[-2 - N_BRANCHES:-2]
    y_sets = [refs[s:s + N_BRANCHES] for s in range(0, len(refs) - 2 - N_BRANCHES, N_BRANCHES)]

    def emit(y_refs):
        for c in range(D_MODEL // MERGE_CHUNK):
            cols = slice(c * MERGE_CHUNK, (c + 1) * MERGE_CHUNK)
            acc = None
            for k, (y_ref, g_ref) in enumerate(zip(y_refs, g_refs)):
                proj = jnp.dot(y_ref[...], wb_ref[0, k, :, cols], preferred_element_type=F32)
                gate = 0.5 * jnp.tanh(0.5 * g_ref[:, cols].astype(F32)) + 0.5
                acc = gate * proj if acc is None else acc + gate * proj
            o_ref[:, cols] = acc.astype(o_ref.dtype)

    if len(y_sets) == 1:
        emit(y_sets[0])
    else:
        i = pl.program_id(0)

        @pl.when(i < n_lat_tiles)
        def _():
            emit(y_sets[0])

        @pl.when(i >= n_lat_tiles)
        def _():
            emit(y_sets[1])


def _merge(ys_lat, ys_ctx, p, w_branch, layer):
    n_lat = ys_lat[0].shape[0]
    n_act = n_lat + (ys_ctx[0].shape[0] if ys_ctx else 0)
    tm = _pick(ys_ctx[0].shape[0] if ys_ctx else n_lat, (512, 256))
    n_lat_tiles = n_lat // tm
    y_specs = [pl.BlockSpec((tm, BRANCH_WIDTH), lambda i: (jnp.minimum(i, n_lat_tiles - 1), 0))] * N_BRANCHES
    if ys_ctx:
        y_specs += [pl.BlockSpec((tm, BRANCH_WIDTH), lambda i: (jnp.maximum(i - n_lat_tiles, 0), 0))] * N_BRANCHES
    g_specs = [pl.BlockSpec((tm, D_MODEL), functools.partial(
        lambda i, k: (i, P_G // D_MODEL + k), k=k)) for k in range(N_BRANCHES)]
    return pl.pallas_call(
        functools.partial(_merge_kernel, n_lat_tiles=n_lat_tiles),
        grid=(n_act // tm,),
        in_specs=y_specs + g_specs + [
            pl.BlockSpec((1, N_BRANCHES, BRANCH_WIDTH, D_MODEL), lambda i: (layer, 0, 0, 0))],
        out_specs=pl.BlockSpec((tm, D_MODEL), lambda i: (i, 0)),
        out_shape=jax.ShapeDtypeStruct((n_act, D_MODEL), BF16),
        compiler_params=_params("arbitrary"),
        name="branch_merge",
    )(*ys_lat, *ys_ctx, p, p, p, p, w_branch)


R_E, R_W, R_RANK = 0, 2, 4


def _router_kernel(x_ref, g_ref, sh_ref, sc_ref, wr_ref, br_ref, h_ref, route_ref, counts_ref, run_ref,
                   *, tm, bsz):
    step = pl.program_id(0)
    r = _mod_row(step, tm, bsz)
    h = _modulated(x_ref[...], g_ref[...], sh_ref, sc_ref, r)
    h_ref[...] = h
    h_hi = h.astype(BF16)
    h_lo = (h - h_hi.astype(F32)).astype(BF16)
    logits = jnp.dot(h_hi, wr_ref[0], preferred_element_type=F32) + (
        jnp.dot(h_lo, wr_ref[0], preferred_element_type=F32) + jnp.dot(h_hi, wr_ref[1], preferred_element_type=F32))
    bias = br_ref[...]
    lane = lax.broadcasted_iota(jnp.int32, logits.shape, 1)
    lane_f = lane.astype(F32)
    neg = -jnp.inf

    def rmax(x, mask):
        return jnp.max(jnp.where(mask, x, neg), axis=-1, keepdims=True)

    def first_at(x, val, mask):
        return jnp.min(jnp.where(mask & (x == val), lane_f, 2.0 * LANES), axis=-1, keepdims=True)

    def pick(x, idx):
        return jnp.sum(jnp.where(lane_f == idx, x, 0.0), axis=-1, keepdims=True)

    gmask = lane < N_GROUPS
    ge = jnp.where(gmask, jnp.exp(logits - rmax(logits, gmask)), 0.0)
    g_prob = ge / jnp.sum(ge, axis=-1, keepdims=True)
    g_score = g_prob + bias
    g_sel = first_at(g_score, rmax(g_score, gmask), gmask)
    p_group = pick(g_prob, g_sel)

    group_of_lane = lax.shift_right_arithmetic(lane - N_EXPERTS, 3).astype(F32)
    emask = (lane >= N_EXPERTS) & (group_of_lane == g_sel)
    ee = jnp.where(emask, jnp.exp(logits - rmax(logits, emask)), 0.0)
    e_prob = ee / jnp.sum(ee, axis=-1, keepdims=True)
    e_score = e_prob + bias
    i1 = first_at(e_score, rmax(e_score, emask), emask)
    mask2 = emask & (lane_f != i1)
    i2 = first_at(e_score, rmax(e_score, mask2), mask2)
    p1 = pick(e_prob, i1)
    p2 = pick(e_prob, i2)
    w1 = p_group * p1 / (p1 + p2)
    w2 = p_group * p2 / (p1 + p2)
    e1 = i1 - N_EXPERTS
    e2 = i2 - N_EXPERTS

    @pl.when(step == 0)
    def _():
        run_ref[...] = jnp.zeros_like(run_ref)

    oh1 = lane_f == e1
    oh2 = lane_f == e2
    both = oh1.astype(F32) + oh2.astype(F32)
    tri = (lax.broadcasted_iota(jnp.int32, (tm, tm), 1) < lax.broadcasted_iota(jnp.int32, (tm, tm), 0)).astype(BF16)
    before = jnp.dot(tri, both.astype(BF16), preferred_element_type=F32) + run_ref[0:1, :]
    rank1 = jnp.sum(jnp.where(oh1, before, 0.0), axis=-1, keepdims=True)
    rank2 = jnp.sum(jnp.where(oh2, before + oh1.astype(F32), 0.0), axis=-1, keepdims=True)
    run_ref[...] = run_ref[...] + jnp.sum(both, axis=0, keepdims=True)
    counts_ref[...] = run_ref[...]

    vals = (e1, e2, w1, w2, rank1, rank2)
    out = jnp.zeros_like(logits)
    for k, v in enumerate(vals):
        out = jnp.where(lane == k, v, out)
    route_ref[...] = out


def _ffn_input_and_route(x1, gain, mod, w_route, b_route, bsz):
    n = x1.shape[0]
    tm = _pick(n, (512, 256))
    rows = mod.shape[0]
    return pl.pallas_call(
        functools.partial(_router_kernel, tm=tm, bsz=bsz),
        grid=(n // tm,),
        in_specs=[
            pl.BlockSpec((tm, D_MODEL), lambda i: (i, 0)),
            pl.BlockSpec((1, D_MODEL), lambda i: (0, 0)),
            pl.BlockSpec((rows, D_MODEL), lambda i: (0, 3)),
            pl.BlockSpec((rows, D_MODEL), lambda i: (0, 4)),
            pl.BlockSpec((2, D_MODEL, LANES), lambda i: (0, 0, 0)),
            pl.BlockSpec((1, LANES), lambda i: (0, 0)),
        ],
        out_specs=[pl.BlockSpec((tm, D_MODEL), lambda i: (i, 0)),
                   pl.BlockSpec((tm, LANES), lambda i: (i, 0)),
                   pl.BlockSpec((8, LANES), lambda i: (0, 0))],
        out_shape=[jax.ShapeDtypeStruct((n, D_MODEL), F32),
                   jax.ShapeDtypeStruct((n, LANES), F32),
                   jax.ShapeDtypeStruct((8, LANES), F32)],
        scratch_shapes=[pltpu.VMEM((8, LANES), F32)],
        compiler_params=_params("arbitrary"),
        name="ffn_input_route",
    )(x1, gain.reshape(1, D_MODEL), mod, mod, w_route, b_route)


def _dispatch_indices(route, counts_row, n_tok):
    counts = counts_row[0, :N_EXPERTS].astype(jnp.int32)
    padded = (counts + MOE_BLOCK - 1) // MOE_BLOCK * MOE_BLOCK
    pad_end = jnp.cumsum(padded)
    pad_start = pad_end - padded
    eid = route[:, R_E:R_E + TOP_K].astype(jnp.int32)
    rank = route[:, R_RANK:R_RANK + TOP_K].astype(jnp.int32)
    onehot = eid[:, :, None] == jnp.arange(N_EXPERTS, dtype=jnp.int32)[None, None, :]
    pos = (jnp.sum(jnp.where(onehot, pad_start[None, None, :], 0), axis=-1) + rank).reshape(-1)
    n_blocks = (n_tok * TOP_K + N_EXPERTS * (MOE_BLOCK - 1)) // MOE_BLOCK
    blk_start = jnp.arange(n_blocks, dtype=jnp.int32) * MOE_BLOCK
    blk_e = jnp.minimum(jnp.sum(pad_end[None, :] <= blk_start[:, None], axis=1), N_EXPERTS - 1).astype(jnp.int32)
    n_used = (pad_end[-1:] // MOE_BLOCK).astype(jnp.int32)
    last_blk = jnp.where(counts > 0, pad_end // MOE_BLOCK - 1, -1).astype(jnp.int32)
    blk = jnp.arange(n_blocks, dtype=jnp.int32)
    prev_e = jnp.concatenate([jnp.full((1,), -1, jnp.int32), blk_e[:-1]])
    first = (blk < n_used) & (blk_e != prev_e)
    ordinal = jnp.cumsum(first.astype(jnp.int32)) - 1
    n_first = jnp.sum(first.astype(jnp.int32))
    k = jnp.arange(N_EXPERTS, dtype=jnp.int32)
    expert_at = jnp.sum(jnp.where(first[None, :] & (ordinal[None, :] == k[:, None]), blk_e[None, :], 0), axis=1)

    def expert_after(d):
        target = ordinal + d
        e = jnp.sum(jnp.where(k[None, :] == target[:, None], expert_at[None, :], 0), axis=1)
        return jnp.where(target < n_first, e, -1).astype(jnp.int32)

    sched = (blk_e, n_used, first.astype(jnp.int32), (ordinal % MOE_W_SLOTS).astype(jnp.int32),
             expert_after(1), expert_after(2))
    return sched, last_blk, pos.astype(jnp.int32), n_blocks


def _row_copy(src, src_row, dst, dst_row, sem):
    return pltpu.make_async_copy(src.at[pl.ds(src_row, 1), :], dst.at[pl.ds(dst_row, 1), :], sem)


DISPATCH_CHUNK = 1024
DMA_UNROLL = 8


def _dispatch_kernel(pos_ref, last_ref, nused_ref, h_ref, xs_hbm, zbuf, sem, zsem):
    base = pl.program_id(0) * DISPATCH_CHUNK
    n_blocks = xs_hbm.shape[0] // MOE_BLOCK

    @pl.when(pl.program_id(0) == 0)
    def _():
        zbuf[...] = jnp.zeros_like(zbuf)

        def fill(b):
            return pltpu.make_async_copy(zbuf, xs_hbm.at[pl.ds(pl.multiple_of(b * MOE_BLOCK, MOE_BLOCK), MOE_BLOCK), :],
                                         zsem.at[0])

        def each(fn):
            def expert(e, carry):
                @pl.when(last_ref[e] >= 0)
                def _():
                    fn(fill(last_ref[e]))
                return carry

            def unused(b, carry):
                fn(fill(b))
                return carry

            lax.fori_loop(0, N_EXPERTS, expert, 0)
            lax.fori_loop(nused_ref[0], n_blocks, unused, 0)

        each(lambda c: c.start())
        each(lambda c: c.wait())

    def group(g, carry):
        row0 = pl.multiple_of(g * DMA_UNROLL, DMA_UNROLL)
        for j in range(DMA_UNROLL):
            for k in range(TOP_K):
                _row_copy(h_ref, row0 + j, xs_hbm, pos_ref[(base + row0 + j) * TOP_K + k], sem.at[0]).start()
        return carry

    lax.fori_loop(0, DISPATCH_CHUNK // DMA_UNROLL, group, 0)
    for k in range(TOP_K):
        pltpu.make_async_copy(h_ref, xs_hbm.at[pl.ds(0, DISPATCH_CHUNK), :], sem.at[0]).wait()


def _dispatch(h2, pos, last_blk, n_used, n_rows):
    n, width = h2.shape
    assert n % DISPATCH_CHUNK == 0
    grid_spec = pltpu.PrefetchScalarGridSpec(
        num_scalar_prefetch=3,
        grid=(n // DISPATCH_CHUNK,),
        in_specs=[pl.BlockSpec((DISPATCH_CHUNK, width), lambda i, *_: (i, 0))],
        out_specs=pl.BlockSpec(memory_space=pl.ANY),
        scratch_shapes=[pltpu.VMEM((MOE_BLOCK, width), h2.dtype), pltpu.SemaphoreType.DMA((1,)),
                        pltpu.SemaphoreType.DMA((1,))],
    )
    return pl.pallas_call(
        _dispatch_kernel,
        grid_spec=grid_spec,
        out_shape=jax.ShapeDtypeStruct((n_rows, width), h2.dtype),
        compiler_params=_params("arbitrary"),
        name="moe_dispatch",
    )(pos, last_blk, n_used, h2)


def _moe_kernel(be_ref, nused_ref, first_ref, slot_ref, nxt1_ref, nxt2_ref, x_ref, wg_hbm, wu_hbm, wd_hbm, o_ref,
                wgf, wuf, wdf, wgb, wub, wdb, sem, *, layer):
    i = pl.program_id(0)

    def w_copies(e, s):
        return (pltpu.make_async_copy(wg_hbm.at[layer, e], wgf.at[s], sem.at[s]),
                pltpu.make_async_copy(wu_hbm.at[layer, e], wuf.at[s], sem.at[s]),
                pltpu.make_async_copy(wd_hbm.at[layer, e], wdf.at[s], sem.at[s]))

    def start(e, s):
        for c in w_copies(e, s):
            c.start(priority=BULK_DMA_PRIORITY)

    @pl.when(first_ref[i] == 1)
    def _():
        s = slot_ref[i]

        @pl.when(i == 0)
        def _():
            start(be_ref[i], s)

            @pl.when(nxt1_ref[i] >= 0)
            def _():
                start(nxt1_ref[i], (s + 1) % MOE_W_SLOTS)

        @pl.when(nxt2_ref[i] >= 0)
        def _():
            start(nxt2_ref[i], (s + 2) % MOE_W_SLOTS)

        for c in w_copies(be_ref[i], s):
            c.wait()
        wgb[...] = wgf[s].astype(BF16)
        wub[...] = wuf[s].astype(BF16)
        wdb[...] = wdf[s].astype(BF16)

    @pl.when(i < nused_ref[0])
    def _():
        x = x_ref[...].astype(BF16)
        gate = jnp.dot(x, wgb[...], preferred_element_type=F32)
        up = jnp.dot(x, wub[...], preferred_element_type=F32)
        hid = (gate * jax.nn.sigmoid(gate) * up).astype(BF16)
        o_ref[...] = jnp.dot(hid, wdb[...], preferred_element_type=F32)

    @pl.when(i >= nused_ref[0])
    def _():
        o_ref[...] = jnp.zeros_like(o_ref)


def _moe(xs, sched, w_gate, w_up, w_down, layer):
    n_blocks = sched[0].shape[0]
    grid_spec = pltpu.PrefetchScalarGridSpec(
        num_scalar_prefetch=len(sched),
        grid=(n_blocks,),
        in_specs=[pl.BlockSpec((MOE_BLOCK, D_MODEL), lambda i, *_: (i, 0))] + [pl.BlockSpec(memory_space=pl.ANY)] * 3,
        out_specs=pl.BlockSpec((MOE_BLOCK, D_MODEL), lambda i, *_: (i, 0)),
        scratch_shapes=[
            pltpu.VMEM((MOE_W_SLOTS, D_MODEL, EXPERT_HIDDEN), F32),
            pltpu.VMEM((MOE_W_SLOTS, D_MODEL, EXPERT_HIDDEN), F32),
            pltpu.VMEM((MOE_W_SLOTS, EXPERT_HIDDEN, D_MODEL), F32),
            pltpu.VMEM((D_MODEL, EXPERT_HIDDEN), BF16),
            pltpu.VMEM((D_MODEL, EXPERT_HIDDEN), BF16),
            pltpu.VMEM((EXPERT_HIDDEN, D_MODEL), BF16),
            pltpu.SemaphoreType.DMA((MOE_W_SLOTS,)),
        ],
    )
    return pl.pallas_call(
        functools.partial(_moe_kernel, layer=layer),
        grid_spec=grid_spec,
        out_shape=jax.ShapeDtypeStruct((n_blocks * MOE_BLOCK, D_MODEL), F32),
        compiler_params=_params("arbitrary"),
        name="moe_experts",
    )(*sched, xs, w_gate, w_up, w_down)


def _combine_kernel(pos_ref, ys_hbm, x_ref, route_ref, ga_ref, gn_ref, *rest, tm, bsz, final):
    if final:
        o_ref, rbuf, sem = rest
    else:
        nsh_ref, nsc_ref, o_ref, h_ref, rbuf, sem = rest
    i = pl.program_id(0)
    n = pl.num_programs(0)
    slot = i % 2

    def gather(blk, s):
        def group(g, carry):
            row0 = pl.multiple_of(g * DMA_UNROLL, DMA_UNROLL)
            for j in range(DMA_UNROLL):
                for k in range(TOP_K):
                    _row_copy(ys_hbm, pos_ref[(blk * tm + row0 + j) * TOP_K + k], rbuf.at[s], k * tm + row0 + j,
                              sem.at[s]).start()
            return carry
        lax.fori_loop(0, tm // DMA_UNROLL, group, 0)

    @pl.when(i == 0)
    def _():
        gather(0, 0)

    @pl.when(i + 1 < n)
    def _():
        gather(i + 1, 1 - slot)

    pltpu.make_async_copy(ys_hbm.at[pl.ds(0, TOP_K * tm), :], rbuf.at[slot], sem.at[slot]).wait()
    y = rbuf[slot, 0:tm, :] * route_ref[:, R_W:R_W + 1] + rbuf[slot, tm:2 * tm, :] * route_ref[:, R_W + 1:R_W + 2]
    r = _mod_row(i, tm, bsz)
    x = x_ref[...] + ga_ref[pl.ds(r, 1), :] * y
    if final:
        o_ref[...] = _rms(x, gn_ref[...])
    else:
        o_ref[...] = x
        h_ref[...] = _modulated(x, gn_ref[...], nsh_ref, nsc_ref, r).astype(h_ref.dtype)


def _combine(ys, pos, x1, route, mod, gain, next_mod, bsz):
    n = x1.shape[0]
    tm = 256
    rows = mod.shape[0]
    final = next_mod is None
    tile = pl.BlockSpec((tm, D_MODEL), lambda i, pos: (i, 0))
    in_specs = [
        pl.BlockSpec(memory_space=pl.ANY),
        tile,
        pl.BlockSpec((tm, LANES), lambda i, pos: (i, 0)),
        pl.BlockSpec((rows, D_MODEL), lambda i, pos: (0, 5)),
        pl.BlockSpec((1, D_MODEL), lambda i, pos: (0, 0)),
    ]
    args = [pos, ys, x1, route, mod, gain.reshape(1, D_MODEL)]
    out_specs, out_shape = tile, jax.ShapeDtypeStruct((n, D_MODEL), F32)
    if not final:
        in_specs += [pl.BlockSpec((rows, D_MODEL), lambda i, pos: (0, 0)),
                     pl.BlockSpec((rows, D_MODEL), lambda i, pos: (0, 1))]
        args += [next_mod, next_mod]
        out_specs, out_shape = [tile, tile], [out_shape, jax.ShapeDtypeStruct((n, D_MODEL), BF16)]
    grid_spec = pltpu.PrefetchScalarGridSpec(
        num_scalar_prefetch=1,
        grid=(n // tm,),
        in_specs=in_specs,
        out_specs=out_specs,
        scratch_shapes=[pltpu.VMEM((2, TOP_K * tm, D_MODEL), F32), pltpu.SemaphoreType.DMA((2,))],
    )
    return pl.pallas_call(
        functools.partial(_combine_kernel, tm=tm, bsz=bsz, final=final),
        grid_spec=grid_spec,
        out_shape=out_shape,
        compiler_params=_params("arbitrary"),
        name="moe_combine",
    )(*args)


def _rotate_half_cols(w, quarter):
    r1, r2, c1, c2 = (w[..., k * quarter:(k + 1) * quarter] for k in range(4))
    return jnp.concatenate([-r2, r1, -c2, c1], axis=-1)


def _kpe_rows(wt):
    kpe = jnp.swapaxes(wt[:, MLA_Q_LORA + MLA_KV_LORA:A_COLS, :], 1, 2)
    both = jnp.concatenate([kpe, _rotate_half_cols(kpe, MLA_ROPE // 4)], axis=-1)
    return jnp.swapaxes(both, 1, 2).astype(BF16)


def _relayout_w_uq(w_uq):
    w = w_uq.reshape(MLA_Q_LORA, MLA_HEADS, MLA_NOPE + MLA_ROPE)
    nope, pe = w[..., :MLA_NOPE], w[..., MLA_NOPE:]
    z = jnp.zeros_like(pe)
    out = jnp.concatenate([nope, pe, z, _rotate_half_cols(pe, MLA_ROPE // 4), z], axis=-1)
    return out.reshape(MLA_Q_LORA, MLA_HEADS * 3 * LANES).astype(BF16)


def _relayout_w_ukv(w_ukv):
    w = w_ukv.reshape(MLA_KV_LORA, MLA_HEADS, MLA_NOPE + MLA_V)
    return jnp.concatenate([w[..., :MLA_NOPE].reshape(MLA_KV_LORA, -1),
                            w[..., MLA_NOPE:].reshape(MLA_KV_LORA, -1)], axis=-1).astype(BF16)


def _rope_tables(tm):
    t = jnp.arange(SEQ)
    rows = (t // GRID_W).astype(F32)
    cols = (t % GRID_W).astype(F32)

    def cs(rot_dim):
        quarter = rot_dim // 4
        inv_freq = ROPE_THETA ** (-jnp.arange(quarter, dtype=F32) / quarter)
        ang_r = rows[:, None] * inv_freq[None, :]
        ang_c = cols[:, None] * inv_freq[None, :]
        cos = jnp.concatenate([jnp.cos(ang_r)] * 2 + [jnp.cos(ang_c)] * 2, axis=-1)
        sin = jnp.concatenate([jnp.sin(ang_r)] * 2 + [jnp.sin(ang_c)] * 2, axis=-1)
        return cos, sin

    def with_identity(tab, ident):
        return jnp.concatenate([tab, jnp.broadcast_to(ident, (tm, LANES))], axis=0)

    cos_a, sin_a = cs(MLA_ROPE)
    zeros = jnp.zeros((SEQ, LANES - MLA_ROPE), F32)
    lane = jnp.arange(LANES)
    one_lo = (lane < MLA_ROPE).astype(F32)[None, :]
    cosa = with_identity(jnp.concatenate([cos_a, zeros], axis=1), one_lo)
    sina = with_identity(jnp.concatenate([sin_a, zeros], axis=1), jnp.zeros((1, LANES), F32))
    csa = with_identity(jnp.concatenate([cos_a, sin_a], axis=1), one_lo)
    cos_b, sin_b = cs(GQA_HEAD_DIM)
    sign = jnp.where((lane % (GQA_HEAD_DIM // 2)) < GQA_HEAD_DIM // 4, -1.0, 1.0).astype(F32)[None, :]
    cosb = with_identity(cos_b, jnp.ones((1, LANES), F32))
    sinb = with_identity(sin_b * sign, jnp.zeros((1, LANES), F32))
    return cosa, sina, csa, cosb, sinb


def kernel(x, c, ctx, c_ctx, w_mod, b_mod, g_mix, g_ffn, w_in, w_uq, g_qa, w_ukv, g_kva, g_qn, g_kn, rpb, w_conv,
           w_branch, w_o, w_group, b_group, w_router, b_router, w_gate_e, w_up_e, w_down_e, g_final):
    bsz = x.shape[0]
    n_lat = bsz * SEQ
    n_ctx = bsz * CTX_LEN
    n_all = n_lat + n_ctx
    mod_rows = -(-(bsz + 1) // 8) * 8

    c_all = jnp.concatenate([c, c_ctx[None, :], jnp.zeros((mod_rows - bsz - 1, D_MODEL), F32)], axis=0)
    mods = _modulation(c_all, bsz + 1, w_mod, b_mod)
    tables = _rope_tables(_pick(n_all, (512, 256)))

    w_in_t = jnp.swapaxes(w_in, 1, 2)
    w_kpe_t = _kpe_rows(w_in_t)
    w_branch_b = w_branch.astype(BF16)
    w_o_b = w_o.astype(BF16)

    for l in range(DEPTH):
        last = l == DEPTH - 1
        n_act = n_lat if last else n_all
        mod = mods[l]
        w_uq_x = _relayout_w_uq(w_uq[l])
        w_ukv_r = _relayout_w_ukv(w_ukv[l])
        w_route = jnp.concatenate([w_group[l], jnp.zeros((D_MODEL, N_EXPERTS - N_GROUPS), F32), w_router[l]], axis=1)
        w_route_hi = w_route.astype(BF16)
        w_route = jnp.stack([w_route_hi, (w_route - w_route_hi.astype(F32)).astype(BF16)])
        b_route = jnp.concatenate([b_group[l], jnp.zeros((N_EXPERTS - N_GROUPS,), F32), b_router[l]])[None, :]
        w_conv8 = jnp.concatenate([w_conv[l], jnp.zeros((8 - CONV_WIDTH, CONV_CH), F32)], axis=0)

        if l == 0:
            h, x_all = _hmod(x.reshape(n_lat, D_MODEL), ctx.reshape(n_ctx, D_MODEL), g_mix[l], mod, bsz)
        p = _in_proj(h, w_in_t, l)
        pk = _matmul_nt(h, w_kpe_t, l, BF16, "kpe_proj")
        qa, ka, va, qb, kb = _prep(p, pk, tables, w_uq_x, w_ukv_r, g_qa[l][None, :], g_kva[l][None, :],
                                   g_qn[l][None, :], g_kn[l][None, :])

        lat_ctx = lambda k_arr, k_off, v_arr, v_off: [(k_arr, k_off, v_arr, v_off, SEQ, 0),
                                                      (k_arr, k_off, v_arr, v_off, CTX_LEN, n_lat)]
        ctx_only = lambda k_arr, k_off, v_arr, v_off: [(k_arr, k_off, v_arr, v_off, CTX_LEN, n_lat)]
        mla = dict(n_groups=MLA_HEADS, rep=1, dk=MLA_DK_PAD, dv=MLA_V, scale=1.0)
        gqa = dict(n_groups=GQA_KV_HEADS, rep=GQA_HEADS // GQA_KV_HEADS, dk=GQA_HEAD_DIM, dv=GQA_HEAD_DIM, scale=1.0)
        nac = dict(n_groups=NA_HEADS, rep=1, dk=NA_HEAD_DIM, dv=NA_HEAD_DIM, scale=NA_SCALE)

        ys_lat = [
            _attention(qa, 0, lat_ctx(ka, 0, va, 0), bsz, SEQ, 0, name="mla_attention", **mla),
            _attention(qb, 0, lat_ctx(kb, 0, p, P_B + 768), bsz, SEQ, 0, name="gqa_attention", **gqa),
            _na_attention(p, _na_bias(rpb[l]), bsz, n_lat),
            _short_conv(p, w_conv8, bsz, SEQ, 0),
        ]
        ys_ctx = []
        if not last:
            ys_ctx = [
                _attention(qa, 0, ctx_only(ka, 0, va, 0), bsz, CTX_LEN, n_lat, name="mla_attention_ctx", **mla),
                _attention(qb, 0, ctx_only(kb, 0, p, P_B + 768), bsz, CTX_LEN, n_lat,
                           name="gqa_attention_ctx", **gqa),
                _attention(p, P_C, ctx_only(p, P_C + 512, p, P_C + 1024), bsz, CTX_LEN, n_lat,
                           name="na_attention_ctx", **nac),
                _short_conv(p, w_conv8, bsz, CTX_LEN, n_lat),
            ]

        mixed = _merge(ys_lat, ys_ctx, p, w_branch_b, l)
        x1 = _out_proj_residual(mixed, w_o_b, l, x_all, mod, bsz)
        h2, route, counts = _ffn_input_and_route(x1, g_ffn[l], mod, w_route, b_route, bsz)
        sched, last_blk, pos, n_blocks = _dispatch_indices(route, counts, n_act)
        xs = _dispatch(h2, pos, last_blk, sched[1], n_blocks * MOE_BLOCK)
        ys = _moe(xs, sched, w_gate_e, w_up_e, w_down_e, l)
        if last:
            x_all = _combine(ys, pos, x1, route, mod, g_final, None, bsz)
        else:
            x_all, h = _combine(ys, pos, x1, route, mod, g_mix[l + 1], mods[l + 1], bsz)
    return x_all.reshape(bsz, SEQ, D_MODEL)
```
